```python
import math
import jax, jax.numpy as jnp
from jax import lax
import numpy as np

D_MODEL = 2048
BATCH = 2
SEQ = 8192
DEPTH = 1
DEC_BATCH = 32
DEC_SEQ = 16
PAST_LEN = 1024

CHUNK = 64
Q_BLOCK = 128
HEAD_DIM = 128
N_DIFF_HEADS = D_MODEL // (2 * HEAD_DIM)
N_FOX_HEADS = D_MODEL // (2 * HEAD_DIM)
DIFF_QK_DIM = HEAD_DIM // 2
DIFF_W = N_DIFF_HEADS * HEAD_DIM
FOX_W = N_FOX_HEADS * HEAD_DIM
D_FF = 4 * D_MODEL
CONV_WIDTH = 3
REL_BUCKETS = 32
REL_MAX_DIST = 128
N_IN = 3 * DIFF_W + 3 * FOX_W + N_FOX_HEADS + 2 * D_MODEL
EPS = 1e-6
NEG_INF = -1e30

kernel_name = "streaming_diff_fox_hybrid_step"


def _rms(x, g):
    xf = x.astype(jnp.float32)
    y = xf * lax.rsqrt(jnp.mean(xf * xf, axis=-1, keepdims=True) + EPS)
    return (y * g.astype(jnp.float32)).astype(x.dtype)


def _t5_bucket(rel):
    nb = REL_BUCKETS // 2
    max_exact = nb // 2
    n = jnp.abs(rel)
    nf = jnp.maximum(n, 1).astype(jnp.float32)
    large = max_exact + (jnp.log(nf / max_exact) / math.log(REL_MAX_DIST / max_exact)
                         * (nb - max_exact)).astype(jnp.int32)
    large = jnp.minimum(large, nb - 1)
    return jnp.where(rel > 0, nb, 0) + jnp.where(n < max_exact, n, large)


def _lambda(lq1, lk1, lq2, lk2, lam_init):
    f = lambda a: a.astype(jnp.float32)
    return jnp.exp(jnp.sum(f(lq1) * f(lk1))) - jnp.exp(jnp.sum(f(lq2) * f(lk2))) + lam_init


def _project(xn, w_in, b_forget):
    B, T, _ = xn.shape
    z = xn @ w_in
    cuts = np.cumsum([DIFF_W, DIFF_W, DIFF_W, FOX_W, FOX_W, FOX_W, N_FOX_HEADS]).tolist()
    qd, kd, vd, qf, kf, vf, fl, gl = jnp.split(z, cuts, axis=-1)
    heads = lambda a: a.reshape(B, T, -1, HEAD_DIM)
    logf = jax.nn.log_sigmoid((fl + b_forget).astype(jnp.float32))
    gates = jax.nn.sigmoid(gl)
    return heads(qd), heads(kd), heads(vd), heads(qf), heads(kf), heads(vf), logf, gates


def _attend(qd, qf, fq, qpos, kd, vd, kf, vf, fk, kpos, rel_table, lam):
    f32 = jnp.float32
    rel = kpos[None, :] - qpos[:, None]
    bias = jnp.transpose(rel_table.astype(f32)[_t5_bucket(rel)], (2, 0, 1))[None]
    chunk_ok = (kpos[None, :] // CHUNK) <= (qpos[:, None] // CHUNK)
    sd = DIFF_QK_DIM ** -0.5
    s1 = jnp.einsum('bqhd,bkhd->bhqk', qd[..., :DIFF_QK_DIM], kd[..., :DIFF_QK_DIM],
                    preferred_element_type=f32) * sd + bias
    s2 = jnp.einsum('bqhd,bkhd->bhqk', qd[..., DIFF_QK_DIM:], kd[..., DIFF_QK_DIM:],
                    preferred_element_type=f32) * sd + bias
    a1 = jax.nn.softmax(jnp.where(chunk_ok, s1, NEG_INF), axis=-1)
    a2 = jax.nn.softmax(jnp.where(chunk_ok, s2, NEG_INF), axis=-1)
    od = jnp.einsum('bhqk,bkhd->bqhd', (a1 - lam * a2).astype(vd.dtype), vd)
    tok_ok = kpos[None, :] <= qpos[:, None]
    decay = jnp.swapaxes(fq, 1, 2)[..., :, None] - jnp.swapaxes(fk, 1, 2)[..., None, :]
    sf = jnp.einsum('bqhd,bkhd->bhqk', qf, kf, preferred_element_type=f32) * HEAD_DIM ** -0.5 + decay
    af = jax.nn.softmax(jnp.where(tok_ok, sf, NEG_INF), axis=-1)
    of = jnp.einsum('bhqk,bkhd->bqhd', af.astype(vf.dtype), vf)
    return od, of


def _prompt_attention(qd, kd, vd, qf, kf, vf, logf, rel_table, lam):
    B, S = qd.shape[:2]
    F = jnp.cumsum(logf, axis=1)
    pos = jnp.arange(S)

    def blk(i):
        qs = i * Q_BLOCK
        sl = lambda a: lax.dynamic_slice_in_dim(a, qs, Q_BLOCK, axis=1)
        return _attend(sl(qd), sl(qf), sl(F), qs + jnp.arange(Q_BLOCK),
                       kd, vd, kf, vf, F, pos, rel_table, lam)

    od, of = lax.map(blk, jnp.arange(S // Q_BLOCK))
    unblock = lambda a: jnp.swapaxes(a, 0, 1).reshape(B, S, a.shape[3], a.shape[4])
    return unblock(od), unblock(of)


def _merge(od, of, gates, subln, lam_init, w_bd, w_bf, w_out):
    B, T = od.shape[:2]
    od = _rms(od, subln) * (1.0 - lam_init)
    yd = od.reshape(B, T, DIFF_W) @ w_bd
    yf = of.reshape(B, T, FOX_W) @ w_bf
    ga, gb = jnp.split(gates, 2, axis=-1)
    return (ga * yd + gb * yf) @ w_out


def _conv_ffn(xn, conv_prev, w_up, conv_w, conv_b, w_down):
    T = xn.shape[1]
    a, b = jnp.split(xn @ w_up, 2, axis=-1)
    a_ext = jnp.concatenate([conv_prev.astype(a.dtype), a], axis=1)
    ac = sum(conv_w[k] * a_ext[:, k:k + T] for k in range(CONV_WIDTH)) + conv_b
    return (jax.nn.gelu(ac, approximate=True) * b) @ w_down, a_ext[:, T:]


def _layer(h, past, rel_table, lam, lam_init, pre1, w_in, b_forget, subln, w_bd, w_bf,
           w_out, post1, pre2, w_up, conv_w, conv_b, w_down, post2):
    B, T, _ = h.shape
    qd, kd, vd, qf, kf, vf, logf, gates = _project(_rms(h, pre1), w_in, b_forget)
    if past is None:
        od, of = _prompt_attention(qd, kd, vd, qf, kf, vf, logf, rel_table, lam)
        conv_prev = jnp.zeros((B, CONV_WIDTH - 1, D_FF), h.dtype)
    else:
        pk, pv, pfk, pfv, plogf, conv_prev = past
        P = pk.shape[1]
        cat = lambda c, n: jnp.concatenate([c.astype(n.dtype), n], axis=1)
        F = jnp.cumsum(jnp.concatenate([plogf.astype(jnp.float32), logf], axis=1), axis=1)
        od, of = _attend(qd, qf, F[:, P:], P + jnp.arange(T), cat(pk, kd), cat(pv, vd),
                         cat(pfk, kf), cat(pfv, vf), F, jnp.arange(P + T), rel_table, lam)
    h = h + _rms(_merge(od, of, gates, subln, lam_init, w_bd, w_bf, w_out), post1)
    ff, conv_state = _conv_ffn(_rms(h, pre2), conv_prev, w_up, conv_w, conv_b, w_down)
    h = h + _rms(ff, post2)
    return h, (kd, vd, kf, vf, logf, conv_state)


def setup_inputs(seed: int = 0) -> dict:
    key = jax.random.key(seed)
    ks = jax.random.split(key, 32)
    nrm = lambda k, shape, s=1.0: s * jax.random.normal(k, shape, jnp.float32)
    gain = lambda k, shape: 1.0 + 0.05 * jax.random.normal(k, shape, jnp.float32)
    return {
        "x_prompt": nrm(ks[0], (BATCH, SEQ, D_MODEL)),
        "x_sample": nrm(ks[1], (DEC_BATCH, DEC_SEQ, D_MODEL)),
        "cache_diff_k": nrm(ks[2], (DEPTH, DEC_BATCH, PAST_LEN, N_DIFF_HEADS, HEAD_DIM)),
        "cache_diff_v": nrm(ks[3], (DEPTH, DEC_BATCH, PAST_LEN, N_DIFF_HEADS, HEAD_DIM)),
        "cache_fox_k": nrm(ks[4], (DEPTH, DEC_BATCH, PAST_LEN, N_FOX_HEADS, HEAD_DIM)),
        "cache_fox_v": nrm(ks[5], (DEPTH, DEC_BATCH, PAST_LEN, N_FOX_HEADS, HEAD_DIM)),
        "cache_fox_logf": jax.nn.log_sigmoid(nrm(ks[6], (DEPTH, DEC_BATCH, PAST_LEN, N_FOX_HEADS))),
        "state_ffn_conv": nrm(ks[7], (DEPTH, DEC_BATCH, CONV_WIDTH - 1, D_FF)),
        "rel_table": nrm(ks[8], (REL_BUCKETS, N_DIFF_HEADS), 0.5),
        "pre_norm1": gain(ks[9], (DEPTH, D_MODEL)),
        "w_in": nrm(ks[10], (DEPTH, D_MODEL, N_IN), D_MODEL ** -0.5),
        "b_forget": nrm(ks[11], (DEPTH, N_FOX_HEADS), 0.1),
        "lam_q1": nrm(ks[12], (DEPTH, DIFF_QK_DIM), 0.1),
        "lam_k1": nrm(ks[13], (DEPTH, DIFF_QK_DIM), 0.1),
        "lam_q2": nrm(ks[14], (DEPTH, DIFF_QK_DIM), 0.1),
        "lam_k2": nrm(ks[15], (DEPTH, DIFF_QK_DIM), 0.1),
        "diff_subln": gain(ks[16], (DEPTH, HEAD_DIM)),
        "w_branch_diff": nrm(ks[17], (DEPTH, DIFF_W, D_MODEL), DIFF_W ** -0.5),
        "w_branch_fox": nrm(ks[18], (DEPTH, FOX_W, D_MODEL), FOX_W ** -0.5),
        "w_out": nrm(ks[19], (DEPTH, D_MODEL, D_MODEL), D_MODEL ** -0.5),
        "post_norm1": gain(ks[20], (DEPTH, D_MODEL)),
        "pre_norm2": gain(ks[21], (DEPTH, D_MODEL)),
        "w_up": nrm(ks[22], (DEPTH, D_MODEL, 2 * D_FF), D_MODEL ** -0.5),
        "conv_w": nrm(ks[23], (DEPTH, CONV_WIDTH, D_FF), CONV_WIDTH ** -0.5),
        "conv_b": nrm(ks[24], (DEPTH, D_FF), 0.01),
        "w_down": nrm(ks[25], (DEPTH, D_FF, D_MODEL), D_FF ** -0.5),
        "post_norm2": gain(ks[26], (DEPTH, D_MODEL)),
    }


def reference(x_prompt, x_sample, cache_diff_k, cache_diff_v, cache_fox_k, cache_fox_v,
              cache_fox_logf, state_ffn_conv, rel_table, pre_norm1, w_in, b_forget,
              lam_q1, lam_k1, lam_q2, lam_k2, diff_subln, w_branch_diff, w_branch_fox,
              w_out, post_norm1, pre_norm2, w_up, conv_w, conv_b, w_down, post_norm2):
    hp, hs = x_prompt, x_sample
    new_p, new_s = [], []
    for l in range(DEPTH):
        lam_init = 0.8 - 0.6 * math.exp(-0.3 * l)
        lam = _lambda(lam_q1[l], lam_k1[l], lam_q2[l], lam_k2[l], lam_init)
        params = (pre_norm1[l], w_in[l], b_forget[l], diff_subln[l], w_branch_diff[l],
                  w_branch_fox[l], w_out[l], post_norm1[l], pre_norm2[l], w_up[l],
                  conv_w[l], conv_b[l], w_down[l], post_norm2[l])
        hp, sp = _layer(hp, None, rel_table, lam, lam_init, *params)
        past = (cache_diff_k[l], cache_diff_v[l], cache_fox_k[l], cache_fox_v[l],
                cache_fox_logf[l], state_ffn_conv[l])
        hs, ss = _layer(hs, past, rel_table, lam, lam_init, *params)
        new_p.append(sp)
        new_s.append(ss)
    st = lambda lst, i: jnp.stack([e[i] for e in lst])
    return (hp, hs,
            st(new_p, 0), st(new_p, 1), st(new_p, 2), st(new_p, 3), st(new_p, 4), st(new_p, 5),
            st(new_s, 0), st(new_s, 1), st(new_s, 2), st(new_s, 3), st(new_s, 4), st(new_s, 5))
```

```python
import functools
import math

import numpy as np
import jax
import jax.numpy as jnp
from jax import lax
from jax.experimental import pallas as pl
from jax.experimental.pallas import tpu as pltpu

HEAD_DIM = 128
DIFF_QK_DIM = HEAD_DIM // 2
CHUNK = 64
CONV_WIDTH = 3
REL_BUCKETS = 32
REL_MAX_DIST = 128
EPS = 1e-6
NEG_INF = -1e30

LANES = 128
SUBLANES = 8
VMEM_LIMIT_BYTES = 60 * 1024 * 1024

F32 = jnp.float32
BF16 = jnp.bfloat16


def _params(*sem):
    return pltpu.CompilerParams(dimension_semantics=sem, vmem_limit_bytes=VMEM_LIMIT_BYTES)


def _rms_rows(x, g):
    return x * lax.rsqrt(jnp.mean(x * x, axis=-1, keepdims=True) + EPS) * g


def _dot(a, b):
    return jnp.dot(a, b, preferred_element_type=F32)


def _dot_nt(a, b):
    return lax.dot_general(a, b, (((1,), (1,)), ((), ())), preferred_element_type=F32)


def _inproj_kernel(x_ref, g_ref, w_ref, cs_ref, wf_ref, bf_ref,
                   zb_ref, kd_ref, vd_ref, kf_ref, vf_ref, logf_ref, flp_ref, xn_ref, *, n_heads):
    j = pl.program_id(1)

    @pl.when(j == 0)
    def _():
        xn = _rms_rows(x_ref[...], g_ref[...]).astype(BF16)
        xn_ref[...] = xn
        fl = _dot(xn, wf_ref[...]) + bf_ref[...]
        lf = jnp.minimum(fl, 0.0) - jnp.log1p(jnp.exp(-jnp.abs(fl)))
        flp_ref[...] = lf
        logf_ref[...] = lf[:, :n_heads]

    z = _dot(xn_ref[...], w_ref[...])
    zb_ref[...] = (z * cs_ref[...]).astype(BF16)
    for seg, ref in ((1, kd_ref), (2, vd_ref), (4, kf_ref), (5, vf_ref)):
        @pl.when(j == seg)
        def _(ref=ref):
            ref[...] = z


def _inproj(x, g, w_qkv, colscale, w_f, b_f, n_heads, tm):
    m, d = x.shape
    dw = w_qkv.shape[1] // 6
    row = lambda i, j: (i, 0)
    f32_out = jax.ShapeDtypeStruct((m, dw), F32)
    return pl.pallas_call(
        functools.partial(_inproj_kernel, n_heads=n_heads),
        grid=(m // tm, 6),
        in_specs=[
            pl.BlockSpec((tm, d), row),
            pl.BlockSpec((1, d), lambda i, j: (0, 0)),
            pl.BlockSpec((d, dw), lambda i, j: (0, j)),
            pl.BlockSpec((1, dw), lambda i, j: (0, j)),
            pl.BlockSpec((d, LANES), lambda i, j: (0, 0)),
            pl.BlockSpec((1, LANES), lambda i, j: (0, 0)),
        ],
        out_specs=[
            pl.BlockSpec((tm, dw), lambda i, j: (i, j)),
            pl.BlockSpec((tm, dw), row),
            pl.BlockSpec((tm, dw), row),
            pl.BlockSpec((tm, dw), row),
            pl.BlockSpec((tm, dw), row),
            pl.BlockSpec((tm, n_heads), row),
            pl.BlockSpec((tm, LANES), row),
        ],
        out_shape=[
            jax.ShapeDtypeStruct((m, 6 * dw), BF16),
            f32_out, f32_out, f32_out, f32_out,
            jax.ShapeDtypeStruct((m, n_heads), F32),
            jax.ShapeDtypeStruct((m, LANES), F32),
        ],
        scratch_shapes=[pltpu.VMEM((tm, d), BF16)],
        compiler_params=_params("arbitrary", "arbitrary"),
        name="inproj",
    )(x, g, w_qkv, colscale, w_f, b_f)


def _split3(x):
    hi = x.astype(BF16)
    r1 = x - hi.astype(F32)
    mid = r1.astype(BF16)
    lo = (r1 - mid.astype(F32)).astype(BF16)
    return hi, mid, lo


def _cumsum_kernel(x_ref, o_ref, *, chunk, n_heads):
    t = x_ref.shape[0]
    r = lax.broadcasted_iota(jnp.int32, (chunk, chunk), 0)
    c = lax.broadcasted_iota(jnp.int32, (chunk, chunk), 1)
    tri = (r >= c).astype(BF16)
    carry = jnp.zeros((1, LANES), F32)
    for k in range(t // chunk):
        hi, mid, lo = _split3(x_ref[k * chunk:(k + 1) * chunk, :])
        f = _dot(tri, hi) + _dot(tri, mid) + _dot(tri, lo) + carry
        carry = f[chunk - 1:chunk, :]
        o_ref[:, k * chunk:(k + 1) * chunk] = f.T[:n_heads, :]


def _cumsum_rows(flp, n_heads, chunk):
    nb, t, _ = flp.shape
    return pl.pallas_call(
        functools.partial(_cumsum_kernel, chunk=chunk, n_heads=n_heads),
        grid=(nb,),
        in_specs=[pl.BlockSpec((None, t, LANES), lambda b: (b, 0, 0))],
        out_specs=pl.BlockSpec((None, n_heads, t), lambda b: (b, 0, 0)),
        out_shape=jax.ShapeDtypeStruct((nb, n_heads, t), F32),
        compiler_params=_params("arbitrary"),
        name="cumsum_rows",
    )(flp)


def _t5_bucket(rel):
    nb = REL_BUCKETS // 2
    max_exact = nb // 2
    n = jnp.abs(rel)
    nf = jnp.maximum(n, 1).astype(jnp.float32)
    large = max_exact + (jnp.log(nf / max_exact) / math.log(REL_MAX_DIST / max_exact)
                         * (nb - max_exact)).astype(jnp.int32)
    large = jnp.minimum(large, nb - 1)
    return jnp.where(rel > 0, nb, 0) + jnp.where(n < max_exact, n, large)


def _bias_kernel(table_ref, idx_ref, mask_ref, o_ref):
    h = pl.program_id(0)
    idx = idx_ref[...]
    acc = mask_ref[...]
    for b in range(REL_BUCKETS):
        acc = acc + jnp.where(idx == b, table_ref[b, h], 0.0)
    o_ref[...] = acc


def _bias_tiles(rel_table, qpos, kpos):
    n_heads = rel_table.shape[1]
    nt, r = qpos.shape
    c = kpos.shape[1]
    rel = kpos[:, None, :] - qpos[:, :, None]
    idx = _t5_bucket(rel).astype(jnp.int32)
    ok = (kpos[:, None, :] // CHUNK) <= (qpos[:, :, None] // CHUNK)
    mask = jnp.where(ok, 0.0, NEG_INF).astype(F32)
    return pl.pallas_call(
        _bias_kernel,
        grid=(n_heads, nt),
        in_specs=[
            pl.BlockSpec(memory_space=pltpu.SMEM),
            pl.BlockSpec((None, r, c), lambda h, t: (t, 0, 0)),
            pl.BlockSpec((None, r, c), lambda h, t: (t, 0, 0)),
        ],
        out_specs=pl.BlockSpec((None, None, r, c), lambda h, t: (h, t, 0, 0)),
        out_shape=jax.ShapeDtypeStruct((n_heads, nt, r, c), F32),
        compiler_params=_params("arbitrary", "arbitrary"),
        name="bias_tiles",
    )(rel_table.astype(F32), idx, mask)


def _far_bucket(min_dist):
    nb = REL_BUCKETS // 2
    max_exact = nb // 2
    large = max_exact + math.log(min_dist / max_exact) / math.log(REL_MAX_DIST / max_exact) * (nb - max_exact)
    assert large >= nb - 1 + 0.5, "key tile too short for a constant far-field bias"
    return nb - 1


def _lambda_value(lam_ref, lam_init):
    a = lam_ref[...]
    s1 = jnp.sum(a[0:1] * a[1:2], axis=-1, keepdims=True)
    s2 = jnp.sum(a[2:3] * a[3:4], axis=-1, keepdims=True)
    return jnp.exp(s1) - jnp.exp(s2) + lam_init


def _split_diff_queries(q):
    lane = lax.broadcasted_iota(jnp.int32, q.shape, 1)
    zero = jnp.zeros_like(q)
    return jnp.concatenate([jnp.where(lane < DIFF_QK_DIM, q, zero),
                            jnp.where(lane >= DIFF_QK_DIM, q, zero)], axis=0)


def _diff_output(acc, l, lam, subln, lam_init, t):
    od = acc[:t] / l[:t] - lam * (acc[t:] / l[t:])
    return _rms_rows(od, subln) * (1.0 - lam_init)


def _prompt_attn_kernel(table_ref, lam_ref, subln_ref, qd_ref, kd_ref, vd_ref, qf_ref, kf_ref, vf_ref,
                        frow_ref, bias_ref, cmask_ref, od_ref, of_ref,
                        md_ref, ld_ref, accd_ref, mf_ref, lf_ref, accf_ref,
                        *, tile, lam_init, far_bucket):
    h = pl.program_id(1)
    qi = pl.program_id(2)
    t = tile

    md_ref[...] = jnp.full_like(md_ref, NEG_INF)
    ld_ref[...] = jnp.zeros_like(ld_ref)
    accd_ref[...] = jnp.zeros_like(accd_ref)
    mf_ref[...] = jnp.full_like(mf_ref, NEG_INF)
    lf_ref[...] = jnp.zeros_like(lf_ref)
    accf_ref[...] = jnp.zeros_like(accf_ref)

    def online_step(s, v, m_ref, l_ref, acc_ref):
        m_prev = m_ref[...]
        m_new = jnp.maximum(m_prev, jnp.max(s, axis=-1, keepdims=True))
        alpha = jnp.exp(m_prev - m_new)
        p = jnp.exp(s - m_new)
        l_ref[...] = alpha * l_ref[...] + jnp.sum(p, axis=-1, keepdims=True)
        acc_ref[...] = alpha * acc_ref[...] + _dot(p.astype(BF16), v)
        m_ref[...] = m_new

    def rows(ref, j):
        return ref[pl.ds(pl.multiple_of(j * t, t), t), :]

    qs = _split_diff_queries(qd_ref[...])

    def diff_tile(j, add):
        s = _dot_nt(qs, rows(kd_ref, j))
        if add.ndim == 2 and add.shape[0] == t:
            s = (s.reshape(2, t, t) + add[None]).reshape(2 * t, t)
        else:
            s = s + add
        online_step(s, rows(vd_ref, j), md_ref, ld_ref, accd_ref)

    diff_tile(qi, bias_ref[1])

    @pl.when(qi >= 1)
    def _():
        diff_tile(qi - 1, bias_ref[0])

    c_far = table_ref[far_bucket, h]

    def diff_far(j, carry):
        diff_tile(j, c_far)
        return carry

    lax.fori_loop(0, jnp.maximum(qi - 1, 0), diff_far, 0)

    qf = qf_ref[...]
    f0 = frow_ref[:, pl.ds(pl.multiple_of(qi * t, t), LANES)][:, 0:1]

    def fox_tile(j, mask):
        s = _dot_nt(qf, rows(kf_ref, j))
        decay = f0 - frow_ref[:, pl.ds(pl.multiple_of(j * t, t), t)]
        s = s + decay
        if mask is not None:
            s = s + mask
        online_step(s, rows(vf_ref, j), mf_ref, lf_ref, accf_ref)

    fox_tile(qi, cmask_ref[...])

    def fox_far(j, carry):
        fox_tile(j, None)
        return carry

    lax.fori_loop(0, qi, fox_far, 0)

    lam = _lambda_value(lam_ref, lam_init)
    od_ref[...] = _diff_output(accd_ref[...], ld_ref[...], lam, subln_ref[...], lam_init, t).astype(BF16)
    of_ref[...] = (accf_ref[...] / lf_ref[...]).astype(BF16)


def _prompt_attention(zb, frow, rel_table, lam_params, subln, lam_init, batch, seq, n_heads, tile):
    m = batch * seq
    dw = n_heads * HEAD_DIM
    nq = seq // tile
    t = tile
    ar = jnp.arange(t, dtype=jnp.int32)
    qpos = jnp.stack([ar + t, ar + t])
    kpos = jnp.stack([ar, ar + t])
    bias = _bias_tiles(rel_table, qpos, kpos)
    cmask = jnp.where(ar[None, :] <= ar[:, None], 0.0, NEG_INF).astype(F32)
    far_bucket = _far_bucket(t + 1)

    def q_spec(seg):
        return pl.BlockSpec((t, HEAD_DIM), lambda b, h, q, seg=seg: (b * nq + q, seg * n_heads + h))

    def kv_spec(seg):
        return pl.BlockSpec((seq, HEAD_DIM), lambda b, h, q, seg=seg: (b, seg * n_heads + h))

    out_spec = pl.BlockSpec((t, HEAD_DIM), lambda b, h, q: (b * nq + q, h))
    const2 = lambda b, h, q: (0, 0)
    kern = functools.partial(_prompt_attn_kernel, tile=t, lam_init=lam_init, far_bucket=far_bucket)
    return pl.pallas_call(
        kern,
        grid=(batch, n_heads, nq),
        in_specs=[
            pl.BlockSpec(memory_space=pltpu.SMEM),
            pl.BlockSpec((4, DIFF_QK_DIM), const2),
            pl.BlockSpec((1, HEAD_DIM), const2),
            q_spec(0), kv_spec(1), kv_spec(2), q_spec(3), kv_spec(4), kv_spec(5),
            pl.BlockSpec((None, None, 1, seq), lambda b, h, q: (b, h, 0, 0)),
            pl.BlockSpec((None, 2, t, t), lambda b, h, q: (h, 0, 0, 0)),
            pl.BlockSpec((t, t), const2),
        ],
        out_specs=[out_spec, out_spec],
        out_shape=[jax.ShapeDtypeStruct((m, dw), BF16), jax.ShapeDtypeStruct((m, dw), BF16)],
        scratch_shapes=[
            pltpu.VMEM((2 * t, 1), F32), pltpu.VMEM((2 * t, 1), F32), pltpu.VMEM((2 * t, HEAD_DIM), F32),
            pltpu.VMEM((t, 1), F32), pltpu.VMEM((t, 1), F32), pltpu.VMEM((t, HEAD_DIM), F32),
        ],
        compiler_params=_params("arbitrary", "arbitrary", "arbitrary"),
        name="prompt_attention",
    )(rel_table.astype(F32), lam_params, subln, zb, zb, zb, zb, zb, zb,
      frow.reshape(batch, n_heads, 1, seq), bias, cmask)


def _sample_attn_kernel(lam_ref, subln_ref, qd_ref, kdn_ref, vdn_ref, qf_ref, kfn_ref, vfn_ref,
                        kdc_ref, vdc_ref, kfc_ref, vfc_ref, frow_ref, bias_ref, cmask_ref,
                        od_ref, of_ref, *, past, t_new, lam_init):
    def attend(q, kc, vc, kn, vn, add_c, add_n):
        sc = _dot_nt(q, kc) + add_c
        sn = _dot_nt(q, kn) + add_n
        mx = jnp.maximum(jnp.max(sc, axis=-1, keepdims=True), jnp.max(sn, axis=-1, keepdims=True))
        pc = jnp.exp(sc - mx)
        pn = jnp.exp(sn - mx)
        l = jnp.sum(pc, axis=-1, keepdims=True) + jnp.sum(pn, axis=-1, keepdims=True)
        acc = _dot(pc.astype(BF16), vc) + _dot(pn.astype(BF16), vn)
        return acc, l

    qs = _split_diff_queries(qd_ref[...])
    bias_c = bias_ref[:, :past]
    bias_n = bias_ref[:, past:past + t_new]
    acc, l = attend(qs, kdc_ref[...].astype(BF16), vdc_ref[...].astype(BF16), kdn_ref[...], vdn_ref[...],
                    jnp.concatenate([bias_c, bias_c], axis=0), jnp.concatenate([bias_n, bias_n], axis=0))
    lam = _lambda_value(lam_ref, lam_init)
    od_ref[...] = _diff_output(acc, l, lam, subln_ref[...], lam_init, t_new).astype(BF16)

    f0 = frow_ref[:, past:past + 1]
    dec_c = f0 - frow_ref[:, :past]
    dec_n = f0 - frow_ref[:, past:past + t_new]
    acc, l = attend(qf_ref[...], kfc_ref[...].astype(BF16), vfc_ref[...].astype(BF16), kfn_ref[...], vfn_ref[...],
                    dec_c, dec_n + cmask_ref[...])
    of_ref[...] = (acc / l).astype(BF16)


def _sample_attention(zb, caches, frow, rel_table, lam_params, subln, lam_init, nb, t_new, past, n_heads):
    dw = n_heads * HEAD_DIM
    tpad = frow.shape[-1]
    ar = jnp.arange(t_new, dtype=jnp.int32)
    qpos = (past + ar)[None]
    kpos = jnp.arange(tpad, dtype=jnp.int32)[None]
    bias = _bias_tiles(rel_table, qpos, kpos)[:, 0]
    cmask = jnp.where(ar[None, :] <= ar[:, None], 0.0, NEG_INF).astype(F32)

    def new_spec(seg):
        return pl.BlockSpec((t_new, HEAD_DIM), lambda b, h, seg=seg: (b, seg * n_heads + h))

    cache_spec = pl.BlockSpec((None, past, HEAD_DIM), lambda b, h: (b, 0, h))
    out_spec = pl.BlockSpec((t_new, HEAD_DIM), lambda b, h: (b, h))
    const2 = lambda b, h: (0, 0)
    kern = functools.partial(_sample_attn_kernel, past=past, t_new=t_new, lam_init=lam_init)
    caches = [c.reshape(nb, past, dw) for c in caches]
    return pl.pallas_call(
        kern,
        grid=(nb, n_heads),
        in_specs=[
            pl.BlockSpec((4, DIFF_QK_DIM), const2),
            pl.BlockSpec((1, HEAD_DIM), const2),
            new_spec(0), new_spec(1), new_spec(2), new_spec(3), new_spec(4), new_spec(5),
            cache_spec, cache_spec, cache_spec, cache_spec,
            pl.BlockSpec((None, None, 1, tpad), lambda b, h: (b, h, 0, 0)),
            pl.BlockSpec((None, t_new, tpad), lambda b, h: (h, 0, 0)),
            pl.BlockSpec((t_new, t_new), const2),
        ],
        out_specs=[out_spec, out_spec],
        out_shape=[jax.ShapeDtypeStruct((nb * t_new, dw), BF16), jax.ShapeDtypeStruct((nb * t_new, dw), BF16)],
        compiler_params=_params("arbitrary", "arbitrary"),
        name="sample_attention",
    )(lam_params, subln, zb, zb, zb, zb, zb, zb, *caches,
      frow.reshape(nb, n_heads, 1, tpad), bias, cmask)


def _merge_kernel(x_ref, g1_ref, od_ref, of_ref, wbd_ref, wbf_ref, wga_ref, wgb_ref, wout_ref, gp_ref,
                  o_ref, xn_ref, u_ref):
    n = pl.program_id(1)

    @pl.when(n == 0)
    def _():
        xn_ref[...] = _rms_rows(x_ref[...], g1_ref[...]).astype(BF16)

    xn = xn_ref[...]
    yd = _dot(od_ref[...], wbd_ref[...])
    yf = _dot(of_ref[...], wbf_ref[...])
    ga = jax.nn.sigmoid(_dot(xn, wga_ref[...]))
    gb = jax.nn.sigmoid(_dot(xn, wgb_ref[...]))
    u_ref[n] = (ga * yd + gb * yf).astype(BF16)

    @pl.when(n == pl.num_programs(1) - 1)
    def _():
        u = jnp.concatenate([u_ref[k] for k in range(u_ref.shape[0])], axis=1)
        o = _dot(u, wout_ref[...])
        o_ref[...] = x_ref[...] + _rms_rows(o, gp_ref[...])


def _merge(x, g1, od, of, w_bd, w_bf, w_ga, w_gb, w_out, g_post, tm, tn):
    m, d = x.shape
    dw = od.shape[1]
    nn = d // tn
    row = lambda i, n: (i, 0)
    col = lambda i, n: (0, n)
    const2 = lambda i, n: (0, 0)
    return pl.pallas_call(
        _merge_kernel,
        grid=(m // tm, nn),
        in_specs=[
            pl.BlockSpec((tm, d), row),
            pl.BlockSpec((1, d), const2),
            pl.BlockSpec((tm, dw), row),
            pl.BlockSpec((tm, dw), row),
            pl.BlockSpec((dw, tn), col),
            pl.BlockSpec((dw, tn), col),
            pl.BlockSpec((d, tn), col),
            pl.BlockSpec((d, tn), col),
            pl.BlockSpec((d, d), const2, pipeline_mode=pl.Buffered(1)),
            pl.BlockSpec((1, d), const2),
        ],
        out_specs=pl.BlockSpec((tm, d), row),
        out_shape=jax.ShapeDtypeStruct((m, d), F32),
        scratch_shapes=[pltpu.VMEM((tm, d), BF16), pltpu.VMEM((nn, tm, tn), BF16)],
        compiler_params=_params("arbitrary", "arbitrary"),
        name="merge",
    )(x, g1, od, of, w_bd, w_bf, w_ga, w_gb, w_out, g_post)


def _gelu_tanh(x):
    return 0.5 * x * (1.0 + jnp.tanh(math.sqrt(2.0 / math.pi) * (x + 0.044715 * (x * x * x))))


def _ffn_kernel(*refs, seq, has_edges):
    if has_edges:
        (h_ref, g2_ref, wa_ref, wb_ref, cw_ref, cb_ref, wd_ref, gp_ref, e0_ref, e1_ref,
         o_ref, cs_ref, xn_ref, tail_ref) = refs
    else:
        (h_ref, g2_ref, wa_ref, wb_ref, cw_ref, cb_ref, wd_ref, gp_ref,
         o_ref, cs_ref, xn_ref, tail_ref) = refs
    i = pl.program_id(0)
    f = pl.program_id(1)
    tm = h_ref.shape[0]

    @pl.when(f == 0)
    def _():
        xn_ref[...] = _rms_rows(h_ref[...], g2_ref[...]).astype(BF16)

    xn = xn_ref[...]
    a = _dot(xn, wa_ref[...])
    gate = _dot(xn, wb_ref[...])
    row = lax.broadcasted_iota(jnp.int32, a.shape, 0)
    back1 = pltpu.roll(a, 1, 0)
    back2 = pltpu.roll(a, 2, 0)
    if has_edges:
        t = row % seq
        am1 = jnp.where(t >= 1, back1, 0.0) + e1_ref[...]
        am2 = jnp.where(t >= 2, back2, 0.0) + e0_ref[...]
        cs_ref[...] = a.reshape(tm // seq, seq, a.shape[1])[:, seq - (CONV_WIDTH - 1):, :]
    else:
        @pl.when((i * tm) % seq == 0)
        def _():
            tail_ref[f] = jnp.zeros(tail_ref.shape[1:], F32)

        prev = tail_ref[f]
        p0 = prev[SUBLANES - 2:SUBLANES - 1]
        p1 = prev[SUBLANES - 1:SUBLANES]
        am1 = jnp.where(row == 0, p1, back1)
        am2 = jnp.where(row == 0, p0, jnp.where(row == 1, p1, back2))
        tail_ref[f] = a[tm - SUBLANES:, :]
        cs_ref[...] = a[tm - (CONV_WIDTH - 1):, :][None]
    cw = cw_ref[...]
    ac = cw[0:1] * am2 + cw[1:2] * am1 + cw[2:3] * a + cb_ref[...]
    g = (_gelu_tanh(ac) * gate).astype(BF16)
    contrib = _dot(g, wd_ref[...])

    @pl.when(f == 0)
    def _():
        o_ref[...] = contrib

    @pl.when(f > 0)
    def _():
        o_ref[...] += contrib

    @pl.when(f == pl.num_programs(1) - 1)
    def _():
        o_ref[...] = h_ref[...] + _rms_rows(o_ref[...], gp_ref[...])


def _ffn(h, g2, w_a, w_b, conv_w, conv_b, w_d, g_post, edges, nseq, seq, tm, tf):
    m, d = h.shape
    dff = w_a.shape[1]
    nf = dff // tf
    has_edges = edges is not None
    row = lambda i, f: (i, 0)
    col = lambda i, f: (0, f)
    const2 = lambda i, f: (0, 0)
    in_specs = [
        pl.BlockSpec((tm, d), row),
        pl.BlockSpec((1, d), const2),
        pl.BlockSpec((d, tf), col),
        pl.BlockSpec((d, tf), col),
        pl.BlockSpec((CONV_WIDTH, tf), col),
        pl.BlockSpec((1, tf), col),
        pl.BlockSpec((tf, d), lambda i, f: (f, 0)),
        pl.BlockSpec((1, d), const2),
    ]
    args = [h, g2, w_a, w_b, conv_w, conv_b, w_d, g_post]
    if has_edges:
        assert tm % seq == 0
        in_specs += [pl.BlockSpec((tm, tf), lambda i, f: (i, f))] * 2
        args += list(edges)
        cs_spec = pl.BlockSpec((tm // seq, CONV_WIDTH - 1, tf), lambda i, f: (i, 0, f))
    else:
        assert seq % tm == 0
        cs_spec = pl.BlockSpec((1, CONV_WIDTH - 1, tf), lambda i, f: ((i * tm) // seq, 0, f))
    return pl.pallas_call(
        functools.partial(_ffn_kernel, seq=seq, has_edges=has_edges),
        grid=(m // tm, nf),
        in_specs=in_specs,
        out_specs=[pl.BlockSpec((tm, d), row), cs_spec],
        out_shape=[jax.ShapeDtypeStruct((m, d), F32),
                   jax.ShapeDtypeStruct((nseq, CONV_WIDTH - 1, dff), F32)],
        scratch_shapes=[pltpu.VMEM((tm, d), BF16), pltpu.VMEM((nf, SUBLANES, tf), F32)],
        compiler_params=_params("arbitrary", "arbitrary"),
        name="conv_ffn",
    )(*args)


def _tile(n, cap):
    t = min(n, cap)
    assert n % t == 0
    return t


def _layer(hp, hs, past, rel_table, lam_params, lam_init, p):
    (pre1, w_in, b_forget, subln, w_bd, w_bf, w_out, post1, pre2, w_up, conv_w, conv_b, w_down, post2) = p
    batch, seq, d = hp.shape
    nb, t_new, _ = hs.shape
    n_heads = d // (2 * HEAD_DIM)
    dw = n_heads * HEAD_DIM
    dff = w_down.shape[0]
    pk, pv, pfk, pfv, plogf, conv_prev = past
    plen = pk.shape[1]

    row2 = lambda v: v.reshape(1, -1).astype(F32)
    w_qkv = w_in[:, :6 * dw].astype(BF16)
    w_f = jnp.pad(w_in[:, 6 * dw:6 * dw + n_heads], ((0, 0), (0, LANES - n_heads))).astype(BF16)
    b_f = jnp.pad(b_forget.astype(F32), (0, LANES - n_heads)).reshape(1, LANES)
    w_ga = w_in[:, 6 * dw + n_heads:6 * dw + n_heads + d].astype(BF16)
    w_gb = w_in[:, 6 * dw + n_heads + d:].astype(BF16)
    one = jnp.ones((dw,), F32)
    colscale = jnp.concatenate([one * DIFF_QK_DIM ** -0.5, one, one,
                                one * HEAD_DIM ** -0.5, one, one]).reshape(1, 6 * dw)
    w_bd_b, w_bf_b, w_out_b = w_bd.astype(BF16), w_bf.astype(BF16), w_out.astype(BF16)
    w_a, w_b = w_up[:, :dff].astype(BF16), w_up[:, dff:].astype(BF16)
    w_d = w_down.astype(BF16)
    subln2 = row2(subln)

    def dense_tail(x2, od, of, edges, nseq, t_seq):
        m = x2.shape[0]
        tm = _tile(m, 512)
        h1 = _merge(x2, row2(pre1), od, of, w_bd_b, w_bf_b, w_ga, w_gb, w_out_b, row2(post1),
                    tm, _tile(d, 512))
        return _ffn(h1, row2(pre2), w_a, w_b, conv_w.astype(F32), row2(conv_b), w_d, row2(post2),
                    edges, nseq, t_seq, tm, _tile(dff, 1024))

    xp = hp.reshape(batch * seq, d)
    zb, kd, vd, kf, vf, logf, flp = _inproj(xp, row2(pre1), w_qkv, colscale, w_f, b_f, n_heads,
                                            _tile(batch * seq, 512))
    frow = _cumsum_rows(flp.reshape(batch, seq, LANES), n_heads, _tile(seq, 256))
    od, of = _prompt_attention(zb, frow, rel_table, lam_params, subln2, lam_init,
                               batch, seq, n_heads, _tile(seq, 256))
    hp_out, conv_p = dense_tail(xp, od, of, None, batch, seq)
    heads = lambda a, n, t: a.reshape(n, t, n_heads, HEAD_DIM)
    state_p = (heads(kd, batch, seq), heads(vd, batch, seq), heads(kf, batch, seq), heads(vf, batch, seq),
               logf.reshape(batch, seq, n_heads), conv_p)

    xs = hs.reshape(nb * t_new, d)
    zb, kd, vd, kf, vf, logf, flp = _inproj(xs, row2(pre1), w_qkv, colscale, w_f, b_f, n_heads,
                                            _tile(nb * t_new, 512))
    tpad = -(-(plen + t_new) // LANES) * LANES
    flog = jnp.concatenate([jnp.pad(plogf.astype(F32), ((0, 0), (0, 0), (0, LANES - n_heads))),
                            flp.reshape(nb, t_new, LANES),
                            jnp.zeros((nb, tpad - plen - t_new, LANES), F32)], axis=1)
    frow = _cumsum_rows(flog, n_heads, LANES)
    od, of = _sample_attention(zb, (pk, pv, pfk, pfv), frow, rel_table, lam_params, subln2, lam_init,
                               nb, t_new, plen, n_heads)
    cp = conv_prev.astype(F32)
    e0 = jnp.zeros((nb, t_new, dff), F32).at[:, 0].set(cp[:, 0]).at[:, 1].set(cp[:, 1])
    e1 = jnp.zeros((nb, t_new, dff), F32).at[:, 0].set(cp[:, 1])
    edges = (e0.reshape(nb * t_new, dff), e1.reshape(nb * t_new, dff))
    hs_out, conv_s = dense_tail(xs, od, of, edges, nb, t_new)
    state_s = (heads(kd, nb, t_new), heads(vd, nb, t_new), heads(kf, nb, t_new), heads(vf, nb, t_new),
               logf.reshape(nb, t_new, n_heads), conv_s)
    return hp_out.reshape(batch, seq, d), hs_out.reshape(nb, t_new, d), state_p, state_s


def kernel(x_prompt, x_sample, cache_diff_k, cache_diff_v, cache_fox_k, cache_fox_v, cache_fox_logf,
           state_ffn_conv, rel_table, pre_norm1, w_in, b_forget, lam_q1, lam_k1, lam_q2, lam_k2,
           diff_subln, w_branch_diff, w_branch_fox, w_out, post_norm1, pre_norm2, w_up, conv_w, conv_b,
           w_down, post_norm2):
    depth = w_in.shape[0]
    hp, hs = x_prompt, x_sample
    new_p, new_s = [], []
    for l in range(depth):
        lam_init = 0.8 - 0.6 * math.exp(-0.3 * l)
        lam_params = jnp.stack([lam_q1[l], lam_k1[l], lam_q2[l], lam_k2[l]]).astype(F32)
        params = (pre_norm1[l], w_in[l], b_forget[l], diff_subln[l], w_branch_diff[l], w_branch_fox[l],
                  w_out[l], post_norm1[l], pre_norm2[l], w_up[l], conv_w[l], conv_b[l], w_down[l],
                  post_norm2[l])
        past = (cache_diff_k[l], cache_diff_v[l], cache_fox_k[l], cache_fox_v[l], cache_fox_logf[l],
                state_ffn_conv[l])
        hp, hs, sp, ss = _layer(hp, hs, past, rel_table, lam_params, lam_init, params)
        new_p.append(sp)
        new_s.append(ss)
    st = lambda lst, i: jnp.stack([e[i] for e in lst])
    return (hp, hs,
            st(new_p, 0), st(new_p, 1), st(new_p, 2), st(new_p, 3), st(new_p, 4), st(new_p, 5),
            st(new_s, 0), st(new_s, 1), st(new_s, 2), st(new_s, 3), st(new_s, 4), st(new_s, 5))
```

```python
import functools
import math

import jax
import jax.numpy as jnp
from jax import lax
from jax.experimental import pallas as pl
from jax.experimental.pallas import tpu as pltpu

HEAD_DIM = 128
DIFF_QK_DIM = HEAD_DIM // 2
CHUNK = 64
CONV_WIDTH = 3
REL_BUCKETS = 32
REL_MAX_DIST = 128
EPS = 1e-6
NEG_INF = -1e30
LOG2E = math.log2(math.e)

LANES = 128
SUBLANES = 8
VMEM_LIMIT_BYTES = 60 * 1024 * 1024

F32 = jnp.float32
BF16 = jnp.bfloat16


def _params(*sem):
    return pltpu.CompilerParams(dimension_semantics=sem, vmem_limit_bytes=VMEM_LIMIT_BYTES)


def _rms_rows(x, g):
    return x * lax.rsqrt(jnp.mean(x * x, axis=-1, keepdims=True) + EPS) * g


def _dot(a, b):
    return jnp.dot(a, b, preferred_element_type=F32)


def _dot_nt(a, b):
    return lax.dot_general(a, b, (((1,), (1,)), ((), ())), preferred_element_type=F32)


def _split3(x):
    hi = x.astype(BF16)
    r1 = x - hi.astype(F32)
    mid = r1.astype(BF16)
    lo = (r1 - mid.astype(F32)).astype(BF16)
    return hi, mid, lo


def _split3_on_axis(x, index):
    hi, mid, lo = (v.astype(F32) for v in _split3(x))
    return jnp.where(index == 0, hi, jnp.where(index == 1, mid, jnp.where(index == 2, lo, 0.0))).astype(BF16)


def _transpose_bf16(x):
    return x.astype(F32).T.astype(BF16)


def _inproj_kernel(x_ref, g_ref, w_ref, cs_ref, wf_ref, bf_ref,
                   zb_ref, kd_ref, vd_ref, kf_ref, vf_ref, logf_ref, flp_ref, xn_ref, *, n_heads):
    j = pl.program_id(1)

    @pl.when(j == 0)
    def _():
        xn = _rms_rows(x_ref[...], g_ref[...]).astype(BF16)
        xn_ref[...] = xn
        fl = _dot(xn, wf_ref[...]) + bf_ref[...]
        lf = jnp.minimum(fl, 0.0) - jnp.log1p(jnp.exp(-jnp.abs(fl)))
        flp_ref[...] = lf
        logf_ref[...] = lf[:, :n_heads]

    z = _dot(xn_ref[...], w_ref[...])
    zb_ref[...] = (z * cs_ref[...]).astype(BF16)
    for seg, ref in ((1, kd_ref), (2, vd_ref), (4, kf_ref), (5, vf_ref)):
        @pl.when(j == seg)
        def _(ref=ref):
            ref[...] = z


def _inproj(x, g, w_qkv, colscale, w_f, b_f, n_heads, tm):
    m, d = x.shape
    dw = w_qkv.shape[1] // 6
    row = lambda i, j: (i, 0)
    f32_out = jax.ShapeDtypeStruct((m, dw), F32)
    return pl.pallas_call(
        functools.partial(_inproj_kernel, n_heads=n_heads),
        grid=(m // tm, 6),
        in_specs=[
            pl.BlockSpec((tm, d), row),
            pl.BlockSpec((1, d), lambda i, j: (0, 0)),
            pl.BlockSpec((d, dw), lambda i, j: (0, j)),
            pl.BlockSpec((1, dw), lambda i, j: (0, j)),
            pl.BlockSpec((d, LANES), lambda i, j: (0, 0)),
            pl.BlockSpec((1, LANES), lambda i, j: (0, 0)),
        ],
        out_specs=[
            pl.BlockSpec((tm, dw), lambda i, j: (i, j)),
            pl.BlockSpec((tm, dw), row),
            pl.BlockSpec((tm, dw), row),
            pl.BlockSpec((tm, dw), row),
            pl.BlockSpec((tm, dw), row),
            pl.BlockSpec((tm, n_heads), row),
            pl.BlockSpec((tm, LANES), row),
        ],
        out_shape=[
            jax.ShapeDtypeStruct((m, 6 * dw), BF16),
            f32_out, f32_out, f32_out, f32_out,
            jax.ShapeDtypeStruct((m, n_heads), F32),
            jax.ShapeDtypeStruct((m, LANES), F32),
        ],
        scratch_shapes=[pltpu.VMEM((tm, d), BF16)],
        compiler_params=_params("arbitrary", "arbitrary"),
        name="inproj",
    )(x, g, w_qkv, colscale, w_f, b_f)


def _cumsum_kernel(x_ref, o_ref, *, chunk, n_heads, as_key_columns):
    t = x_ref.shape[0]
    r = lax.broadcasted_iota(jnp.int32, (chunk, chunk), 0)
    c = lax.broadcasted_iota(jnp.int32, (chunk, chunk), 1)
    tri = (r >= c).astype(BF16)
    lane = lax.broadcasted_iota(jnp.int32, (chunk, LANES), 1)
    carry = jnp.zeros((1, LANES), F32)
    for k in range(t // chunk):
        hi, mid, lo = _split3(x_ref[k * chunk:(k + 1) * chunk, :])
        f = _dot(tri, hi) + _dot(tri, mid) + _dot(tri, lo) + carry
        carry = f[chunk - 1:chunk, :]
        if as_key_columns:
            for h in range(n_heads):
                o_ref[h, k * chunk:(k + 1) * chunk, :] = _split3_on_axis(
                    jnp.broadcast_to(f[:, h:h + 1] * LOG2E, (chunk, LANES)), lane)
        else:
            o_ref[:, k * chunk:(k + 1) * chunk] = f.T[:n_heads, :]


def _cumsum(flp, n_heads, chunk, as_key_columns):
    nb, t, _ = flp.shape
    if as_key_columns:
        out_spec = pl.BlockSpec((None, n_heads, t, LANES), lambda b: (b, 0, 0, 0))
        out_shape = jax.ShapeDtypeStruct((nb, n_heads, t, LANES), BF16)
    else:
        out_spec = pl.BlockSpec((None, n_heads, t), lambda b: (b, 0, 0))
        out_shape = jax.ShapeDtypeStruct((nb, n_heads, t), F32)
    return pl.pallas_call(
        functools.partial(_cumsum_kernel, chunk=chunk, n_heads=n_heads, as_key_columns=as_key_columns),
        grid=(nb,),
        in_specs=[pl.BlockSpec((None, t, LANES), lambda b: (b, 0, 0))],
        out_specs=out_spec,
        out_shape=out_shape,
        compiler_params=_params("arbitrary"),
        name="cumsum_keys" if as_key_columns else "cumsum_rows",
    )(flp)


def _t5_bucket(rel):
    nb = REL_BUCKETS // 2
    max_exact = nb // 2
    n = jnp.abs(rel)
    nf = jnp.maximum(n, 1).astype(jnp.float32)
    large = max_exact + (jnp.log(nf / max_exact) / math.log(REL_MAX_DIST / max_exact)
                         * (nb - max_exact)).astype(jnp.int32)
    large = jnp.minimum(large, nb - 1)
    return jnp.where(rel > 0, nb, 0) + jnp.where(n < max_exact, n, large)


def _bias_kernel(table_ref, idx_ref, mask_ref, o_ref, *, minus_bucket):
    h = pl.program_id(0)
    idx = idx_ref[...]
    acc = mask_ref[...]
    if minus_bucket is not None:
        acc = acc - table_ref[minus_bucket, h]
    for b in range(REL_BUCKETS):
        acc = acc + jnp.where(idx == b, table_ref[b, h], 0.0)
    o_ref[...] = acc * LOG2E


def _bias_tiles(rel_table, qpos, kpos, keys_major=False, minus_bucket=None):
    n_heads = rel_table.shape[1]
    if keys_major:
        qp, kp = qpos[:, None, :], kpos[:, :, None]
    else:
        qp, kp = qpos[:, :, None], kpos[:, None, :]
    idx = _t5_bucket(kp - qp).astype(jnp.int32)
    mask = jnp.where((kp // CHUNK) <= (qp // CHUNK), 0.0, NEG_INF).astype(F32)
    nt, r, c = idx.shape
    return pl.pallas_call(
        functools.partial(_bias_kernel, minus_bucket=minus_bucket),
        grid=(n_heads, nt),
        in_specs=[
            pl.BlockSpec(memory_space=pltpu.SMEM),
            pl.BlockSpec((None, r, c), lambda h, t: (t, 0, 0)),
            pl.BlockSpec((None, r, c), lambda h, t: (t, 0, 0)),
        ],
        out_specs=pl.BlockSpec((None, None, r, c), lambda h, t: (h, t, 0, 0)),
        out_shape=jax.ShapeDtypeStruct((n_heads, nt, r, c), F32),
        compiler_params=_params("arbitrary", "arbitrary"),
        name="bias_tiles",
    )(rel_table.astype(F32), idx, mask)


def _far_bucket(min_dist):
    nb = REL_BUCKETS // 2
    max_exact = nb // 2
    large = max_exact + math.log(min_dist / max_exact) / math.log(REL_MAX_DIST / max_exact) * (nb - max_exact)
    assert large >= nb - 1 + 0.5, "key tile too short for a constant far-field bias"
    return nb - 1


def _lambda_value(lam_ref, lam_init):
    a = lam_ref[...]
    s1 = jnp.sum(a[0:1] * a[1:2], axis=-1, keepdims=True)
    s2 = jnp.sum(a[2:3] * a[3:4], axis=-1, keepdims=True)
    return jnp.exp(s1) - jnp.exp(s2) + lam_init


def _prompt_attn_kernel(table_ref, lam_ref, subln_ref, qd_ref, kd_ref, vd_ref, qf_ref, kf_ref, vf_ref,
                        fcol_ref, bias_ref, cmask_ref, od_ref, of_ref,
                        kda_ref, kfa_ref, vdt_ref, vft_ref,
                        md_ref, ld_ref, accd_ref, mf_ref, lf_ref, accf_ref,
                        sd_ref, pd_ref, ad_ref, sf_ref, pf_ref, af_ref,
                        *, tile, lam_init, far_bucket):
    h = pl.program_id(1)
    qi = pl.program_id(2)
    t = tile
    seq = kd_ref.shape[0]

    @pl.when(qi == 0)
    def _():
        lane = lax.broadcasted_iota(jnp.int32, (seq, LANES), 1)
        kda_ref[:, :HEAD_DIM] = kd_ref[...]
        kda_ref[:, HEAD_DIM:] = jnp.where(lane < 3, 1.0, 0.0).astype(BF16)
        kfa_ref[:, :HEAD_DIM] = kf_ref[...]
        kfa_ref[:, HEAD_DIM:] = fcol_ref[...]

        def transpose_values(c, carry):
            r = pl.ds(pl.multiple_of(c * t, t), t)
            vdt_ref[c] = _transpose_bf16(vd_ref[r, :])
            vft_ref[c] = _transpose_bf16(vf_ref[r, :])
            return carry

        lax.fori_loop(0, seq // t, transpose_values, 0)

    md_ref[...] = jnp.full_like(md_ref, NEG_INF)
    ld_ref[...] = jnp.zeros_like(ld_ref)
    accd_ref[...] = jnp.zeros_like(accd_ref)
    mf_ref[...] = jnp.full_like(mf_ref, NEG_INF)
    lf_ref[...] = jnp.zeros_like(lf_ref)
    accf_ref[...] = jnp.zeros_like(accf_ref)

    def softmax_step(s, m_ref, l_ref):
        m_prev = m_ref[...]
        m_new = jnp.maximum(m_prev, jnp.max(s, axis=0, keepdims=True))
        alpha = jnp.exp2(m_prev - m_new)
        p = jnp.exp2(s - m_new)
        l_ref[...] = alpha * l_ref[...] + jnp.sum(p, axis=0, keepdims=True)
        m_ref[...] = m_new
        return p.astype(BF16), alpha

    def online_step(s, vt, m_ref, l_ref, acc_ref):
        p, alpha = softmax_step(s, m_ref, l_ref)
        acc_ref[...] = alpha * acc_ref[...] + _dot(vt, p)

    def key_rows(ref, j):
        return ref[pl.ds(pl.multiple_of(j * t, t), t), :]

    row = lax.broadcasted_iota(jnp.int32, (HEAD_DIM, t), 0)
    qdt = qd_ref[...].astype(F32).T
    zero = jnp.zeros_like(qdt)
    top = jnp.concatenate([jnp.where(row < DIFF_QK_DIM, qdt, zero),
                           jnp.where(row >= DIFF_QK_DIM, qdt, zero)], axis=1).astype(BF16)
    row2 = lax.broadcasted_iota(jnp.int32, (HEAD_DIM, 2 * t), 0)
    bottom = _split3_on_axis(jnp.full((HEAD_DIM, 2 * t), table_ref[far_bucket, h] * LOG2E, F32), row2)
    qd_aug = jnp.concatenate([top, bottom], axis=0)
    qf_aug = jnp.concatenate([qf_ref[...].astype(F32).T.astype(BF16),
                              jnp.where(row < 3, -1.0, 0.0).astype(BF16)], axis=0)

    def diff_tile(j, bias):
        s = _dot(key_rows(kda_ref, j), qd_aug)
        if bias is not None:
            s = s + jnp.concatenate([bias, bias], axis=1)
        online_step(s, vdt_ref[j], md_ref, ld_ref, accd_ref)

    def fox_tile(j, mask):
        s = _dot(key_rows(kfa_ref, j), qf_aug)
        if mask is not None:
            s = s + mask
        online_step(s, vft_ref[j], mf_ref, lf_ref, accf_ref)

    diff_tile(qi, bias_ref[1])
    fox_tile(qi, cmask_ref[...])

    @pl.when(qi >= 1)
    def _():
        diff_tile(qi - 1, bias_ref[0])
        fox_tile(qi - 1, None)

    n_far = jnp.maximum(qi - 1, 0)
    branches = ((kda_ref, qd_aug, vdt_ref, sd_ref, pd_ref, ad_ref, md_ref, ld_ref, accd_ref),
                (kfa_ref, qf_aug, vft_ref, sf_ref, pf_ref, af_ref, mf_ref, lf_ref, accf_ref))

    @pl.when(n_far > 0)
    def _():
        for k_ref, q_aug, vt_ref, s_ref, p_ref, a_ref, m_ref, l_ref, acc_ref in branches:
            s_ref[0] = _dot(key_rows(k_ref, 0), q_aug)
            p_ref[1] = jnp.zeros(p_ref.shape[1:], BF16)
            a_ref[...] = jnp.ones_like(a_ref)

        def far_tiles(j, carry):
            slot = j % 2
            for k_ref, q_aug, vt_ref, s_ref, p_ref, a_ref, m_ref, l_ref, acc_ref in branches:
                acc_ref[...] = a_ref[...] * acc_ref[...] + _dot(vt_ref[jnp.maximum(j - 1, 0)], p_ref[1 - slot])
                p, alpha = softmax_step(s_ref[slot], m_ref, l_ref)
                p_ref[slot] = p
                a_ref[...] = alpha
                s_ref[1 - slot] = _dot(key_rows(k_ref, j + 1), q_aug)
            return carry

        lax.fori_loop(0, n_far, far_tiles, 0)
        last = n_far - 1
        for k_ref, q_aug, vt_ref, s_ref, p_ref, a_ref, m_ref, l_ref, acc_ref in branches:
            acc_ref[...] = a_ref[...] * acc_ref[...] + _dot(vt_ref[last], p_ref[last % 2])

    lam = _lambda_value(lam_ref, lam_init)
    acc = accd_ref[...]
    l = ld_ref[...]
    odt = acc[:, :t] / l[:, :t] - lam * (acc[:, t:] / l[:, t:])
    od_ref[...] = (_rms_rows(odt.T, subln_ref[...]) * (1.0 - lam_init)).astype(BF16)
    of_ref[...] = (accf_ref[...] / lf_ref[...]).T.astype(BF16)


def _prompt_attention(zb, fcols, rel_table, lam_params, subln, lam_init, batch, seq, n_heads, tile):
    m = batch * seq
    dw = n_heads * HEAD_DIM
    nq = seq // tile
    t = tile
    far_bucket = _far_bucket(t + 1)
    ar = jnp.arange(t, dtype=jnp.int32)
    bias = _bias_tiles(rel_table, jnp.stack([ar + t, ar + t]), jnp.stack([ar, ar + t]),
                       keys_major=True, minus_bucket=far_bucket)
    cmask = jnp.where(ar[:, None] <= ar[None, :], 0.0, NEG_INF).astype(F32)

    def q_spec(seg):
        return pl.BlockSpec((t, HEAD_DIM), lambda b, h, q, seg=seg: (b * nq + q, seg * n_heads + h))

    def kv_spec(seg):
        return pl.BlockSpec((seq, HEAD_DIM), lambda b, h, q, seg=seg: (b, seg * n_heads + h))

    out_spec = pl.BlockSpec((t, HEAD_DIM), lambda b, h, q: (b * nq + q, h))
    const2 = lambda b, h, q: (0, 0)
    kern = functools.partial(_prompt_attn_kernel, tile=t, lam_init=lam_init, far_bucket=far_bucket)
    return pl.pallas_call(
        kern,
        grid=(batch, n_heads, nq),
        in_specs=[
            pl.BlockSpec(memory_space=pltpu.SMEM),
            pl.BlockSpec((4, DIFF_QK_DIM), const2),
            pl.BlockSpec((1, HEAD_DIM), const2),
            q_spec(0), kv_spec(1), kv_spec(2), q_spec(3), kv_spec(4), kv_spec(5),
            pl.BlockSpec((None, None, seq, LANES), lambda b, h, q: (b, h, 0, 0)),
            pl.BlockSpec((None, 2, t, t), lambda b, h, q: (h, 0, 0, 0)),
            pl.BlockSpec((t, t), const2),
        ],
        out_specs=[out_spec, out_spec],
        out_shape=[jax.ShapeDtypeStruct((m, dw), BF16), jax.ShapeDtypeStruct((m, dw), BF16)],
        scratch_shapes=[
            pltpu.VMEM((seq, 2 * HEAD_DIM), BF16), pltpu.VMEM((seq, 2 * HEAD_DIM), BF16),
            pltpu.VMEM((nq, HEAD_DIM, t), BF16), pltpu.VMEM((nq, HEAD_DIM, t), BF16),
            pltpu.VMEM((1, 2 * t), F32), pltpu.VMEM((1, 2 * t), F32), pltpu.VMEM((HEAD_DIM, 2 * t), F32),
            pltpu.VMEM((1, t), F32), pltpu.VMEM((1, t), F32), pltpu.VMEM((HEAD_DIM, t), F32),
            pltpu.VMEM((2, t, 2 * t), F32), pltpu.VMEM((2, t, 2 * t), BF16), pltpu.VMEM((1, 2 * t), F32),
            pltpu.VMEM((2, t, t), F32), pltpu.VMEM((2, t, t), BF16), pltpu.VMEM((1, t), F32),
        ],
        compiler_params=_params("arbitrary", "arbitrary", "arbitrary"),
        name="prompt_attention",
    )(rel_table.astype(F32), lam_params, subln, zb, zb, zb, zb, zb, zb, fcols, bias, cmask)


def _split_diff_queries(q):
    lane = lax.broadcasted_iota(jnp.int32, q.shape, 1)
    zero = jnp.zeros_like(q)
    return jnp.concatenate([jnp.where(lane < DIFF_QK_DIM, q, zero),
                            jnp.where(lane >= DIFF_QK_DIM, q, zero)], axis=0)


def _sample_attn_kernel(lam_ref, subln_ref, qd_ref, kdn_ref, vdn_ref, qf_ref, kfn_ref, vfn_ref,
                        kdc_ref, vdc_ref, kfc_ref, vfc_ref, frow_ref, bias_ref, cmask_ref,
                        od_ref, of_ref, *, past, t_new, lam_init):
    def attend(q, kc, vc, kn, vn, add_c, add_n):
        sc = _dot_nt(q, kc) + add_c
        sn = _dot_nt(q, kn) + add_n
        mx = jnp.maximum(jnp.max(sc, axis=-1, keepdims=True), jnp.max(sn, axis=-1, keepdims=True))
        pc = jnp.exp2(sc - mx)
        pn = jnp.exp2(sn - mx)
        l = jnp.sum(pc, axis=-1, keepdims=True) + jnp.sum(pn, axis=-1, keepdims=True)
        acc = _dot(pc.astype(BF16), vc) + _dot(pn.astype(BF16), vn)
        return acc, l

    qs = _split_diff_queries(qd_ref[...])
    bias_c = bias_ref[:, :past]
    bias_n = bias_ref[:, past:past + t_new]
    acc, l = attend(qs, kdc_ref[...].astype(BF16), vdc_ref[...].astype(BF16), kdn_ref[...], vdn_ref[...],
                    jnp.concatenate([bias_c, bias_c], axis=0), jnp.concatenate([bias_n, bias_n], axis=0))
    lam = _lambda_value(lam_ref, lam_init)
    od = acc[:t_new] / l[:t_new] - lam * (acc[t_new:] / l[t_new:])
    od_ref[...] = (_rms_rows(od, subln_ref[...]) * (1.0 - lam_init)).astype(BF16)

    f0 = frow_ref[:, past:past + 1]
    dec_c = (f0 - frow_ref[:, :past]) * LOG2E
    dec_n = (f0 - frow_ref[:, past:past + t_new]) * LOG2E
    acc, l = attend(qf_ref[...], kfc_ref[...].astype(BF16), vfc_ref[...].astype(BF16), kfn_ref[...], vfn_ref[...],
                    dec_c, dec_n + cmask_ref[...])
    of_ref[...] = (acc / l).astype(BF16)


def _sample_attention(zb, caches, layer, frow, rel_table, lam_params, subln, lam_init, nb, t_new, past, n_heads):
    dw = n_heads * HEAD_DIM
    tpad = frow.shape[-1]
    ar = jnp.arange(t_new, dtype=jnp.int32)
    bias = _bias_tiles(rel_table, (past + ar)[None], jnp.arange(tpad, dtype=jnp.int32)[None])[:, 0]
    cmask = jnp.where(ar[None, :] <= ar[:, None], 0.0, NEG_INF).astype(F32)

    def new_spec(seg):
        return pl.BlockSpec((t_new, HEAD_DIM), lambda b, h, seg=seg: (b, seg * n_heads + h))

    cache_spec = pl.BlockSpec((None, past, HEAD_DIM), lambda b, h: (layer * nb + b, 0, h))
    out_spec = pl.BlockSpec((t_new, HEAD_DIM), lambda b, h: (b, h))
    const2 = lambda b, h: (0, 0)
    kern = functools.partial(_sample_attn_kernel, past=past, t_new=t_new, lam_init=lam_init)
    caches = [c.reshape(-1, past, dw) for c in caches]
    return pl.pallas_call(
        kern,
        grid=(nb, n_heads),
        in_specs=[
            pl.BlockSpec((4, DIFF_QK_DIM), const2),
            pl.BlockSpec((1, HEAD_DIM), const2),
            new_spec(0), new_spec(1), new_spec(2), new_spec(3), new_spec(4), new_spec(5),
            cache_spec, cache_spec, cache_spec, cache_spec,
            pl.BlockSpec((None, None, 1, tpad), lambda b, h: (b, h, 0, 0)),
            pl.BlockSpec((None, t_new, tpad), lambda b, h: (h, 0, 0)),
            pl.BlockSpec((t_new, t_new), const2),
        ],
        out_specs=[out_spec, out_spec],
        out_shape=[jax.ShapeDtypeStruct((nb * t_new, dw), BF16), jax.ShapeDtypeStruct((nb * t_new, dw), BF16)],
        compiler_params=_params("arbitrary", "arbitrary"),
        name="sample_attention",
    )(lam_params, subln, zb, zb, zb, zb, zb, zb, *caches,
      frow.reshape(nb, n_heads, 1, tpad), bias, cmask)


def _merge_kernel(x_ref, g1_ref, od_ref, of_ref, wbd_ref, wbf_ref, wga_ref, wgb_ref, wout_ref, gp_ref,
                  o_ref, xn_ref, u_ref):
    n = pl.program_id(1)

    @pl.when(n == 0)
    def _():
        xn_ref[...] = _rms_rows(x_ref[...], g1_ref[...]).astype(BF16)

    xn = xn_ref[...]
    yd = _dot(od_ref[...], wbd_ref[...])
    yf = _dot(of_ref[...], wbf_ref[...])
    ga = jax.nn.sigmoid(_dot(xn, wga_ref[...]))
    gb = jax.nn.sigmoid(_dot(xn, wgb_ref[...]))
    u_ref[n] = (ga * yd + gb * yf).astype(BF16)

    @pl.when(n == pl.num_programs(1) - 1)
    def _():
        u = jnp.concatenate([u_ref[k] for k in range(u_ref.shape[0])], axis=1)
        o = _dot(u, wout_ref[...])
        o_ref[...] = x_ref[...] + _rms_rows(o, gp_ref[...])


def _merge(x, g1, od, of, w_bd, w_bf, w_ga, w_gb, w_out, g_post, tm, tn):
    m, d = x.shape
    dw = od.shape[1]
    nn = d // tn
    row = lambda i, n: (i, 0)
    col = lambda i, n: (0, n)
    const2 = lambda i, n: (0, 0)
    return pl.pallas_call(
        _merge_kernel,
        grid=(m // tm, nn),
        in_specs=[
            pl.BlockSpec((tm, d), row),
            pl.BlockSpec((1, d), const2),
            pl.BlockSpec((tm, dw), row),
            pl.BlockSpec((tm, dw), row),
            pl.BlockSpec((dw, tn), col),
            pl.BlockSpec((dw, tn), col),
            pl.BlockSpec((d, tn), col),
            pl.BlockSpec((d, tn), col),
            pl.BlockSpec((d, d), const2, pipeline_mode=pl.Buffered(1)),
            pl.BlockSpec((1, d), const2),
        ],
        out_specs=pl.BlockSpec((tm, d), row),
        out_shape=jax.ShapeDtypeStruct((m, d), F32),
        scratch_shapes=[pltpu.VMEM((tm, d), BF16), pltpu.VMEM((nn, tm, tn), BF16)],
        compiler_params=_params("arbitrary", "arbitrary"),
        name="merge",
    )(x, g1, od, of, w_bd, w_bf, w_ga, w_gb, w_out, g_post)


def _gelu_tanh(x):
    return 0.5 * x * (1.0 + jnp.tanh(math.sqrt(2.0 / math.pi) * (x + 0.044715 * (x * x * x))))


def _ffn_kernel(*refs, seq, has_edges):
    if has_edges:
        (h_ref, g2_ref, wa_ref, wb_ref, cw_ref, cb_ref, wd_ref, gp_ref, e0_ref, e1_ref,
         o_ref, cs_ref, xn_ref, tail_ref) = refs
    else:
        (h_ref, g2_ref, wa_ref, wb_ref, cw_ref, cb_ref, wd_ref, gp_ref,
         o_ref, cs_ref, xn_ref, tail_ref) = refs
    i = pl.program_id(0)
    f = pl.program_id(1)
    tm = h_ref.shape[0]

    @pl.when(f == 0)
    def _():
        xn_ref[...] = _rms_rows(h_ref[...], g2_ref[...]).astype(BF16)

    xn = xn_ref[...]
    a = _dot(xn, wa_ref[...])
    gate = _dot(xn, wb_ref[...])
    row = lax.broadcasted_iota(jnp.int32, a.shape, 0)
    back1 = pltpu.roll(a, 1, 0)
    back2 = pltpu.roll(a, 2, 0)
    if has_edges:
        t = row % seq
        am1 = jnp.where(t >= 1, back1, 0.0) + e1_ref[...]
        am2 = jnp.where(t >= 2, back2, 0.0) + e0_ref[...]
        cs_ref[...] = a.reshape(tm // seq, seq, a.shape[1])[:, seq - (CONV_WIDTH - 1):, :]
    else:
        @pl.when((i * tm) % seq == 0)
        def _():
            tail_ref[f] = jnp.zeros(tail_ref.shape[1:], F32)

        prev = tail_ref[f]
        p0 = prev[SUBLANES - 2:SUBLANES - 1]
        p1 = prev[SUBLANES - 1:SUBLANES]
        am1 = jnp.where(row == 0, p1, back1)
        am2 = jnp.where(row == 0, p0, jnp.where(row == 1, p1, back2))
        tail_ref[f] = a[tm - SUBLANES:, :]
        cs_ref[...] = a[tm - (CONV_WIDTH - 1):, :][None]
    cw = cw_ref[...]
    ac = cw[0:1] * am2 + cw[1:2] * am1 + cw[2:3] * a + cb_ref[...]
    g = (_gelu_tanh(ac) * gate).astype(BF16)
    contrib = _dot(g, wd_ref[...])

    @pl.when(f == 0)
    def _():
        o_ref[...] = contrib

    @pl.when(f > 0)
    def _():
        o_ref[...] += contrib

    @pl.when(f == pl.num_programs(1) - 1)
    def _():
        o_ref[...] = h_ref[...] + _rms_rows(o_ref[...], gp_ref[...])


def _ffn(h, g2, w_a, w_b, conv_w, conv_b, w_d, g_post, edges, seq, tm, tf):
    m, d = h.shape
    dff = w_a.shape[1]
    nf = dff // tf
    has_edges = edges is not None
    row = lambda i, f: (i, 0)
    col = lambda i, f: (0, f)
    const2 = lambda i, f: (0, 0)
    in_specs = [
        pl.BlockSpec((tm, d), row),
        pl.BlockSpec((1, d), const2),
        pl.BlockSpec((d, tf), col),
        pl.BlockSpec((d, tf), col),
        pl.BlockSpec((CONV_WIDTH, tf), col),
        pl.BlockSpec((1, tf), col),
        pl.BlockSpec((tf, d), lambda i, f: (f, 0)),
        pl.BlockSpec((1, d), const2),
    ]
    args = [h, g2, w_a, w_b, conv_w, conv_b, w_d, g_post]
    if has_edges:
        assert tm % seq == 0
        in_specs += [pl.BlockSpec((tm, tf), lambda i, f: (i, f))] * 2
        args += list(edges)
        tails_per_tile = tm // seq
    else:
        assert seq % tm == 0
        tails_per_tile = 1
    n_tails = (m // tm) * tails_per_tile
    out, tails = pl.pallas_call(
        functools.partial(_ffn_kernel, seq=seq, has_edges=has_edges),
        grid=(m // tm, nf),
        in_specs=in_specs,
        out_specs=[pl.BlockSpec((tm, d), row),
                   pl.BlockSpec((tails_per_tile, CONV_WIDTH - 1, tf), lambda i, f: (i, 0, f))],
        out_shape=[jax.ShapeDtypeStruct((m, d), F32),
                   jax.ShapeDtypeStruct((n_tails, CONV_WIDTH - 1, dff), F32)],
        scratch_shapes=[pltpu.VMEM((tm, d), BF16), pltpu.VMEM((nf, SUBLANES, tf), F32)],
        compiler_params=_params("arbitrary", "arbitrary"),
        name="conv_ffn",
    )(*args)
    if not has_edges:
        tiles_per_seq = seq // tm
        tails = tails[tiles_per_seq - 1::tiles_per_seq]
    return out, tails


def _tile(n, cap):
    t = min(n, cap)
    assert n % t == 0
    return t


def _layer(layer, hp, hs, past, rel_table, lam_params, lam_init, p):
    (pre1, w_in, b_forget, subln, w_bd, w_bf, w_out, post1, pre2, w_up, conv_w, conv_b, w_down, post2) = p
    batch, seq, d = hp.shape
    nb, t_new, _ = hs.shape
    n_heads = d // (2 * HEAD_DIM)
    dw = n_heads * HEAD_DIM
    dff = w_down.shape[0]
    caches, plogf, conv_prev = past
    plen = plogf.shape[1]

    row2 = lambda v: v.reshape(1, -1).astype(F32)
    w_qkv = w_in[:, :6 * dw].astype(BF16)
    w_f = jnp.pad(w_in[:, 6 * dw:6 * dw + n_heads], ((0, 0), (0, LANES - n_heads))).astype(BF16)
    b_f = jnp.pad(b_forget.astype(F32), (0, LANES - n_heads)).reshape(1, LANES)
    w_ga = w_in[:, 6 * dw + n_heads:6 * dw + n_heads + d].astype(BF16)
    w_gb = w_in[:, 6 * dw + n_heads + d:].astype(BF16)
    one = jnp.ones((dw,), F32)
    colscale = jnp.concatenate([one * (DIFF_QK_DIM ** -0.5 * LOG2E), one, one,
                                one * (HEAD_DIM ** -0.5 * LOG2E), one, one]).reshape(1, 6 * dw)
    w_bd_b, w_bf_b, w_out_b = w_bd.astype(BF16), w_bf.astype(BF16), w_out.astype(BF16)
    w_a, w_b = w_up[:, :dff].astype(BF16), w_up[:, dff:].astype(BF16)
    w_d = w_down.astype(BF16)
    subln2 = row2(subln)

    def dense_tail(x2, od, of, edges, t_seq):
        m = x2.shape[0]
        tm = _tile(m, 512)
        h1 = _merge(x2, row2(pre1), od, of, w_bd_b, w_bf_b, w_ga, w_gb, w_out_b, row2(post1),
                    tm, _tile(d, 512))
        return _ffn(h1, row2(pre2), w_a, w_b, conv_w.astype(F32), row2(conv_b), w_d, row2(post2),
                    edges, t_seq, tm, _tile(dff, 1024))

    heads = lambda a, n, t: a.reshape(n, t, n_heads, HEAD_DIM)

    xp = hp.reshape(batch * seq, d)
    zb, kd, vd, kf, vf, logf, flp = _inproj(xp, row2(pre1), w_qkv, colscale, w_f, b_f, n_heads,
                                            _tile(batch * seq, 512))
    fcols = _cumsum(flp.reshape(batch, seq, LANES), n_heads, _tile(seq, 256), as_key_columns=True)
    od, of = _prompt_attention(zb, fcols, rel_table, lam_params, subln2, lam_init,
                               batch, seq, n_heads, _tile(seq, 256))
    hp_out, conv_p = dense_tail(xp, od, of, None, seq)
    state_p = (heads(kd, batch, seq), heads(vd, batch, seq), heads(kf, batch, seq), heads(vf, batch, seq),
               logf.reshape(batch, seq, n_heads), conv_p)

    xs = hs.reshape(nb * t_new, d)
    zb, kd, vd, kf, vf, logf, flp = _inproj(xs, row2(pre1), w_qkv, colscale, w_f, b_f, n_heads,
                                            _tile(nb * t_new, 512))
    tpad = -(-(plen + t_new) // LANES) * LANES
    flog = jnp.concatenate([jnp.pad(plogf.astype(F32), ((0, 0), (0, 0), (0, LANES - n_heads))),
                            flp.reshape(nb, t_new, LANES),
                            jnp.zeros((nb, tpad - plen - t_new, LANES), F32)], axis=1)
    frow = _cumsum(flog, n_heads, LANES, as_key_columns=False)
    od, of = _sample_attention(zb, caches, layer, frow, rel_table, lam_params, subln2, lam_init,
                               nb, t_new, plen, n_heads)
    cp = conv_prev.astype(F32)
    e0 = jnp.zeros((nb, t_new, dff), F32).at[:, 0].set(cp[:, 0]).at[:, 1].set(cp[:, 1])
    e1 = jnp.zeros((nb, t_new, dff), F32).at[:, 0].set(cp[:, 1])
    edges = (e0.reshape(nb * t_new, dff), e1.reshape(nb * t_new, dff))
    hs_out, conv_s = dense_tail(xs, od, of, edges, t_new)
    state_s = (heads(kd, nb, t_new), heads(vd, nb, t_new), heads(kf, nb, t_new), heads(vf, nb, t_new),
               logf.reshape(nb, t_new, n_heads), conv_s)
    return hp_out.reshape(batch, seq, d), hs_out.reshape(nb, t_new, d), state_p, state_s


def kernel(x_prompt, x_sample, cache_diff_k, cache_diff_v, cache_fox_k, cache_fox_v, cache_fox_logf,
           state_ffn_conv, rel_table, pre_norm1, w_in, b_forget, lam_q1, lam_k1, lam_q2, lam_k2,
           diff_subln, w_branch_diff, w_branch_fox, w_out, post_norm1, pre_norm2, w_up, conv_w, conv_b,
           w_down, post_norm2):
    depth = w_in.shape[0]
    hp, hs = x_prompt, x_sample
    caches = (cache_diff_k, cache_diff_v, cache_fox_k, cache_fox_v)
    new_p, new_s = [], []
    for l in range(depth):
        lam_init = 0.8 - 0.6 * math.exp(-0.3 * l)
        lam_params = jnp.stack([lam_q1[l], lam_k1[l], lam_q2[l], lam_k2[l]]).astype(F32)
        params = (pre_norm1[l], w_in[l], b_forget[l], diff_subln[l], w_branch_diff[l], w_branch_fox[l],
                  w_out[l], post_norm1[l], pre_norm2[l], w_up[l], conv_w[l], conv_b[l], w_down[l],
                  post_norm2[l])
        past = (caches, cache_fox_logf[l], state_ffn_conv[l])
        hp, hs, sp, ss = _layer(l, hp, hs, past, rel_table, lam_params, lam_init, params)
        new_p.append(sp)
        new_s.append(ss)
    st = lambda lst, i: jnp.stack([e[i] for e in lst])
    return (hp, hs,
            st(new_p, 0), st(new_p, 1), st(new_p, 2), st(new_p, 3), st(new_p, 4), st(new_p, 5),
            st(new_s, 0), st(new_s, 1), st(new_s, 2), st(new_s, 3), st(new_s, 4), st(new_s, 5))
```

```python
import functools
import math

import jax
import jax.numpy as jnp
from jax import lax
from jax.experimental import pallas as pl
from jax.experimental.pallas import tpu as pltpu

HEAD_DIM = 128
DIFF_QK_DIM = HEAD_DIM // 2
CHUNK = 64
CONV_WIDTH = 3
REL_BUCKETS = 32
REL_MAX_DIST = 128
EPS = 1e-6
NEG_INF = -1e30
LOG2E = math.log2(math.e)

LANES = 128
SUBLANES = 8
ONES_ROWS = 16
ATTN_TILE = 256
FAR_UNROLL = 4
VMEM_LIMIT_BYTES = 60 * 1024 * 1024

F32 = jnp.float32
BF16 = jnp.bfloat16


def _params(*sem, flags=None):
    return pltpu.CompilerParams(dimension_semantics=sem, vmem_limit_bytes=VMEM_LIMIT_BYTES, flags=flags)


def _rms_rows(x, g):
    return x * lax.rsqrt(jnp.mean(x * x, axis=-1, keepdims=True) + EPS) * g


def _dot(a, b):
    return jnp.dot(a, b, preferred_element_type=F32)


def _dot_nt(a, b):
    return lax.dot_general(a, b, (((1,), (1,)), ((), ())), preferred_element_type=F32)


def _split3(x):
    hi = x.astype(BF16)
    r1 = x - hi.astype(F32)
    mid = r1.astype(BF16)
    lo = (r1 - mid.astype(F32)).astype(BF16)
    return hi, mid, lo


def _split3_on_axis(x, index):
    hi, mid, lo = (v.astype(F32) for v in _split3(x))
    return jnp.where(index == 0, hi, jnp.where(index == 1, mid, jnp.where(index == 2, lo, 0.0))).astype(BF16)


def _transpose_bf16(x):
    return x.astype(F32).T.astype(BF16)


def _inproj_kernel(x_ref, g_ref, w_ref, cs_ref, wf_ref, bf_ref,
                   zb_ref, kd_ref, vd_ref, kf_ref, vf_ref, logf_ref, flp_ref, xn_ref, *, n_heads):
    j = pl.program_id(1)

    @pl.when(j == 0)
    def _():
        xn = _rms_rows(x_ref[...], g_ref[...]).astype(BF16)
        xn_ref[...] = xn
        fl = _dot(xn, wf_ref[...]) + bf_ref[...]
        lf = jnp.minimum(fl, 0.0) - jnp.log1p(jnp.exp(-jnp.abs(fl)))
        flp_ref[...] = lf
        logf_ref[...] = lf[:, :n_heads]

    z = _dot(xn_ref[...], w_ref[...])
    zb_ref[...] = (z * cs_ref[...]).astype(BF16)
    for seg, ref in ((1, kd_ref), (2, vd_ref), (4, kf_ref), (5, vf_ref)):
        @pl.when(j == seg)
        def _(ref=ref):
            ref[...] = z


def _inproj(x, g, w_qkv, colscale, w_f, b_f, n_heads, tm):
    m, d = x.shape
    dw = w_qkv.shape[1] // 6
    row = lambda i, j: (i, 0)
    f32_out = jax.ShapeDtypeStruct((m, dw), F32)
    return pl.pallas_call(
        functools.partial(_inproj_kernel, n_heads=n_heads),
        grid=(m // tm, 6),
        in_specs=[
            pl.BlockSpec((tm, d), row),
            pl.BlockSpec((1, d), lambda i, j: (0, 0)),
            pl.BlockSpec((d, dw), lambda i, j: (0, j)),
            pl.BlockSpec((1, dw), lambda i, j: (0, j)),
            pl.BlockSpec((d, LANES), lambda i, j: (0, 0)),
            pl.BlockSpec((1, LANES), lambda i, j: (0, 0)),
        ],
        out_specs=[
            pl.BlockSpec((tm, dw), lambda i, j: (i, j)),
            pl.BlockSpec((tm, dw), row),
            pl.BlockSpec((tm, dw), row),
            pl.BlockSpec((tm, dw), row),
            pl.BlockSpec((tm, dw), row),
            pl.BlockSpec((tm, n_heads), row),
            pl.BlockSpec((tm, LANES), row),
        ],
        out_shape=[
            jax.ShapeDtypeStruct((m, 6 * dw), BF16),
            f32_out, f32_out, f32_out, f32_out,
            jax.ShapeDtypeStruct((m, n_heads), F32),
            jax.ShapeDtypeStruct((m, LANES), F32),
        ],
        scratch_shapes=[pltpu.VMEM((tm, d), BF16)],
        compiler_params=_params("arbitrary", "arbitrary"),
        name="inproj",
    )(x, g, w_qkv, colscale, w_f, b_f)


def _cumsum_kernel(x_ref, o_ref, *, chunk, n_heads, as_key_columns):
    t = x_ref.shape[0]
    r = lax.broadcasted_iota(jnp.int32, (chunk, chunk), 0)
    c = lax.broadcasted_iota(jnp.int32, (chunk, chunk), 1)
    tri = (r >= c).astype(BF16)
    lane = lax.broadcasted_iota(jnp.int32, (chunk, LANES), 1)
    carry = jnp.zeros((1, LANES), F32)
    for k in range(t // chunk):
        hi, mid, lo = _split3(x_ref[k * chunk:(k + 1) * chunk, :])
        f = _dot(tri, hi) + _dot(tri, mid) + _dot(tri, lo) + carry
        carry = f[chunk - 1:chunk, :]
        if as_key_columns:
            for h in range(n_heads):
                o_ref[h, k * chunk:(k + 1) * chunk, :] = _split3_on_axis(
                    jnp.broadcast_to(f[:, h:h + 1] * LOG2E, (chunk, LANES)), lane)
        else:
            o_ref[:, k * chunk:(k + 1) * chunk] = f.T[:n_heads, :]


def _cumsum(flp, n_heads, chunk, as_key_columns):
    nb, t, _ = flp.shape
    if as_key_columns:
        out_spec = pl.BlockSpec((None, n_heads, t, LANES), lambda b: (b, 0, 0, 0))
        out_shape = jax.ShapeDtypeStruct((nb, n_heads, t, LANES), BF16)
    else:
        out_spec = pl.BlockSpec((None, n_heads, t), lambda b: (b, 0, 0))
        out_shape = jax.ShapeDtypeStruct((nb, n_heads, t), F32)
    return pl.pallas_call(
        functools.partial(_cumsum_kernel, chunk=chunk, n_heads=n_heads, as_key_columns=as_key_columns),
        grid=(nb,),
        in_specs=[pl.BlockSpec((None, t, LANES), lambda b: (b, 0, 0))],
        out_specs=out_spec,
        out_shape=out_shape,
        compiler_params=_params("arbitrary"),
        name="cumsum_keys" if as_key_columns else "cumsum_rows",
    )(flp)


def _t5_bucket(rel):
    nb = REL_BUCKETS // 2
    max_exact = nb // 2
    n = jnp.abs(rel)
    nf = jnp.maximum(n, 1).astype(jnp.float32)
    large = max_exact + (jnp.log(nf / max_exact) / math.log(REL_MAX_DIST / max_exact)
                         * (nb - max_exact)).astype(jnp.int32)
    large = jnp.minimum(large, nb - 1)
    return jnp.where(rel > 0, nb, 0) + jnp.where(n < max_exact, n, large)


def _bias_kernel(table_ref, idx_ref, mask_ref, o_ref, *, minus_bucket):
    h = pl.program_id(0)
    idx = idx_ref[...]
    acc = mask_ref[...]
    if minus_bucket is not None:
        acc = acc - table_ref[minus_bucket, h]
    for b in range(REL_BUCKETS):
        acc = acc + jnp.where(idx == b, table_ref[b, h], 0.0)
    o_ref[...] = acc * LOG2E


def _bias_tiles(rel_table, qpos, kpos, keys_major=False, minus_bucket=None):
    n_heads = rel_table.shape[1]
    if keys_major:
        qp, kp = qpos[:, None, :], kpos[:, :, None]
    else:
        qp, kp = qpos[:, :, None], kpos[:, None, :]
    idx = _t5_bucket(kp - qp).astype(jnp.int32)
    mask = jnp.where((kp // CHUNK) <= (qp // CHUNK), 0.0, NEG_INF).astype(F32)
    nt, r, c = idx.shape
    return pl.pallas_call(
        functools.partial(_bias_kernel, minus_bucket=minus_bucket),
        grid=(n_heads, nt),
        in_specs=[
            pl.BlockSpec(memory_space=pltpu.SMEM),
            pl.BlockSpec((None, r, c), lambda h, t: (t, 0, 0)),
            pl.BlockSpec((None, r, c), lambda h, t: (t, 0, 0)),
        ],
        out_specs=pl.BlockSpec((None, None, r, c), lambda h, t: (h, t, 0, 0)),
        out_shape=jax.ShapeDtypeStruct((n_heads, nt, r, c), F32),
        compiler_params=_params("arbitrary", "arbitrary"),
        name="bias_tiles",
    )(rel_table.astype(F32), idx, mask)


def _far_bucket(min_dist):
    nb = REL_BUCKETS // 2
    max_exact = nb // 2
    large = max_exact + math.log(min_dist / max_exact) / math.log(REL_MAX_DIST / max_exact) * (nb - max_exact)
    assert large >= nb - 1 + 0.5, "key tile too short for a constant far-field bias"
    return nb - 1


def _lambda_value(lam_ref, lam_init):
    a = lam_ref[...]
    s1 = jnp.sum(a[0:1] * a[1:2], axis=-1, keepdims=True)
    s2 = jnp.sum(a[2:3] * a[3:4], axis=-1, keepdims=True)
    return jnp.exp(s1) - jnp.exp(s2) + lam_init


def _prompt_attn_kernel(table_ref, lam_ref, subln_ref, qd_ref, kd_ref, vd_ref, qf_ref, kf_ref, vf_ref,
                        fcol_ref, bias_ref, cmask_ref, od_ref, of_ref,
                        kda_ref, kfa_ref, vdt_ref, vft_ref,
                        md_ref, accd_ref, mf_ref, accf_ref,
                        sd_ref, pd_ref, ad_ref, sf_ref, pf_ref, af_ref,
                        *, tile, lam_init, far_bucket):
    h = pl.program_id(1)
    qi = pl.program_id(2)
    t = tile
    seq = kd_ref.shape[0]

    @pl.when(qi == 0)
    def _():
        lane = lax.broadcasted_iota(jnp.int32, (seq, LANES), 1)
        kda_ref[:, :HEAD_DIM] = kd_ref[...]
        kda_ref[:, HEAD_DIM:] = jnp.where(lane < 3, 1.0, 0.0).astype(BF16)
        kfa_ref[:, :HEAD_DIM] = kf_ref[...]
        kfa_ref[:, HEAD_DIM:] = fcol_ref[...]

        ones = jnp.ones((ONES_ROWS, t), BF16)

        def transpose_values(c, carry):
            r = pl.ds(pl.multiple_of(c * t, t), t)
            vdt_ref[c] = jnp.concatenate([_transpose_bf16(vd_ref[r, :]), ones], axis=0)
            vft_ref[c] = jnp.concatenate([_transpose_bf16(vf_ref[r, :]), ones], axis=0)
            return carry

        lax.fori_loop(0, seq // t, transpose_values, 0)

    def key_rows(ref, j):
        return ref[pl.ds(pl.multiple_of(j * t, t), t), :]

    row = lax.broadcasted_iota(jnp.int32, (HEAD_DIM, t), 0)
    qdt = qd_ref[...].astype(F32).T
    zero = jnp.zeros_like(qdt)
    top = jnp.concatenate([jnp.where(row < DIFF_QK_DIM, qdt, zero),
                           jnp.where(row >= DIFF_QK_DIM, qdt, zero)], axis=1).astype(BF16)
    row2 = lax.broadcasted_iota(jnp.int32, (HEAD_DIM, 2 * t), 0)
    bottom = _split3_on_axis(jnp.full((HEAD_DIM, 2 * t), table_ref[far_bucket, h] * LOG2E, F32), row2)
    qd_aug = jnp.concatenate([top, bottom], axis=0)
    qf_aug = jnp.concatenate([qf_ref[...].astype(F32).T.astype(BF16),
                              jnp.where(row < 3, -1.0, 0.0).astype(BF16)], axis=0)

    branches = ((kda_ref, qd_aug, vdt_ref, sd_ref, pd_ref, ad_ref, md_ref, accd_ref),
                (kfa_ref, qf_aug, vft_ref, sf_ref, pf_ref, af_ref, mf_ref, accf_ref))

    def pipeline_step(j, adds, score_next):
        for (k_ref, q_aug, vt_ref, s_ref, p_ref, a_ref, m_ref, acc_ref), add in zip(branches, adds):
            acc_ref[...] = a_ref[...] * acc_ref[...] + _dot(vt_ref[jnp.maximum(j - 1, 0)], p_ref[...])
            s = s_ref[...]
            if add is not None:
                s = s + add
            m_prev = m_ref[...]
            m_new = jnp.maximum(m_prev, jnp.max(s, axis=0, keepdims=True))
            p_ref[...] = jnp.exp2(s - m_new).astype(BF16)
            a_ref[...] = jnp.exp2(m_prev - m_new)
            m_ref[...] = m_new
            if score_next:
                s_ref[...] = _dot(key_rows(k_ref, j + 1), q_aug)

    for k_ref, q_aug, vt_ref, s_ref, p_ref, a_ref, m_ref, acc_ref in branches:
        m_ref[...] = jnp.full_like(m_ref, NEG_INF)
        acc_ref[...] = jnp.zeros_like(acc_ref)
        s_ref[...] = _dot(key_rows(k_ref, 0), q_aug)
        p_ref[...] = jnp.zeros_like(p_ref)
        a_ref[...] = jnp.ones_like(a_ref)

    def far_tiles(unroll):
        def body(i, carry):
            for u in range(unroll):
                pipeline_step(i * unroll + u, (None, None), True)
            return carry
        return body

    n_far = jnp.maximum(qi - 1, 0)
    n_main = n_far // FAR_UNROLL
    lax.fori_loop(0, n_main, far_tiles(FAR_UNROLL), 0)
    lax.fori_loop(n_main * FAR_UNROLL, n_far, far_tiles(1), 0)

    @pl.when(qi >= 1)
    def _():
        bias = bias_ref[0]
        pipeline_step(qi - 1, (jnp.concatenate([bias, bias], axis=1), None), True)

    bias = bias_ref[1]
    pipeline_step(qi, (jnp.concatenate([bias, bias], axis=1), cmask_ref[...]), False)
    for k_ref, q_aug, vt_ref, s_ref, p_ref, a_ref, m_ref, acc_ref in branches:
        acc_ref[...] = a_ref[...] * acc_ref[...] + _dot(vt_ref[qi], p_ref[...])

    lam = _lambda_value(lam_ref, lam_init)
    acc = accd_ref[...]
    num, den = acc[:HEAD_DIM], acc[HEAD_DIM:HEAD_DIM + 1]
    odt = num[:, :t] / den[:, :t] - lam * (num[:, t:] / den[:, t:])
    od_ref[...] = (_rms_rows(odt.T, subln_ref[...]) * (1.0 - lam_init)).astype(BF16)
    acc = accf_ref[...]
    of_ref[...] = (acc[:HEAD_DIM] / acc[HEAD_DIM:HEAD_DIM + 1]).T.astype(BF16)


def _prompt_attention(zb, fcols, rel_table, lam_params, subln, lam_init, batch, seq, n_heads, tile):
    m = batch * seq
    dw = n_heads * HEAD_DIM
    nq = seq // tile
    t = tile
    far_bucket = _far_bucket(t + 1)
    vrows = HEAD_DIM + ONES_ROWS
    ar = jnp.arange(t, dtype=jnp.int32)
    bias = _bias_tiles(rel_table, jnp.stack([ar + t, ar + t]), jnp.stack([ar, ar + t]),
                       keys_major=True, minus_bucket=far_bucket)
    cmask = jnp.where(ar[:, None] <= ar[None, :], 0.0, NEG_INF).astype(F32)

    def q_spec(seg):
        return pl.BlockSpec((t, HEAD_DIM), lambda b, h, q, seg=seg: (b * nq + q, seg * n_heads + h))

    def kv_spec(seg):
        return pl.BlockSpec((seq, HEAD_DIM), lambda b, h, q, seg=seg: (b, seg * n_heads + h))

    out_spec = pl.BlockSpec((t, HEAD_DIM), lambda b, h, q: (b * nq + q, h))
    const2 = lambda b, h, q: (0, 0)
    kern = functools.partial(_prompt_attn_kernel, tile=t, lam_init=lam_init, far_bucket=far_bucket)
    return pl.pallas_call(
        kern,
        grid=(batch, n_heads, nq),
        in_specs=[
            pl.BlockSpec(memory_space=pltpu.SMEM),
            pl.BlockSpec((4, DIFF_QK_DIM), const2),
            pl.BlockSpec((1, HEAD_DIM), const2),
            q_spec(0), kv_spec(1), kv_spec(2), q_spec(3), kv_spec(4), kv_spec(5),
            pl.BlockSpec((None, None, seq, LANES), lambda b, h, q: (b, h, 0, 0)),
            pl.BlockSpec((None, 2, t, t), lambda b, h, q: (h, 0, 0, 0)),
            pl.BlockSpec((t, t), const2),
        ],
        out_specs=[out_spec, out_spec],
        out_shape=[jax.ShapeDtypeStruct((m, dw), BF16), jax.ShapeDtypeStruct((m, dw), BF16)],
        scratch_shapes=[
            pltpu.VMEM((seq, 2 * HEAD_DIM), BF16), pltpu.VMEM((seq, 2 * HEAD_DIM), BF16),
            pltpu.VMEM((nq, vrows, t), BF16), pltpu.VMEM((nq, vrows, t), BF16),
            pltpu.VMEM((1, 2 * t), F32), pltpu.VMEM((vrows, 2 * t), F32),
            pltpu.VMEM((1, t), F32), pltpu.VMEM((vrows, t), F32),
            pltpu.VMEM((t, 2 * t), F32), pltpu.VMEM((t, 2 * t), BF16), pltpu.VMEM((1, 2 * t), F32),
            pltpu.VMEM((t, t), F32), pltpu.VMEM((t, t), BF16), pltpu.VMEM((1, t), F32),
        ],
        compiler_params=_params("arbitrary", "arbitrary", "arbitrary"),
        name="prompt_attention",
    )(rel_table.astype(F32), lam_params, subln, zb, zb, zb, zb, zb, zb, fcols, bias, cmask)


def _split_diff_queries(q):
    lane = lax.broadcasted_iota(jnp.int32, q.shape, 1)
    zero = jnp.zeros_like(q)
    return jnp.concatenate([jnp.where(lane < DIFF_QK_DIM, q, zero),
                            jnp.where(lane >= DIFF_QK_DIM, q, zero)], axis=0)


def _sample_attn_kernel(lam_ref, subln_ref, z_ref, kdc_ref, vdc_ref, kfc_ref, vfc_ref, frow_ref, bias_ref,
                        cmask_ref, od_ref, of_ref, *, past, t_new, n_heads, lam_init):
    dw = n_heads * HEAD_DIM

    def cached(ref, h):
        return ref[pl.ds(h, past, stride=n_heads), :].astype(BF16)

    def new_rows(seg, h):
        lo = seg * dw + h * HEAD_DIM
        return z_ref[:, lo:lo + HEAD_DIM]

    def attend(q, kc, vc, kn, vn, add_c, add_n):
        sc = _dot_nt(q, kc) + add_c
        sn = _dot_nt(q, kn) + add_n
        mx = jnp.maximum(jnp.max(sc, axis=-1, keepdims=True), jnp.max(sn, axis=-1, keepdims=True))
        pc = jnp.exp2(sc - mx)
        pn = jnp.exp2(sn - mx)
        l = jnp.sum(pc, axis=-1, keepdims=True) + jnp.sum(pn, axis=-1, keepdims=True)
        acc = _dot(pc.astype(BF16), vc) + _dot(pn.astype(BF16), vn)
        return acc, l

    lam = _lambda_value(lam_ref, lam_init)
    cmask = cmask_ref[...]
    for h in range(n_heads):
        cols = slice(h * HEAD_DIM, (h + 1) * HEAD_DIM)
        qs = _split_diff_queries(new_rows(0, h))
        bias_c = bias_ref[h, :, :past]
        bias_n = bias_ref[h, :, past:past + t_new]
        acc, l = attend(qs, cached(kdc_ref, h), cached(vdc_ref, h),
                        new_rows(1, h), new_rows(2, h),
                        jnp.concatenate([bias_c, bias_c], axis=0), jnp.concatenate([bias_n, bias_n], axis=0))
        od = acc[:t_new] / l[:t_new] - lam * (acc[t_new:] / l[t_new:])
        od_ref[:, cols] = (_rms_rows(od, subln_ref[...]) * (1.0 - lam_init)).astype(BF16)

        f0 = frow_ref[h:h + 1, past:past + 1]
        dec_c = (f0 - frow_ref[h:h + 1, :past]) * LOG2E
        dec_n = (f0 - frow_ref[h:h + 1, past:past + t_new]) * LOG2E
        acc, l = attend(new_rows(3, h), cached(kfc_ref, h), cached(vfc_ref, h),
                        new_rows(4, h), new_rows(5, h), dec_c, dec_n + cmask)
        of_ref[:, cols] = (acc / l).astype(BF16)


def _sample_attention(zb, caches, layer, frow, rel_table, lam_params, subln, lam_init, nb, t_new, past, n_heads):
    dw = n_heads * HEAD_DIM
    tpad = frow.shape[-1]
    ar = jnp.arange(t_new, dtype=jnp.int32)
    bias = _bias_tiles(rel_table, (past + ar)[None], jnp.arange(tpad, dtype=jnp.int32)[None])[:, 0]
    cmask = jnp.where(ar[None, :] <= ar[:, None], 0.0, NEG_INF).astype(F32)
    caches = [c.reshape(c.shape[0], nb, past * n_heads, HEAD_DIM) for c in caches]
    cache_spec = pl.BlockSpec((None, None, past * n_heads, HEAD_DIM), lambda b: (layer, b, 0, 0))
    out_spec = pl.BlockSpec((t_new, dw), lambda b: (b, 0))
    const2 = lambda b: (0, 0)
    kern = functools.partial(_sample_attn_kernel, past=past, t_new=t_new, n_heads=n_heads, lam_init=lam_init)
    return pl.pallas_call(
        kern,
        grid=(nb,),
        in_specs=[
            pl.BlockSpec((4, DIFF_QK_DIM), const2),
            pl.BlockSpec((1, HEAD_DIM), const2),
            pl.BlockSpec((t_new, 6 * dw), lambda b: (b, 0)),
            cache_spec, cache_spec, cache_spec, cache_spec,
            pl.BlockSpec((None, n_heads, tpad), lambda b: (b, 0, 0)),
            pl.BlockSpec((n_heads, t_new, tpad), lambda b: (0, 0, 0)),
            pl.BlockSpec((t_new, t_new), const2),
        ],
        out_specs=[out_spec, out_spec],
        out_shape=[jax.ShapeDtypeStruct((nb * t_new, dw), BF16), jax.ShapeDtypeStruct((nb * t_new, dw), BF16)],
        compiler_params=_params("arbitrary"),
        name="sample_attention",
    )(lam_params, subln, zb, *caches, frow, bias, cmask)


def _merge_kernel(x_ref, g1_ref, od_ref, of_ref, wbd_ref, wbf_ref, wga_ref, wgb_ref, wout_ref, gp_ref,
                  o_ref, xn_ref, u_ref):
    n = pl.program_id(1)

    @pl.when(n == 0)
    def _():
        xn_ref[...] = _rms_rows(x_ref[...], g1_ref[...]).astype(BF16)

    xn = xn_ref[...]
    yd = _dot(od_ref[...], wbd_ref[...])
    yf = _dot(of_ref[...], wbf_ref[...])
    ga = jax.nn.sigmoid(_dot(xn, wga_ref[...]))
    gb = jax.nn.sigmoid(_dot(xn, wgb_ref[...]))
    u_ref[n] = (ga * yd + gb * yf).astype(BF16)

    @pl.when(n == pl.num_programs(1) - 1)
    def _():
        u = jnp.concatenate([u_ref[k] for k in range(u_ref.shape[0])], axis=1)
        o = _dot(u, wout_ref[...])
        o_ref[...] = x_ref[...] + _rms_rows(o, gp_ref[...])


def _merge(x, g1, od, of, w_bd, w_bf, w_ga, w_gb, w_out, g_post, tm, tn):
    m, d = x.shape
    dw = od.shape[1]
    nn = d // tn
    row = lambda i, n: (i, 0)
    col = lambda i, n: (0, n)
    const2 = lambda i, n: (0, 0)
    return pl.pallas_call(
        _merge_kernel,
        grid=(m // tm, nn),
        in_specs=[
            pl.BlockSpec((tm, d), row),
            pl.BlockSpec((1, d), const2),
            pl.BlockSpec((tm, dw), row),
            pl.BlockSpec((tm, dw), row),
            pl.BlockSpec((dw, tn), col),
            pl.BlockSpec((dw, tn), col),
            pl.BlockSpec((d, tn), col),
            pl.BlockSpec((d, tn), col),
            pl.BlockSpec((d, d), const2, pipeline_mode=pl.Buffered(1)),
            pl.BlockSpec((1, d), const2),
        ],
        out_specs=pl.BlockSpec((tm, d), row),
        out_shape=jax.ShapeDtypeStruct((m, d), F32),
        scratch_shapes=[pltpu.VMEM((tm, d), BF16), pltpu.VMEM((nn, tm, tn), BF16)],
        compiler_params=_params("arbitrary", "arbitrary"),
        name="merge",
    )(x, g1, od, of, w_bd, w_bf, w_ga, w_gb, w_out, g_post)


def _gelu_tanh(x):
    return 0.5 * x * (1.0 + jnp.tanh(math.sqrt(2.0 / math.pi) * (x + 0.044715 * (x * x * x))))


def _ffn_kernel(*refs, seq, has_edges):
    if has_edges:
        (h_ref, g2_ref, wa_ref, wb_ref, cw_ref, cb_ref, wd_ref, gp_ref, e0_ref, e1_ref,
         o_ref, cs_ref, xn_ref, tail_ref) = refs
    else:
        (h_ref, g2_ref, wa_ref, wb_ref, cw_ref, cb_ref, wd_ref, gp_ref,
         o_ref, cs_ref, xn_ref, tail_ref) = refs
    i = pl.program_id(0)
    f = pl.program_id(1)
    tm = h_ref.shape[0]

    @pl.when(f == 0)
    def _():
        xn_ref[...] = _rms_rows(h_ref[...], g2_ref[...]).astype(BF16)

    xn = xn_ref[...]
    a = _dot(xn, wa_ref[...])
    gate = _dot(xn, wb_ref[...])
    row = lax.broadcasted_iota(jnp.int32, a.shape, 0)
    back1 = pltpu.roll(a, 1, 0)
    back2 = pltpu.roll(a, 2, 0)
    if has_edges:
        t = row % seq
        am1 = jnp.where(t >= 1, back1, 0.0) + e1_ref[...]
        am2 = jnp.where(t >= 2, back2, 0.0) + e0_ref[...]
        cs_ref[...] = a.reshape(tm // seq, seq, a.shape[1])[:, seq - (CONV_WIDTH - 1):, :]
    else:
        @pl.when((i * tm) % seq == 0)
        def _():
            tail_ref[f] = jnp.zeros(tail_ref.shape[1:], F32)

        prev = tail_ref[f]
        p0 = prev[SUBLANES - 2:SUBLANES - 1]
        p1 = prev[SUBLANES - 1:SUBLANES]
        am1 = jnp.where(row == 0, p1, back1)
        am2 = jnp.where(row == 0, p0, jnp.where(row == 1, p1, back2))
        tail_ref[f] = a[tm - SUBLANES:, :]
        cs_ref[...] = a[tm - (CONV_WIDTH - 1):, :][None]
    cw = cw_ref[...]
    ac = cw[0:1] * am2 + cw[1:2] * am1 + cw[2:3] * a + cb_ref[...]
    g = (_gelu_tanh(ac) * gate).astype(BF16)
    contrib = _dot(g, wd_ref[...])

    @pl.when(f == 0)
    def _():
        o_ref[...] = contrib

    @pl.when(f > 0)
    def _():
        o_ref[...] += contrib

    @pl.when(f == pl.num_programs(1) - 1)
    def _():
        o_ref[...] = h_ref[...] + _rms_rows(o_ref[...], gp_ref[...])


def _ffn(h, g2, w_a, w_b, conv_w, conv_b, w_d, g_post, edges, seq, tm, tf):
    m, d = h.shape
    dff = w_a.shape[1]
    nf = dff // tf
    has_edges = edges is not None
    row = lambda i, f: (i, 0)
    col = lambda i, f: (0, f)
    const2 = lambda i, f: (0, 0)
    in_specs = [
        pl.BlockSpec((tm, d), row),
        pl.BlockSpec((1, d), const2),
        pl.BlockSpec((d, tf), col),
        pl.BlockSpec((d, tf), col),
        pl.BlockSpec((CONV_WIDTH, tf), col),
        pl.BlockSpec((1, tf), col),
        pl.BlockSpec((tf, d), lambda i, f: (f, 0)),
        pl.BlockSpec((1, d), const2),
    ]
    args = [h, g2, w_a, w_b, conv_w, conv_b, w_d, g_post]
    if has_edges:
        assert tm % seq == 0
        in_specs += [pl.BlockSpec((tm, tf), lambda i, f: (i, f))] * 2
        args += list(edges)
        tails_per_tile = tm // seq
    else:
        assert seq % tm == 0
        tails_per_tile = 1
    n_tails = (m // tm) * tails_per_tile
    out, tails = pl.pallas_call(
        functools.partial(_ffn_kernel, seq=seq, has_edges=has_edges),
        grid=(m // tm, nf),
        in_specs=in_specs,
        out_specs=[pl.BlockSpec((tm, d), row),
                   pl.BlockSpec((tails_per_tile, CONV_WIDTH - 1, tf), lambda i, f: (i, 0, f))],
        out_shape=[jax.ShapeDtypeStruct((m, d), F32),
                   jax.ShapeDtypeStruct((n_tails, CONV_WIDTH - 1, dff), F32)],
        scratch_shapes=[pltpu.VMEM((tm, d), BF16), pltpu.VMEM((nf, SUBLANES, tf), F32)],
        compiler_params=_params("arbitrary", "arbitrary"),
        name="conv_ffn",
    )(*args)
    if not has_edges:
        tiles_per_seq = seq // tm
        tails = tails[tiles_per_seq - 1::tiles_per_seq]
    return out, tails


def _tile(n, cap):
    t = min(n, cap)
    assert n % t == 0
    return t


def _layer(layer, hp, hs, past, rel_table, lam_params, lam_init, p):
    (pre1, w_in, b_forget, subln, w_bd, w_bf, w_out, post1, pre2, w_up, conv_w, conv_b, w_down, post2) = p
    batch, seq, d = hp.shape
    nb, t_new, _ = hs.shape
    n_heads = d // (2 * HEAD_DIM)
    dw = n_heads * HEAD_DIM
    dff = w_down.shape[0]
    caches, plogf, conv_prev = past
    plen = plogf.shape[1]

    row2 = lambda v: v.reshape(1, -1).astype(F32)
    w_qkv = w_in[:, :6 * dw].astype(BF16)
    w_f = jnp.pad(w_in[:, 6 * dw:6 * dw + n_heads], ((0, 0), (0, LANES - n_heads))).astype(BF16)
    b_f = jnp.pad(b_forget.astype(F32), (0, LANES - n_heads)).reshape(1, LANES)
    w_ga = w_in[:, 6 * dw + n_heads:6 * dw + n_heads + d].astype(BF16)
    w_gb = w_in[:, 6 * dw + n_heads + d:].astype(BF16)
    one = jnp.ones((dw,), F32)
    colscale = jnp.concatenate([one * (DIFF_QK_DIM ** -0.5 * LOG2E), one, one,
                                one * (HEAD_DIM ** -0.5 * LOG2E), one, one]).reshape(1, 6 * dw)
    w_bd_b, w_bf_b, w_out_b = w_bd.astype(BF16), w_bf.astype(BF16), w_out.astype(BF16)
    w_a, w_b = w_up[:, :dff].astype(BF16), w_up[:, dff:].astype(BF16)
    w_d = w_down.astype(BF16)
    subln2 = row2(subln)

    def dense_tail(x2, od, of, edges, t_seq):
        m = x2.shape[0]
        tm = _tile(m, 512)
        h1 = _merge(x2, row2(pre1), od, of, w_bd_b, w_bf_b, w_ga, w_gb, w_out_b, row2(post1),
                    tm, _tile(d, 512))
        return _ffn(h1, row2(pre2), w_a, w_b, conv_w.astype(F32), row2(conv_b), w_d, row2(post2),
                    edges, t_seq, tm, _tile(dff, 1024))

    heads = lambda a, n, t: a.reshape(n, t, n_heads, HEAD_DIM)

    xp = hp.reshape(batch * seq, d)
    zb, kd, vd, kf, vf, logf, flp = _inproj(xp, row2(pre1), w_qkv, colscale, w_f, b_f, n_heads,
                                            _tile(batch * seq, 512))
    fcols = _cumsum(flp.reshape(batch, seq, LANES), n_heads, _tile(seq, 256), as_key_columns=True)
    od, of = _prompt_attention(zb, fcols, rel_table, lam_params, subln2, lam_init,
                               batch, seq, n_heads, _tile(seq, ATTN_TILE))
    hp_out, conv_p = dense_tail(xp, od, of, None, seq)
    state_p = (heads(kd, batch, seq), heads(vd, batch, seq), heads(kf, batch, seq), heads(vf, batch, seq),
               logf.reshape(batch, seq, n_heads), conv_p)

    xs = hs.reshape(nb * t_new, d)
    zb, kd, vd, kf, vf, logf, flp = _inproj(xs, row2(pre1), w_qkv, colscale, w_f, b_f, n_heads,
                                            _tile(nb * t_new, 512))
    tpad = -(-(plen + t_new) // LANES) * LANES
    flog = jnp.concatenate([jnp.pad(plogf.astype(F32), ((0, 0), (0, 0), (0, LANES - n_heads))),
                            flp.reshape(nb, t_new, LANES),
                            jnp.zeros((nb, tpad - plen - t_new, LANES), F32)], axis=1)
    frow = _cumsum(flog, n_heads, LANES, as_key_columns=False)
    od, of = _sample_attention(zb, caches, layer, frow, rel_table, lam_params, subln2, lam_init,
                               nb, t_new, plen, n_heads)
    cp = conv_prev.astype(F32)
    e0 = jnp.zeros((nb, t_new, dff), F32).at[:, 0].set(cp[:, 0]).at[:, 1].set(cp[:, 1])
    e1 = jnp.zeros((nb, t_new, dff), F32).at[:, 0].set(cp[:, 1])
    edges = (e0.reshape(nb * t_new, dff), e1.reshape(nb * t_new, dff))
    hs_out, conv_s = dense_tail(xs, od, of, edges, t_new)
    state_s = (heads(kd, nb, t_new), heads(vd, nb, t_new), heads(kf, nb, t_new), heads(vf, nb, t_new),
               logf.reshape(nb, t_new, n_heads), conv_s)
    return hp_out.reshape(batch, seq, d), hs_out.reshape(nb, t_new, d), state_p, state_s


def kernel(x_prompt, x_sample, cache_diff_k, cache_diff_v, cache_fox_k, cache_fox_v, cache_fox_logf,
           state_ffn_conv, rel_table, pre_norm1, w_in, b_forget, lam_q1, lam_k1, lam_q2, lam_k2,
           diff_subln, w_branch_diff, w_branch_fox, w_out, post_norm1, pre_norm2, w_up, conv_w, conv_b,
           w_down, post_norm2):
    depth = w_in.shape[0]
    hp, hs = x_prompt, x_sample
    caches = (cache_diff_k, cache_diff_v, cache_fox_k, cache_fox_v)
    new_p, new_s = [], []
    for l in range(depth):
        lam_init = 0.8 - 0.6 * math.exp(-0.3 * l)
        lam_params = jnp.stack([lam_q1[l], lam_k1[l], lam_q2[l], lam_k2[l]]).astype(F32)
        params = (pre_norm1[l], w_in[l], b_forget[l], diff_subln[l], w_branch_diff[l], w_branch_fox[l],
                  w_out[l], post_norm1[l], pre_norm2[l], w_up[l], conv_w[l], conv_b[l], w_down[l],
                  post_norm2[l])
        past = (caches, cache_fox_logf[l], state_ffn_conv[l])
        hp, hs, sp, ss = _layer(l, hp, hs, past, rel_table, lam_params, lam_init, params)
        new_p.append(sp)
        new_s.append(ss)
    st = lambda lst, i: jnp.stack([e[i] for e in lst])
    return (hp, hs,
            st(new_p, 0), st(new_p, 1), st(new_p, 2), st(new_p, 3), st(new_p, 4), st(new_p, 5),
            st(new_s, 0), st(new_s, 1), st(new_s, 2), st(new_s, 3), st(new_s, 4), st(new_s, 5))
```

```python
import functools
import math

import jax
import jax.numpy as jnp
from jax import lax
from jax.experimental import pallas as pl
from jax.experimental.pallas import tpu as pltpu

HEAD_DIM = 128
DIFF_QK_DIM = HEAD_DIM // 2
CHUNK = 64
CONV_WIDTH = 3
REL_BUCKETS = 32
REL_MAX_DIST = 128
EPS = 1e-6
NEG_INF = -1e30
LOG2E = math.log2(math.e)

LANES = 128
SUBLANES = 8
ONES_ROWS = 16
ATTN_QUERY_TILE = 512
ATTN_KEY_TILE = 256
FAR_UNROLL = 4
FFN_CHUNKS = 2
INPROJ_ROWS = 256
VMEM_LIMIT_BYTES = 60 * 1024 * 1024

F32 = jnp.float32
BF16 = jnp.bfloat16


def _params(*sem, flags=None):
    return pltpu.CompilerParams(dimension_semantics=sem, vmem_limit_bytes=VMEM_LIMIT_BYTES, flags=flags)


def _rms_rows(x, g):
    return x * lax.rsqrt(jnp.mean(x * x, axis=-1, keepdims=True) + EPS) * g


def _dot(a, b):
    return jnp.dot(a, b, preferred_element_type=F32)


def _dot_nt(a, b):
    return lax.dot_general(a, b, (((1,), (1,)), ((), ())), preferred_element_type=F32)


def _split3(x):
    hi = x.astype(BF16)
    r1 = x - hi.astype(F32)
    mid = r1.astype(BF16)
    lo = (r1 - mid.astype(F32)).astype(BF16)
    return hi, mid, lo


def _split3_on_axis(x, index):
    hi, mid, lo = (v.astype(F32) for v in _split3(x))
    return jnp.where(index == 0, hi, jnp.where(index == 1, mid, jnp.where(index == 2, lo, 0.0))).astype(BF16)


def _transpose_bf16(x):
    return x.astype(F32).T.astype(BF16)


def _inproj_kernel(x_ref, g_ref, w_ref, wf_ref, bf_ref,
                   zb_ref, kd_ref, vd_ref, kf_ref, vf_ref, logf_ref, flp_ref, *, n_heads, q_scales):
    dw = kd_ref.shape[1]
    xn = _rms_rows(x_ref[...], g_ref[...]).astype(BF16)
    fl = _dot(xn, wf_ref[...]) + bf_ref[...]
    lf = jnp.minimum(fl, 0.0) - jnp.log1p(jnp.exp(-jnp.abs(fl)))
    flp_ref[...] = lf
    logf_ref[...] = lf[:, :n_heads]
    f32_outs = {1: kd_ref, 2: vd_ref, 4: kf_ref, 5: vf_ref}
    for seg in range(6):
        cols = slice(seg * dw, (seg + 1) * dw)
        z = _dot(xn, w_ref[:, cols])
        if seg in f32_outs:
            f32_outs[seg][...] = z
        else:
            z = z * q_scales[seg]
        zb_ref[:, cols] = z.astype(BF16)


def _inproj(x, g, w_qkv, w_f, b_f, n_heads, tm):
    m, d = x.shape
    dw = w_qkv.shape[1] // 6
    row = lambda i: (i, 0)
    const2 = lambda i: (0, 0)
    f32_out = jax.ShapeDtypeStruct((m, dw), F32)
    q_scales = {0: DIFF_QK_DIM ** -0.5 * LOG2E, 3: HEAD_DIM ** -0.5 * LOG2E}
    return pl.pallas_call(
        functools.partial(_inproj_kernel, n_heads=n_heads, q_scales=q_scales),
        grid=(m // tm,),
        in_specs=[
            pl.BlockSpec((tm, d), row),
            pl.BlockSpec((1, d), const2),
            pl.BlockSpec((d, 6 * dw), const2, pipeline_mode=pl.Buffered(1)),
            pl.BlockSpec((d, LANES), const2),
            pl.BlockSpec((1, LANES), const2),
        ],
        out_specs=[
            pl.BlockSpec((tm, 6 * dw), row),
            pl.BlockSpec((tm, dw), row),
            pl.BlockSpec((tm, dw), row),
            pl.BlockSpec((tm, dw), row),
            pl.BlockSpec((tm, dw), row),
            pl.BlockSpec((tm, n_heads), row),
            pl.BlockSpec((tm, LANES), row),
        ],
        out_shape=[
            jax.ShapeDtypeStruct((m, 6 * dw), BF16),
            f32_out, f32_out, f32_out, f32_out,
            jax.ShapeDtypeStruct((m, n_heads), F32),
            jax.ShapeDtypeStruct((m, LANES), F32),
        ],
        compiler_params=_params("arbitrary"),
        name="inproj",
    )(x, g, w_qkv, w_f, b_f)


def _cumsum_kernel(x_ref, o_ref, *, chunk, n_heads, as_key_columns):
    t = x_ref.shape[0]
    r = lax.broadcasted_iota(jnp.int32, (chunk, chunk), 0)
    c = lax.broadcasted_iota(jnp.int32, (chunk, chunk), 1)
    tri = (r >= c).astype(BF16)
    lane = lax.broadcasted_iota(jnp.int32, (chunk, LANES), 1)
    carry = jnp.zeros((1, LANES), F32)
    for k in range(t // chunk):
        hi, mid, lo = _split3(x_ref[k * chunk:(k + 1) * chunk, :])
        f = _dot(tri, hi) + _dot(tri, mid) + _dot(tri, lo) + carry
        carry = f[chunk - 1:chunk, :]
        if as_key_columns:
            for h in range(n_heads):
                o_ref[h, k * chunk:(k + 1) * chunk, :] = _split3_on_axis(
                    jnp.broadcast_to(f[:, h:h + 1] * LOG2E, (chunk, LANES)), lane)
        else:
            o_ref[:, k * chunk:(k + 1) * chunk] = f.T[:n_heads, :]


def _cumsum(flp, n_heads, chunk, as_key_columns):
    nb, t, _ = flp.shape
    if as_key_columns:
        out_spec = pl.BlockSpec((None, n_heads, t, LANES), lambda b: (b, 0, 0, 0))
        out_shape = jax.ShapeDtypeStruct((nb, n_heads, t, LANES), BF16)
    else:
        out_spec = pl.BlockSpec((None, n_heads, t), lambda b: (b, 0, 0))
        out_shape = jax.ShapeDtypeStruct((nb, n_heads, t), F32)
    return pl.pallas_call(
        functools.partial(_cumsum_kernel, chunk=chunk, n_heads=n_heads, as_key_columns=as_key_columns),
        grid=(nb,),
        in_specs=[pl.BlockSpec((None, t, LANES), lambda b: (b, 0, 0))],
        out_specs=out_spec,
        out_shape=out_shape,
        compiler_params=_params("arbitrary"),
        name="cumsum_keys" if as_key_columns else "cumsum_rows",
    )(flp)


def _t5_bucket(rel):
    nb = REL_BUCKETS // 2
    max_exact = nb // 2
    n = jnp.abs(rel)
    nf = jnp.maximum(n, 1).astype(jnp.float32)
    large = max_exact + (jnp.log(nf / max_exact) / math.log(REL_MAX_DIST / max_exact)
                         * (nb - max_exact)).astype(jnp.int32)
    large = jnp.minimum(large, nb - 1)
    return jnp.where(rel > 0, nb, 0) + jnp.where(n < max_exact, n, large)


def _bias_kernel(table_ref, idx_ref, mask_ref, o_ref, *, minus_bucket):
    h = pl.program_id(0)
    idx = idx_ref[...]
    acc = mask_ref[...]
    if minus_bucket is not None:
        acc = acc - table_ref[minus_bucket, h]
    for b in range(REL_BUCKETS):
        acc = acc + jnp.where(idx == b, table_ref[b, h], 0.0)
    o_ref[...] = acc * LOG2E


def _bias_tiles(rel_table, qpos, kpos, keys_major=False, minus_bucket=None):
    n_heads = rel_table.shape[1]
    if keys_major:
        qp, kp = qpos[:, None, :], kpos[:, :, None]
    else:
        qp, kp = qpos[:, :, None], kpos[:, None, :]
    idx = _t5_bucket(kp - qp).astype(jnp.int32)
    mask = jnp.where((kp // CHUNK) <= (qp // CHUNK), 0.0, NEG_INF).astype(F32)
    nt, r, c = idx.shape
    return pl.pallas_call(
        functools.partial(_bias_kernel, minus_bucket=minus_bucket),
        grid=(n_heads, nt),
        in_specs=[
            pl.BlockSpec(memory_space=pltpu.SMEM),
            pl.BlockSpec((None, r, c), lambda h, t: (t, 0, 0)),
            pl.BlockSpec((None, r, c), lambda h, t: (t, 0, 0)),
        ],
        out_specs=pl.BlockSpec((None, None, r, c), lambda h, t: (h, t, 0, 0)),
        out_shape=jax.ShapeDtypeStruct((n_heads, nt, r, c), F32),
        compiler_params=_params("arbitrary", "arbitrary"),
        name="bias_tiles",
    )(rel_table.astype(F32), idx, mask)


def _far_bucket(min_dist):
    nb = REL_BUCKETS // 2
    max_exact = nb // 2
    large = max_exact + math.log(min_dist / max_exact) / math.log(REL_MAX_DIST / max_exact) * (nb - max_exact)
    assert large >= nb - 1 + 0.5, "key tile too short for a constant far-field bias"
    return nb - 1


def _lambda_value(lam_ref, lam_init):
    a = lam_ref[...]
    s1 = jnp.sum(a[0:1] * a[1:2], axis=-1, keepdims=True)
    s2 = jnp.sum(a[2:3] * a[3:4], axis=-1, keepdims=True)
    return jnp.exp(s1) - jnp.exp(s2) + lam_init


def _prompt_attn_kernel(table_ref, lam_ref, subln_ref, qd_ref, kd_ref, vd_ref, qf_ref, kf_ref, vf_ref,
                        fcol_ref, bias_ref, cmask_ref, od_ref, of_ref,
                        kda_ref, kfa_ref, vdt_ref, vft_ref,
                        md_ref, accd_ref, mf_ref, accf_ref,
                        sd_ref, pd_ref, ad_ref, sf_ref, pf_ref, af_ref,
                        *, tq, tk, lam_init, far_bucket):
    h = pl.program_id(1)
    qi = pl.program_id(2)
    seq = kd_ref.shape[0]
    ratio = tq // tk

    @pl.when(qi == 0)
    def _():
        lane = lax.broadcasted_iota(jnp.int32, (seq, LANES), 1)
        kda_ref[:, :HEAD_DIM] = kd_ref[...]
        kda_ref[:, HEAD_DIM:] = jnp.where(lane < 3, 1.0, 0.0).astype(BF16)
        kfa_ref[:, :HEAD_DIM] = kf_ref[...]
        kfa_ref[:, HEAD_DIM:] = fcol_ref[...]

        ones = jnp.ones((ONES_ROWS, tk), BF16)

        def transpose_values(c, carry):
            r = pl.ds(pl.multiple_of(c * tk, tk), tk)
            vdt_ref[c] = jnp.concatenate([_transpose_bf16(vd_ref[r, :]), ones], axis=0)
            vft_ref[c] = jnp.concatenate([_transpose_bf16(vf_ref[r, :]), ones], axis=0)
            return carry

        lax.fori_loop(0, seq // tk, transpose_values, 0)

    def key_rows(ref, j):
        return ref[pl.ds(pl.multiple_of(j * tk, tk), tk), :]

    row = lax.broadcasted_iota(jnp.int32, (HEAD_DIM, tq), 0)
    qdt = qd_ref[...].astype(F32).T
    zero = jnp.zeros_like(qdt)
    top = jnp.concatenate([jnp.where(row < DIFF_QK_DIM, qdt, zero),
                           jnp.where(row >= DIFF_QK_DIM, qdt, zero)], axis=1).astype(BF16)
    row2 = lax.broadcasted_iota(jnp.int32, (HEAD_DIM, 2 * tq), 0)
    bottom = _split3_on_axis(jnp.full((HEAD_DIM, 2 * tq), table_ref[far_bucket, h] * LOG2E, F32), row2)
    qd_aug = jnp.concatenate([top, bottom], axis=0)
    qf_aug = jnp.concatenate([qf_ref[...].astype(F32).T.astype(BF16),
                              jnp.where(row < 3, -1.0, 0.0).astype(BF16)], axis=0)

    branches = ((kda_ref, qd_aug, vdt_ref, sd_ref, pd_ref, ad_ref, md_ref, accd_ref),
                (kfa_ref, qf_aug, vft_ref, sf_ref, pf_ref, af_ref, mf_ref, accf_ref))

    def pipeline_step(j, adds, score_next):
        for (k_ref, q_aug, vt_ref, s_ref, p_ref, a_ref, m_ref, acc_ref), add in zip(branches, adds):
            acc_ref[...] = a_ref[...] * acc_ref[...] + _dot(vt_ref[jnp.maximum(j - 1, 0)], p_ref[...])
            s = s_ref[...]
            if add is not None:
                s = s + add
            m_prev = m_ref[...]
            m_new = jnp.maximum(m_prev, jnp.max(s, axis=0, keepdims=True))
            p_ref[...] = jnp.exp2(s - m_new).astype(BF16)
            a_ref[...] = jnp.exp2(m_prev - m_new)
            m_ref[...] = m_new
            if score_next:
                s_ref[...] = _dot(key_rows(k_ref, j + 1), q_aug)

    for k_ref, q_aug, vt_ref, s_ref, p_ref, a_ref, m_ref, acc_ref in branches:
        m_ref[...] = jnp.full_like(m_ref, NEG_INF)
        acc_ref[...] = jnp.zeros_like(acc_ref)
        s_ref[...] = _dot(key_rows(k_ref, 0), q_aug)
        p_ref[...] = jnp.zeros_like(p_ref)
        a_ref[...] = jnp.ones_like(a_ref)

    def far_tiles(unroll):
        def body(i, carry):
            for u in range(unroll):
                pipeline_step(i * unroll + u, (None, None), True)
            return carry
        return body

    first_diag = qi * ratio
    n_far = jnp.maximum(first_diag - 1, 0)
    n_main = n_far // FAR_UNROLL
    lax.fori_loop(0, n_main, far_tiles(FAR_UNROLL), 0)
    lax.fori_loop(n_main * FAR_UNROLL, n_far, far_tiles(1), 0)

    def doubled(bias):
        return jnp.concatenate([bias, bias], axis=1)

    @pl.when(qi >= 1)
    def _():
        pipeline_step(first_diag - 1, (doubled(bias_ref[0]), None), True)

    for u in range(ratio):
        pipeline_step(first_diag + u, (doubled(bias_ref[u + 1]), cmask_ref[u]), u + 1 < ratio)
    last = first_diag + ratio - 1
    for k_ref, q_aug, vt_ref, s_ref, p_ref, a_ref, m_ref, acc_ref in branches:
        acc_ref[...] = a_ref[...] * acc_ref[...] + _dot(vt_ref[last], p_ref[...])

    lam = _lambda_value(lam_ref, lam_init)
    acc = accd_ref[...]
    num, den = acc[:HEAD_DIM], acc[HEAD_DIM:HEAD_DIM + 1]
    odt = num[:, :tq] / den[:, :tq] - lam * (num[:, tq:] / den[:, tq:])
    od_ref[...] = (_rms_rows(odt.T, subln_ref[...]) * (1.0 - lam_init)).astype(BF16)
    acc = accf_ref[...]
    of_ref[...] = (acc[:HEAD_DIM] / acc[HEAD_DIM:HEAD_DIM + 1]).T.astype(BF16)


def _prompt_attention(zb, fcols, rel_table, lam_params, subln, lam_init, batch, seq, n_heads, tq, tk):
    m = batch * seq
    dw = n_heads * HEAD_DIM
    nq = seq // tq
    ratio = tq // tk
    assert tq == ratio * tk
    far_bucket = _far_bucket(tk + 1)
    vrows = HEAD_DIM + ONES_ROWS
    qpos = tk + jnp.arange(tq, dtype=jnp.int32)
    kpos = jnp.arange((ratio + 1) * tk, dtype=jnp.int32).reshape(ratio + 1, tk)
    bias = _bias_tiles(rel_table, jnp.broadcast_to(qpos, (ratio + 1, tq)), kpos,
                       keys_major=True, minus_bucket=far_bucket)
    cmask = jnp.where(kpos[1:, :, None] <= qpos[None, None, :], 0.0, NEG_INF).astype(F32)

    def q_spec(seg):
        return pl.BlockSpec((tq, HEAD_DIM), lambda b, h, q, seg=seg: (b * nq + q, seg * n_heads + h))

    def kv_spec(seg):
        return pl.BlockSpec((seq, HEAD_DIM), lambda b, h, q, seg=seg: (b, seg * n_heads + h))

    out_spec = pl.BlockSpec((tq, HEAD_DIM), lambda b, h, q: (b * nq + q, h))
    const2 = lambda b, h, q: (0, 0)
    kern = functools.partial(_prompt_attn_kernel, tq=tq, tk=tk, lam_init=lam_init, far_bucket=far_bucket)
    return pl.pallas_call(
        kern,
        grid=(batch, n_heads, nq),
        in_specs=[
            pl.BlockSpec(memory_space=pltpu.SMEM),
            pl.BlockSpec((4, DIFF_QK_DIM), const2),
            pl.BlockSpec((1, HEAD_DIM), const2),
            q_spec(0), kv_spec(1), kv_spec(2), q_spec(3), kv_spec(4), kv_spec(5),
            pl.BlockSpec((None, None, seq, LANES), lambda b, h, q: (b, h, 0, 0)),
            pl.BlockSpec((None, ratio + 1, tk, tq), lambda b, h, q: (h, 0, 0, 0)),
            pl.BlockSpec((ratio, tk, tq), lambda b, h, q: (0, 0, 0)),
        ],
        out_specs=[out_spec, out_spec],
        out_shape=[jax.ShapeDtypeStruct((m, dw), BF16), jax.ShapeDtypeStruct((m, dw), BF16)],
        scratch_shapes=[
            pltpu.VMEM((seq, 2 * HEAD_DIM), BF16), pltpu.VMEM((seq, 2 * HEAD_DIM), BF16),
            pltpu.VMEM((seq // tk, vrows, tk), BF16), pltpu.VMEM((seq // tk, vrows, tk), BF16),
            pltpu.VMEM((1, 2 * tq), F32), pltpu.VMEM((vrows, 2 * tq), F32),
            pltpu.VMEM((1, tq), F32), pltpu.VMEM((vrows, tq), F32),
            pltpu.VMEM((tk, 2 * tq), F32), pltpu.VMEM((tk, 2 * tq), BF16), pltpu.VMEM((1, 2 * tq), F32),
            pltpu.VMEM((tk, tq), F32), pltpu.VMEM((tk, tq), BF16), pltpu.VMEM((1, tq), F32),
        ],
        compiler_params=_params("arbitrary", "arbitrary", "arbitrary"),
        name="prompt_attention",
    )(rel_table.astype(F32), lam_params, subln, zb, zb, zb, zb, zb, zb, fcols, bias, cmask)


def _split_diff_queries(q):
    lane = lax.broadcasted_iota(jnp.int32, q.shape, 1)
    zero = jnp.zeros_like(q)
    return jnp.concatenate([jnp.where(lane < DIFF_QK_DIM, q, zero),
                            jnp.where(lane >= DIFF_QK_DIM, q, zero)], axis=0)


def _sample_attn_kernel(lam_ref, subln_ref, z_ref, kdc_ref, vdc_ref, kfc_ref, vfc_ref, frow_ref, bias_ref,
                        cmask_ref, od_ref, of_ref, *, past, t_new, n_heads, lam_init):
    dw = n_heads * HEAD_DIM

    def cached(ref, h):
        return ref[pl.ds(h, past, stride=n_heads), :].astype(BF16)

    def new_rows(seg, h):
        lo = seg * dw + h * HEAD_DIM
        return z_ref[:, lo:lo + HEAD_DIM]

    def attend(q, kc, vc, kn, vn, add_c, add_n):
        sc = _dot_nt(q, kc) + add_c
        sn = _dot_nt(q, kn) + add_n
        mx = jnp.maximum(jnp.max(sc, axis=-1, keepdims=True), jnp.max(sn, axis=-1, keepdims=True))
        pc = jnp.exp2(sc - mx)
        pn = jnp.exp2(sn - mx)
        l = jnp.sum(pc, axis=-1, keepdims=True) + jnp.sum(pn, axis=-1, keepdims=True)
        acc = _dot(pc.astype(BF16), vc) + _dot(pn.astype(BF16), vn)
        return acc, l

    lam = _lambda_value(lam_ref, lam_init)
    cmask = cmask_ref[...]
    for h in range(n_heads):
        cols = slice(h * HEAD_DIM, (h + 1) * HEAD_DIM)
        qs = _split_diff_queries(new_rows(0, h))
        bias_c = bias_ref[h, :, :past]
        bias_n = bias_ref[h, :, past:past + t_new]
        acc, l = attend(qs, cached(kdc_ref, h), cached(vdc_ref, h),
                        new_rows(1, h), new_rows(2, h),
                        jnp.concatenate([bias_c, bias_c], axis=0), jnp.concatenate([bias_n, bias_n], axis=0))
        od = acc[:t_new] / l[:t_new] - lam * (acc[t_new:] / l[t_new:])
        od_ref[:, cols] = (_rms_rows(od, subln_ref[...]) * (1.0 - lam_init)).astype(BF16)

        f0 = frow_ref[h:h + 1, past:past + 1]
        dec_c = (f0 - frow_ref[h:h + 1, :past]) * LOG2E
        dec_n = (f0 - frow_ref[h:h + 1, past:past + t_new]) * LOG2E
        acc, l = attend(new_rows(3, h), cached(kfc_ref, h), cached(vfc_ref, h),
                        new_rows(4, h), new_rows(5, h), dec_c, dec_n + cmask)
        of_ref[:, cols] = (acc / l).astype(BF16)


def _sample_attention(zb, caches, layer, frow, rel_table, lam_params, subln, lam_init, nb, t_new, past, n_heads):
    dw = n_heads * HEAD_DIM
    tpad = frow.shape[-1]
    ar = jnp.arange(t_new, dtype=jnp.int32)
    bias = _bias_tiles(rel_table, (past + ar)[None], jnp.arange(tpad, dtype=jnp.int32)[None])[:, 0]
    cmask = jnp.where(ar[None, :] <= ar[:, None], 0.0, NEG_INF).astype(F32)
    caches = [c.reshape(c.shape[0], nb, past * n_heads, HEAD_DIM) for c in caches]
    cache_spec = pl.BlockSpec((None, None, past * n_heads, HEAD_DIM), lambda b: (layer, b, 0, 0))
    out_spec = pl.BlockSpec((t_new, dw), lambda b: (b, 0))
    const2 = lambda b: (0, 0)
    kern = functools.partial(_sample_attn_kernel, past=past, t_new=t_new, n_heads=n_heads, lam_init=lam_init)
    return pl.pallas_call(
        kern,
        grid=(nb,),
        in_specs=[
            pl.BlockSpec((4, DIFF_QK_DIM), const2),
            pl.BlockSpec((1, HEAD_DIM), const2),
            pl.BlockSpec((t_new, 6 * dw), lambda b: (b, 0)),
            cache_spec, cache_spec, cache_spec, cache_spec,
            pl.BlockSpec((None, n_heads, tpad), lambda b: (b, 0, 0)),
            pl.BlockSpec((n_heads, t_new, tpad), lambda b: (0, 0, 0)),
            pl.BlockSpec((t_new, t_new), const2),
        ],
        out_specs=[out_spec, out_spec],
        out_shape=[jax.ShapeDtypeStruct((nb * t_new, dw), BF16), jax.ShapeDtypeStruct((nb * t_new, dw), BF16)],
        compiler_params=_params("arbitrary"),
        name="sample_attention",
    )(lam_params, subln, zb, *caches, frow, bias, cmask)


def _merge_kernel(x_ref, g1_ref, od_ref, of_ref, wbd_ref, wbf_ref, wga_ref, wgb_ref, wout_ref, gp_ref,
                  o_ref, xn_ref, u_ref):
    n = pl.program_id(1)

    @pl.when(n == 0)
    def _():
        xn_ref[...] = _rms_rows(x_ref[...], g1_ref[...]).astype(BF16)

    xn = xn_ref[...]
    yd = _dot(od_ref[...], wbd_ref[...])
    yf = _dot(of_ref[...], wbf_ref[...])
    ga = jax.nn.sigmoid(_dot(xn, wga_ref[...]))
    gb = jax.nn.sigmoid(_dot(xn, wgb_ref[...]))
    u_ref[n] = (ga * yd + gb * yf).astype(BF16)

    @pl.when(n == pl.num_programs(1) - 1)
    def _():
        u = jnp.concatenate([u_ref[k] for k in range(u_ref.shape[0])], axis=1)
        o = _dot(u, wout_ref[...])
        o_ref[...] = x_ref[...] + _rms_rows(o, gp_ref[...])


def _merge(x, g1, od, of, w_bd, w_bf, w_ga, w_gb, w_out, g_post, tm, tn):
    m, d = x.shape
    dw = od.shape[1]
    nn = d // tn
    row = lambda i, n: (i, 0)
    col = lambda i, n: (0, n)
    const2 = lambda i, n: (0, 0)
    return pl.pallas_call(
        _merge_kernel,
        grid=(m // tm, nn),
        in_specs=[
            pl.BlockSpec((tm, d), row),
            pl.BlockSpec((1, d), const2),
            pl.BlockSpec((tm, dw), row),
            pl.BlockSpec((tm, dw), row),
            pl.BlockSpec((dw, tn), col),
            pl.BlockSpec((dw, tn), col),
            pl.BlockSpec((d, tn), col),
            pl.BlockSpec((d, tn), col),
            pl.BlockSpec((d, d), const2, pipeline_mode=pl.Buffered(1)),
            pl.BlockSpec((1, d), const2),
        ],
        out_specs=pl.BlockSpec((tm, d), row),
        out_shape=jax.ShapeDtypeStruct((m, d), F32),
        scratch_shapes=[pltpu.VMEM((tm, d), BF16), pltpu.VMEM((nn, tm, tn), BF16)],
        compiler_params=_params("arbitrary", "arbitrary"),
        name="merge",
    )(x, g1, od, of, w_bd, w_bf, w_ga, w_gb, w_out, g_post)


def _gelu_tanh(x):
    k = -2.0 * math.sqrt(2.0 / math.pi) * LOG2E
    return x / (1.0 + jnp.exp2(x * (k + (k * 0.044715) * (x * x))))


def _ffn_kernel(*refs, seq, has_edges):
    if has_edges:
        (h_ref, g2_ref, wa_ref, wb_ref, cw_ref, cb_ref, wd_ref, gp_ref, e0_ref, e1_ref,
         o_ref, cs_ref, xn_ref, tail_ref) = refs
    else:
        (h_ref, g2_ref, wa_ref, wb_ref, cw_ref, cb_ref, wd_ref, gp_ref,
         o_ref, cs_ref, xn_ref, tail_ref) = refs
    i = pl.program_id(0)
    f = pl.program_id(1)
    tm = h_ref.shape[0]

    tf = wa_ref.shape[1]

    @pl.when(f == 0)
    def _():
        xn_ref[...] = _rms_rows(h_ref[...], g2_ref[...]).astype(BF16)
        o_ref[...] = jnp.zeros_like(o_ref)

    if not has_edges:
        @pl.when((i * tm) % seq == 0)
        def _():
            tail_ref[f] = jnp.zeros(tail_ref.shape[1:], F32)

    xn = xn_ref[...]

    def gated_chunk(cols):
        width = cols.stop - cols.start
        a = _dot(xn, wa_ref[:, cols])
        gate = _dot(xn, wb_ref[:, cols])
        back1 = pltpu.roll(a, 1, 0)
        back2 = pltpu.roll(a, 2, 0)
        if has_edges:
            t = lax.broadcasted_iota(jnp.int32, a.shape, 0) % seq
            am1 = jnp.where(t >= 1, back1, 0.0) + e1_ref[:, cols]
            am2 = jnp.where(t >= 2, back2, 0.0) + e0_ref[:, cols]
            cs_ref[:, :, cols] = a.reshape(tm // seq, seq, width)[:, seq - (CONV_WIDTH - 1):, :]
        else:
            prev = tail_ref[f, :, cols]
            p0 = prev[SUBLANES - 2:SUBLANES - 1]
            p1 = prev[SUBLANES - 1:SUBLANES]
            top = lax.broadcasted_iota(jnp.int32, (SUBLANES, width), 0)
            am1 = jnp.concatenate([jnp.where(top == 0, p1, back1[:SUBLANES]), back1[SUBLANES:]], axis=0)
            am2 = jnp.concatenate([jnp.where(top == 0, p0, jnp.where(top == 1, p1, back2[:SUBLANES])),
                                   back2[SUBLANES:]], axis=0)
            tail_ref[f, :, cols] = a[tm - SUBLANES:, :]
            cs_ref[:, :, cols] = a[tm - (CONV_WIDTH - 1):, :][None]
        cw = cw_ref[:, cols]
        ac = cw[0:1] * am2 + cw[1:2] * am1 + cw[2:3] * a + cb_ref[:, cols]
        return (_gelu_tanh(ac) * gate).astype(BF16)

    width = tf // FFN_CHUNKS
    contrib = None
    for c in range(FFN_CHUNKS):
        cols = slice(c * width, (c + 1) * width)
        part = _dot(gated_chunk(cols), wd_ref[cols, :])
        contrib = part if contrib is None else contrib + part
    o_ref[...] += contrib

    @pl.when(f == pl.num_programs(1) - 1)
    def _():
        o_ref[...] = h_ref[...] + _rms_rows(o_ref[...], gp_ref[...])


def _ffn(h, g2, w_a, w_b, conv_w, conv_b, w_d, g_post, edges, seq, tm, tf):
    m, d = h.shape
    dff = w_a.shape[1]
    nf = dff // tf
    has_edges = edges is not None
    row = lambda i, f: (i, 0)
    col = lambda i, f: (0, f)
    const2 = lambda i, f: (0, 0)
    in_specs = [
        pl.BlockSpec((tm, d), row),
        pl.BlockSpec((1, d), const2),
        pl.BlockSpec((d, tf), col),
        pl.BlockSpec((d, tf), col),
        pl.BlockSpec((CONV_WIDTH, tf), col),
        pl.BlockSpec((1, tf), col),
        pl.BlockSpec((tf, d), lambda i, f: (f, 0)),
        pl.BlockSpec((1, d), const2),
    ]
    args = [h, g2, w_a, w_b, conv_w, conv_b, w_d, g_post]
    if has_edges:
        assert tm % seq == 0
        in_specs += [pl.BlockSpec((tm, tf), lambda i, f: (i, f))] * 2
        args += list(edges)
        tails_per_tile = tm // seq
    else:
        assert seq % tm == 0
        tails_per_tile = 1
    n_tails = (m // tm) * tails_per_tile
    out, tails = pl.pallas_call(
        functools.partial(_ffn_kernel, seq=seq, has_edges=has_edges),
        grid=(m // tm, nf),
        in_specs=in_specs,
        out_specs=[pl.BlockSpec((tm, d), row),
                   pl.BlockSpec((tails_per_tile, CONV_WIDTH - 1, tf), lambda i, f: (i, 0, f))],
        out_shape=[jax.ShapeDtypeStruct((m, d), F32),
                   jax.ShapeDtypeStruct((n_tails, CONV_WIDTH - 1, dff), F32)],
        scratch_shapes=[pltpu.VMEM((tm, d), BF16), pltpu.VMEM((nf, SUBLANES, tf), F32)],
        compiler_params=_params("arbitrary", "arbitrary"),
        name="conv_ffn",
    )(*args)
    if not has_edges:
        tiles_per_seq = seq // tm
        tails = tails[tiles_per_seq - 1::tiles_per_seq]
    return out, tails


def _tile(n, cap):
    t = min(n, cap)
    assert n % t == 0
    return t


def _layer(layer, hp, hs, past, rel_table, lam_params, lam_init, p):
    (pre1, w_in, b_forget, subln, w_bd, w_bf, w_out, post1, pre2, w_up, conv_w, conv_b, w_down, post2) = p
    batch, seq, d = hp.shape
    nb, t_new, _ = hs.shape
    n_heads = d // (2 * HEAD_DIM)
    dw = n_heads * HEAD_DIM
    dff = w_down.shape[0]
    caches, plogf, conv_prev = past
    plen = plogf.shape[1]

    row2 = lambda v: v.reshape(1, -1).astype(F32)
    w_qkv = w_in[:, :6 * dw].astype(BF16)
    w_f = jnp.pad(w_in[:, 6 * dw:6 * dw + n_heads], ((0, 0), (0, LANES - n_heads))).astype(BF16)
    b_f = jnp.pad(b_forget.astype(F32), (0, LANES - n_heads)).reshape(1, LANES)
    w_ga = w_in[:, 6 * dw + n_heads:6 * dw + n_heads + d].astype(BF16)
    w_gb = w_in[:, 6 * dw + n_heads + d:].astype(BF16)
    w_bd_b, w_bf_b, w_out_b = w_bd.astype(BF16), w_bf.astype(BF16), w_out.astype(BF16)
    w_a, w_b = w_up[:, :dff].astype(BF16), w_up[:, dff:].astype(BF16)
    w_d = w_down.astype(BF16)
    subln2 = row2(subln)

    def dense_tail(x2, od, of, edges, t_seq):
        m = x2.shape[0]
        tm = _tile(m, 512)
        h1 = _merge(x2, row2(pre1), od, of, w_bd_b, w_bf_b, w_ga, w_gb, w_out_b, row2(post1),
                    tm, _tile(d, 512))
        return _ffn(h1, row2(pre2), w_a, w_b, conv_w.astype(F32), row2(conv_b), w_d, row2(post2),
                    edges, t_seq, tm, _tile(dff, 1024))

    heads = lambda a, n, t: a.reshape(n, t, n_heads, HEAD_DIM)

    xp = hp.reshape(batch * seq, d)
    zb, kd, vd, kf, vf, logf, flp = _inproj(xp, row2(pre1), w_qkv, w_f, b_f, n_heads,
                                            _tile(batch * seq, INPROJ_ROWS))
    fcols = _cumsum(flp.reshape(batch, seq, LANES), n_heads, _tile(seq, 256), as_key_columns=True)
    od, of = _prompt_attention(zb, fcols, rel_table, lam_params, subln2, lam_init,
                               batch, seq, n_heads, _tile(seq, ATTN_QUERY_TILE), _tile(seq, ATTN_KEY_TILE))
    hp_out, conv_p = dense_tail(xp, od, of, None, seq)
    state_p = (heads(kd, batch, seq), heads(vd, batch, seq), heads(kf, batch, seq), heads(vf, batch, seq),
               logf.reshape(batch, seq, n_heads), conv_p)

    xs = hs.reshape(nb * t_new, d)
    zb, kd, vd, kf, vf, logf, flp = _inproj(xs, row2(pre1), w_qkv, w_f, b_f, n_heads,
                                            _tile(nb * t_new, INPROJ_ROWS))
    tpad = -(-(plen + t_new) // LANES) * LANES
    flog = jnp.concatenate([jnp.pad(plogf.astype(F32), ((0, 0), (0, 0), (0, LANES - n_heads))),
                            flp.reshape(nb, t_new, LANES),
                            jnp.zeros((nb, tpad - plen - t_new, LANES), F32)], axis=1)
    frow = _cumsum(flog, n_heads, LANES, as_key_columns=False)
    od, of = _sample_attention(zb, caches, layer, frow, rel_table, lam_params, subln2, lam_init,
                               nb, t_new, plen, n_heads)
    cp = conv_prev.astype(F32)
    e0 = jnp.zeros((nb, t_new, dff), F32).at[:, 0].set(cp[:, 0]).at[:, 1].set(cp[:, 1])
    e1 = jnp.zeros((nb, t_new, dff), F32).at[:, 0].set(cp[:, 1])
    edges = (e0.reshape(nb * t_new, dff), e1.reshape(nb * t_new, dff))
    hs_out, conv_s = dense_tail(xs, od, of, edges, t_new)
    state_s = (heads(kd, nb, t_new), heads(vd, nb, t_new), heads(kf, nb, t_new), heads(vf, nb, t_new),
               logf.reshape(nb, t_new, n_heads), conv_s)
    return hp_out.reshape(batch, seq, d), hs_out.reshape(nb, t_new, d), state_p, state_s


def kernel(x_prompt, x_sample, cache_diff_k, cache_diff_v, cache_fox_k, cache_fox_v, cache_fox_logf,
           state_ffn_conv, rel_table, pre_norm1, w_in, b_forget, lam_q1, lam_k1, lam_q2, lam_k2,
           diff_subln, w_branch_diff, w_branch_fox, w_out, post_norm1, pre_norm2, w_up, conv_w, conv_b,
           w_down, post_norm2):
    depth = w_in.shape[0]
    hp, hs = x_prompt, x_sample
    caches = (cache_diff_k, cache_diff_v, cache_fox_k, cache_fox_v)
    new_p, new_s = [], []
    for l in range(depth):
        lam_init = 0.8 - 0.6 * math.exp(-0.3 * l)
        lam_params = jnp.stack([lam_q1[l], lam_k1[l], lam_q2[l], lam_k2[l]]).astype(F32)
        params = (pre_norm1[l], w_in[l], b_forget[l], diff_subln[l], w_branch_diff[l], w_branch_fox[l],
                  w_out[l], post_norm1[l], pre_norm2[l], w_up[l], conv_w[l], conv_b[l], w_down[l],
                  post_norm2[l])
        past = (caches, cache_fox_logf[l], state_ffn_conv[l])
        hp, hs, sp, ss = _layer(l, hp, hs, past, rel_table, lam_params, lam_init, params)
        new_p.append(sp)
        new_s.append(ss)
    st = lambda lst, i: jnp.stack([e[i] for e in lst])
    return (hp, hs,
            st(new_p, 0), st(new_p, 1), st(new_p, 2), st(new_p, 3), st(new_p, 4), st(new_p, 5),
            st(new_s, 0), st(new_s, 1), st(new_s, 2), st(new_s, 3), st(new_s, 4), st(new_s, 5))
```

```python
import functools
import math

import jax
import jax.numpy as jnp
from jax import lax
from jax.experimental import pallas as pl
from jax.experimental.pallas import tpu as pltpu

HEAD_DIM = 128
DIFF_QK_DIM = HEAD_DIM // 2
CHUNK = 64
CONV_WIDTH = 3
REL_BUCKETS = 32
REL_MAX_DIST = 128
EPS = 1e-6
NEG_INF = -1e30
LOG2E = math.log2(math.e)

LANES = 128
SUBLANES = 8
ONES_ROWS = 16
ATTN_QUERY_TILE = 512
ATTN_KEY_TILE = 512
FAR_UNROLL = 2
FFN_CHUNKS = 2
MERGE_ROWS = 256
INPROJ_ROWS = 256
VMEM_LIMIT_BYTES = 60 * 1024 * 1024

F32 = jnp.float32
BF16 = jnp.bfloat16


def _params(*sem, flags=None):
    return pltpu.CompilerParams(dimension_semantics=sem, vmem_limit_bytes=VMEM_LIMIT_BYTES, flags=flags)


def _rms_rows(x, g):
    return x * lax.rsqrt(jnp.mean(x * x, axis=-1, keepdims=True) + EPS) * g


def _dot(a, b):
    return jnp.dot(a, b, preferred_element_type=F32)


def _dot_nt(a, b):
    return lax.dot_general(a, b, (((1,), (1,)), ((), ())), preferred_element_type=F32)


def _split3(x):
    hi = x.astype(BF16)
    r1 = x - hi.astype(F32)
    mid = r1.astype(BF16)
    lo = (r1 - mid.astype(F32)).astype(BF16)
    return hi, mid, lo


def _split3_on_axis(x, index):
    hi, mid, lo = (v.astype(F32) for v in _split3(x))
    return jnp.where(index == 0, hi, jnp.where(index == 1, mid, jnp.where(index == 2, lo, 0.0))).astype(BF16)


def _transpose_bf16(x):
    return x.astype(F32).T.astype(BF16)


def _inproj_kernel(x_ref, g_ref, w_ref, wf_ref, bf_ref,
                   zb_ref, kd_ref, vd_ref, kf_ref, vf_ref, logf_ref, flp_ref, *, n_heads, q_scales):
    dw = kd_ref.shape[1]
    xn = _rms_rows(x_ref[...], g_ref[...]).astype(BF16)
    fl = _dot(xn, wf_ref[...]) + bf_ref[...]
    lf = jnp.minimum(fl, 0.0) - jnp.log1p(jnp.exp(-jnp.abs(fl)))
    flp_ref[...] = lf
    logf_ref[...] = lf[:, :n_heads]
    f32_outs = {1: kd_ref, 2: vd_ref, 4: kf_ref, 5: vf_ref}
    for seg in range(6):
        cols = slice(seg * dw, (seg + 1) * dw)
        z = _dot(xn, w_ref[:, cols])
        if seg in f32_outs:
            f32_outs[seg][...] = z
        else:
            z = z * q_scales[seg]
        zb_ref[:, cols] = z.astype(BF16)


def _inproj(x, g, w_qkv, w_f, b_f, n_heads, tm):
    m, d = x.shape
    dw = w_qkv.shape[1] // 6
    row = lambda i: (i, 0)
    const2 = lambda i: (0, 0)
    f32_out = jax.ShapeDtypeStruct((m, dw), F32)
    q_scales = {0: DIFF_QK_DIM ** -0.5 * LOG2E, 3: HEAD_DIM ** -0.5 * LOG2E}
    return pl.pallas_call(
        functools.partial(_inproj_kernel, n_heads=n_heads, q_scales=q_scales),
        grid=(m // tm,),
        in_specs=[
            pl.BlockSpec((tm, d), row),
            pl.BlockSpec((1, d), const2),
            pl.BlockSpec((d, 6 * dw), const2, pipeline_mode=pl.Buffered(1)),
            pl.BlockSpec((d, LANES), const2),
            pl.BlockSpec((1, LANES), const2),
        ],
        out_specs=[
            pl.BlockSpec((tm, 6 * dw), row),
            pl.BlockSpec((tm, dw), row),
            pl.BlockSpec((tm, dw), row),
            pl.BlockSpec((tm, dw), row),
            pl.BlockSpec((tm, dw), row),
            pl.BlockSpec((tm, n_heads), row),
            pl.BlockSpec((tm, LANES), row),
        ],
        out_shape=[
            jax.ShapeDtypeStruct((m, 6 * dw), BF16),
            f32_out, f32_out, f32_out, f32_out,
            jax.ShapeDtypeStruct((m, n_heads), F32),
            jax.ShapeDtypeStruct((m, LANES), F32),
        ],
        compiler_params=_params("arbitrary"),
        name="inproj",
    )(x, g, w_qkv, w_f, b_f)


def _cumsum_kernel(x_ref, o_ref, *, chunk, n_heads, as_key_columns):
    t = x_ref.shape[0]
    r = lax.broadcasted_iota(jnp.int32, (chunk, chunk), 0)
    c = lax.broadcasted_iota(jnp.int32, (chunk, chunk), 1)
    tri = (r >= c).astype(BF16)
    lane = lax.broadcasted_iota(jnp.int32, (chunk, LANES), 1)
    carry = jnp.zeros((1, LANES), F32)
    for k in range(t // chunk):
        hi, mid, lo = _split3(x_ref[k * chunk:(k + 1) * chunk, :])
        local = _dot(tri, hi) + _dot(tri, mid) + _dot(tri, lo)
        f = local + carry
        carry = carry + local[chunk - 1:chunk, :]
        if as_key_columns:
            for h in range(n_heads):
                o_ref[h, k * chunk:(k + 1) * chunk, :] = _split3_on_axis(
                    jnp.broadcast_to(f[:, h:h + 1] * LOG2E, (chunk, LANES)), lane)
        else:
            o_ref[:, k * chunk:(k + 1) * chunk] = f.T[:n_heads, :]


def _cumsum(flp, n_heads, chunk, as_key_columns):
    nb, t, _ = flp.shape
    if as_key_columns:
        out_spec = pl.BlockSpec((None, n_heads, t, LANES), lambda b: (b, 0, 0, 0))
        out_shape = jax.ShapeDtypeStruct((nb, n_heads, t, LANES), BF16)
    else:
        out_spec = pl.BlockSpec((None, n_heads, t), lambda b: (b, 0, 0))
        out_shape = jax.ShapeDtypeStruct((nb, n_heads, t), F32)
    return pl.pallas_call(
        functools.partial(_cumsum_kernel, chunk=chunk, n_heads=n_heads, as_key_columns=as_key_columns),
        grid=(nb,),
        in_specs=[pl.BlockSpec((None, t, LANES), lambda b: (b, 0, 0))],
        out_specs=out_spec,
        out_shape=out_shape,
        compiler_params=_params("arbitrary"),
        name="cumsum_keys" if as_key_columns else "cumsum_rows",
    )(flp)


def _t5_bucket(rel):
    nb = REL_BUCKETS // 2
    max_exact = nb // 2
    n = jnp.abs(rel)
    nf = jnp.maximum(n, 1).astype(jnp.float32)
    large = max_exact + (jnp.log(nf / max_exact) / math.log(REL_MAX_DIST / max_exact)
                         * (nb - max_exact)).astype(jnp.int32)
    large = jnp.minimum(large, nb - 1)
    return jnp.where(rel > 0, nb, 0) + jnp.where(n < max_exact, n, large)


def _bias_kernel(table_ref, idx_ref, mask_ref, o_ref, *, minus_bucket):
    h = pl.program_id(0)
    idx = idx_ref[...]
    acc = mask_ref[...]
    if minus_bucket is not None:
        acc = acc - table_ref[minus_bucket, h]
    vals = [table_ref[b, h] for b in range(REL_BUCKETS)]
    bit = 1
    while len(vals) > 1:
        odd = (idx & bit) != 0
        vals = [jnp.where(odd, vals[2 * i + 1], vals[2 * i]) for i in range(len(vals) // 2)]
        bit *= 2
    o_ref[...] = (acc + vals[0]) * LOG2E


def _bias_tiles(rel_table, qpos, kpos, keys_major=False, minus_bucket=None):
    n_heads = rel_table.shape[1]
    if keys_major:
        qp, kp = qpos[:, None, :], kpos[:, :, None]
    else:
        qp, kp = qpos[:, :, None], kpos[:, None, :]
    idx = _t5_bucket(kp - qp).astype(jnp.int32)
    mask = jnp.where((kp // CHUNK) <= (qp // CHUNK), 0.0, NEG_INF).astype(F32)
    nt, r, c = idx.shape
    return pl.pallas_call(
        functools.partial(_bias_kernel, minus_bucket=minus_bucket),
        grid=(n_heads, nt),
        in_specs=[
            pl.BlockSpec(memory_space=pltpu.SMEM),
            pl.BlockSpec((None, r, c), lambda h, t: (t, 0, 0)),
            pl.BlockSpec((None, r, c), lambda h, t: (t, 0, 0)),
        ],
        out_specs=pl.BlockSpec((None, None, r, c), lambda h, t: (h, t, 0, 0)),
        out_shape=jax.ShapeDtypeStruct((n_heads, nt, r, c), F32),
        compiler_params=_params("arbitrary", "arbitrary"),
        name="bias_tiles",
    )(rel_table.astype(F32), idx, mask)


def _far_bucket(min_dist):
    nb = REL_BUCKETS // 2
    max_exact = nb // 2
    large = max_exact + math.log(min_dist / max_exact) / math.log(REL_MAX_DIST / max_exact) * (nb - max_exact)
    assert large >= nb - 1 + 0.5, "key tile too short for a constant far-field bias"
    return nb - 1


def _lambda_value(lam_ref, lam_init):
    a = lam_ref[...]
    s1 = jnp.sum(a[0:1] * a[1:2], axis=-1, keepdims=True)
    s2 = jnp.sum(a[2:3] * a[3:4], axis=-1, keepdims=True)
    return jnp.exp(s1) - jnp.exp(s2) + lam_init


def _prompt_attn_kernel(table_ref, lam_ref, subln_ref, qd_ref, kd_ref, vd_ref, qf_ref, kf_ref, vf_ref,
                        fcol_ref, bias_ref, cmask_ref, od_ref, of_ref,
                        kda_ref, kfa_ref, vdt_ref, vft_ref,
                        md_ref, accd_ref, mf_ref, accf_ref,
                        sd_ref, pd_ref, ad_ref, sf_ref, pf_ref, af_ref,
                        *, tq, tk, lam_init, far_bucket):
    h = pl.program_id(1)
    qi = pl.program_id(2)
    seq = kd_ref.shape[0]
    ratio = tq // tk

    @pl.when(qi == 0)
    def _():
        lane = lax.broadcasted_iota(jnp.int32, (seq, LANES), 1)
        kda_ref[:, :HEAD_DIM] = kd_ref[...]
        kda_ref[:, HEAD_DIM:] = jnp.where(lane < 3, 1.0, 0.0).astype(BF16)
        kfa_ref[:, :HEAD_DIM] = kf_ref[...]
        kfa_ref[:, HEAD_DIM:] = fcol_ref[...]

        ones = jnp.ones((ONES_ROWS, tk), BF16)

        def transpose_values(c, carry):
            r = pl.ds(pl.multiple_of(c * tk, tk), tk)
            vdt_ref[c] = jnp.concatenate([_transpose_bf16(vd_ref[r, :]), ones], axis=0)
            vft_ref[c] = jnp.concatenate([_transpose_bf16(vf_ref[r, :]), ones], axis=0)
            return carry

        lax.fori_loop(0, seq // tk, transpose_values, 0)

    def key_rows(ref, j):
        return ref[pl.ds(pl.multiple_of(j * tk, tk), tk), :]

    row = lax.broadcasted_iota(jnp.int32, (HEAD_DIM, tq), 0)
    qdt = qd_ref[...].astype(F32).T
    zero = jnp.zeros_like(qdt)
    top = jnp.concatenate([jnp.where(row < DIFF_QK_DIM, qdt, zero),
                           jnp.where(row >= DIFF_QK_DIM, qdt, zero)], axis=1).astype(BF16)
    row2 = lax.broadcasted_iota(jnp.int32, (HEAD_DIM, 2 * tq), 0)
    bottom = _split3_on_axis(jnp.full((HEAD_DIM, 2 * tq), table_ref[far_bucket, h] * LOG2E, F32), row2)
    qd_aug = jnp.concatenate([top, bottom], axis=0)
    qf_aug = jnp.concatenate([qf_ref[...].astype(F32).T.astype(BF16),
                              jnp.where(row < 3, -1.0, 0.0).astype(BF16)], axis=0)

    branches = ((kda_ref, qd_aug, vdt_ref, sd_ref, pd_ref, ad_ref, md_ref, accd_ref),
                (kfa_ref, qf_aug, vft_ref, sf_ref, pf_ref, af_ref, mf_ref, accf_ref))

    def pipeline_step(j_prev, adds, j_next):
        for (k_ref, q_aug, vt_ref, s_ref, p_ref, a_ref, m_ref, acc_ref), add in zip(branches, adds):
            acc_ref[...] = a_ref[...] * acc_ref[...] + _dot(vt_ref[j_prev], p_ref[...])
            s = s_ref[...]
            if add is not None:
                s = s + add
            m_prev = m_ref[...]
            m_new = jnp.maximum(m_prev, jnp.max(s, axis=0, keepdims=True))
            p_ref[...] = jnp.exp2(s - m_new).astype(BF16)
            a_ref[...] = jnp.exp2(m_prev - m_new)
            m_ref[...] = m_new
            if j_next is not None:
                s_ref[...] = _dot(key_rows(k_ref, j_next), q_aug)

    for k_ref, q_aug, vt_ref, s_ref, p_ref, a_ref, m_ref, acc_ref in branches:
        m_ref[...] = jnp.full_like(m_ref, NEG_INF)
        acc_ref[...] = jnp.zeros_like(acc_ref)
        s_ref[...] = _dot(key_rows(k_ref, 0), q_aug)
        p_ref[...] = jnp.zeros_like(p_ref)
        a_ref[...] = jnp.ones_like(a_ref)

    def far_tiles(unroll):
        def body(i, carry):
            for u in range(unroll):
                j = i * unroll + u
                pipeline_step(jnp.maximum(j - 1, 0), (None, None), j + 1)
            return carry
        return body

    first_diag = qi * ratio
    n_far = jnp.maximum(first_diag - 1, 0)
    n_main = n_far // FAR_UNROLL
    lax.fori_loop(0, n_main, far_tiles(FAR_UNROLL), 0)
    lax.fori_loop(n_main * FAR_UNROLL, n_far, far_tiles(1), 0)

    def doubled(bias):
        return jnp.concatenate([bias, bias], axis=1)

    near = jnp.maximum(first_diag - 1, 0)
    gone = jnp.where(qi == 0, NEG_INF, 0.0)
    pipeline_step(jnp.maximum(near - 1, 0), (doubled(bias_ref[0]) + gone, gone), first_diag)
    for u in range(ratio):
        pipeline_step(near if u == 0 else first_diag + u - 1,
                      (doubled(bias_ref[u + 1]), cmask_ref[u]),
                      first_diag + u + 1 if u + 1 < ratio else None)
    last = first_diag + ratio - 1
    for k_ref, q_aug, vt_ref, s_ref, p_ref, a_ref, m_ref, acc_ref in branches:
        acc_ref[...] = a_ref[...] * acc_ref[...] + _dot(vt_ref[last], p_ref[...])

    lam = _lambda_value(lam_ref, lam_init)
    acc = accd_ref[...]
    num, den = acc[:HEAD_DIM], acc[HEAD_DIM:HEAD_DIM + 1]
    odt = num[:, :tq] / den[:, :tq] - lam * (num[:, tq:] / den[:, tq:])
    od_ref[...] = (_rms_rows(odt.T, subln_ref[...]) * (1.0 - lam_init)).astype(BF16)
    acc = accf_ref[...]
    of_ref[...] = (acc[:HEAD_DIM] / acc[HEAD_DIM:HEAD_DIM + 1]).T.astype(BF16)


def _prompt_attention(zb, fcols, rel_table, lam_params, subln, lam_init, batch, seq, n_heads, tq, tk):
    m = batch * seq
    dw = n_heads * HEAD_DIM
    nq = seq // tq
    ratio = tq // tk
    assert tq == ratio * tk
    far_bucket = _far_bucket(tk + 1)
    vrows = HEAD_DIM + ONES_ROWS
    qpos = tk + jnp.arange(tq, dtype=jnp.int32)
    kpos = jnp.arange((ratio + 1) * tk, dtype=jnp.int32).reshape(ratio + 1, tk)
    bias = _bias_tiles(rel_table, jnp.broadcast_to(qpos, (ratio + 1, tq)), kpos,
                       keys_major=True, minus_bucket=far_bucket)
    cmask = jnp.where(kpos[1:, :, None] <= qpos[None, None, :], 0.0, NEG_INF).astype(F32)

    def q_spec(seg):
        return pl.BlockSpec((tq, HEAD_DIM), lambda b, h, q, seg=seg: (b * nq + q, seg * n_heads + h))

    def kv_spec(seg):
        return pl.BlockSpec((seq, HEAD_DIM), lambda b, h, q, seg=seg: (b, seg * n_heads + h))

    out_spec = pl.BlockSpec((tq, HEAD_DIM), lambda b, h, q: (b * nq + q, h))
    const2 = lambda b, h, q: (0, 0)
    kern = functools.partial(_prompt_attn_kernel, tq=tq, tk=tk, lam_init=lam_init, far_bucket=far_bucket)
    return pl.pallas_call(
        kern,
        grid=(batch, n_heads, nq),
        in_specs=[
            pl.BlockSpec(memory_space=pltpu.SMEM),
            pl.BlockSpec((4, DIFF_QK_DIM), const2),
            pl.BlockSpec((1, HEAD_DIM), const2),
            q_spec(0), kv_spec(1), kv_spec(2), q_spec(3), kv_spec(4), kv_spec(5),
            pl.BlockSpec((None, None, seq, LANES), lambda b, h, q: (b, h, 0, 0)),
            pl.BlockSpec((None, ratio + 1, tk, tq), lambda b, h, q: (h, 0, 0, 0)),
            pl.BlockSpec((ratio, tk, tq), lambda b, h, q: (0, 0, 0)),
        ],
        out_specs=[out_spec, out_spec],
        out_shape=[jax.ShapeDtypeStruct((m, dw), BF16), jax.ShapeDtypeStruct((m, dw), BF16)],
        scratch_shapes=[
            pltpu.VMEM((seq, 2 * HEAD_DIM), BF16), pltpu.VMEM((seq, 2 * HEAD_DIM), BF16),
            pltpu.VMEM((seq // tk, vrows, tk), BF16), pltpu.VMEM((seq // tk, vrows, tk), BF16),
            pltpu.VMEM((1, 2 * tq), F32), pltpu.VMEM((vrows, 2 * tq), F32),
            pltpu.VMEM((1, tq), F32), pltpu.VMEM((vrows, tq), F32),
            pltpu.VMEM((tk, 2 * tq), F32), pltpu.VMEM((tk, 2 * tq), BF16), pltpu.VMEM((1, 2 * tq), F32),
            pltpu.VMEM((tk, tq), F32), pltpu.VMEM((tk, tq), BF16), pltpu.VMEM((1, tq), F32),
        ],
        compiler_params=_params("arbitrary", "arbitrary", "arbitrary"),
        name="prompt_attention",
    )(rel_table.astype(F32), lam_params, subln, zb, zb, zb, zb, zb, zb, fcols, bias, cmask)


def _split_diff_queries(q):
    lane = lax.broadcasted_iota(jnp.int32, q.shape, 1)
    zero = jnp.zeros_like(q)
    return jnp.concatenate([jnp.where(lane < DIFF_QK_DIM, q, zero),
                            jnp.where(lane >= DIFF_QK_DIM, q, zero)], axis=0)


def _sample_attn_kernel(lam_ref, subln_ref, z_ref, kdc_ref, vdc_ref, kfc_ref, vfc_ref, frow_ref, bias_ref,
                        cmask_ref, od_ref, of_ref, *, past, t_new, n_heads, lam_init):
    dw = n_heads * HEAD_DIM

    def cached(ref, h):
        return ref[pl.ds(h, past, stride=n_heads), :].astype(BF16)

    def new_rows(seg, h):
        lo = seg * dw + h * HEAD_DIM
        return z_ref[:, lo:lo + HEAD_DIM]

    def attend(q, kc, vc, kn, vn, add_c, add_n):
        sc = _dot_nt(q, kc) + add_c
        sn = _dot_nt(q, kn) + add_n
        mx = jnp.maximum(jnp.max(sc, axis=-1, keepdims=True), jnp.max(sn, axis=-1, keepdims=True))
        pc = jnp.exp2(sc - mx)
        pn = jnp.exp2(sn - mx)
        l = jnp.sum(pc, axis=-1, keepdims=True) + jnp.sum(pn, axis=-1, keepdims=True)
        acc = _dot(pc.astype(BF16), vc) + _dot(pn.astype(BF16), vn)
        return acc, l

    lam = _lambda_value(lam_ref, lam_init)
    cmask = cmask_ref[...]
    for h in range(n_heads):
        cols = slice(h * HEAD_DIM, (h + 1) * HEAD_DIM)
        qs = _split_diff_queries(new_rows(0, h))
        bias_c = bias_ref[h, :, :past]
        bias_n = bias_ref[h, :, past:past + t_new]
        acc, l = attend(qs, cached(kdc_ref, h), cached(vdc_ref, h),
                        new_rows(1, h), new_rows(2, h),
                        jnp.concatenate([bias_c, bias_c], axis=0), jnp.concatenate([bias_n, bias_n], axis=0))
        od = acc[:t_new] / l[:t_new] - lam * (acc[t_new:] / l[t_new:])
        od_ref[:, cols] = (_rms_rows(od, subln_ref[...]) * (1.0 - lam_init)).astype(BF16)

        f0 = frow_ref[h:h + 1, past:past + 1]
        dec_c = (f0 - frow_ref[h:h + 1, :past]) * LOG2E
        dec_n = (f0 - frow_ref[h:h + 1, past:past + t_new]) * LOG2E
        acc, l = attend(new_rows(3, h), cached(kfc_ref, h), cached(vfc_ref, h),
                        new_rows(4, h), new_rows(5, h), dec_c, dec_n + cmask)
        of_ref[:, cols] = (acc / l).astype(BF16)


def _sample_attention(zb, caches, layer, frow, rel_table, lam_params, subln, lam_init, nb, t_new, past, n_heads):
    dw = n_heads * HEAD_DIM
    tpad = frow.shape[-1]
    ar = jnp.arange(t_new, dtype=jnp.int32)
    bias = _bias_tiles(rel_table, (past + ar)[None], jnp.arange(tpad, dtype=jnp.int32)[None])[:, 0]
    cmask = jnp.where(ar[None, :] <= ar[:, None], 0.0, NEG_INF).astype(F32)
    caches = [c.reshape(c.shape[0], nb, past * n_heads, HEAD_DIM) for c in caches]
    cache_spec = pl.BlockSpec((None, None, past * n_heads, HEAD_DIM), lambda b: (layer, b, 0, 0))
    out_spec = pl.BlockSpec((t_new, dw), lambda b: (b, 0))
    const2 = lambda b: (0, 0)
    kern = functools.partial(_sample_attn_kernel, past=past, t_new=t_new, n_heads=n_heads, lam_init=lam_init)
    return pl.pallas_call(
        kern,
        grid=(nb,),
        in_specs=[
            pl.BlockSpec((4, DIFF_QK_DIM), const2),
            pl.BlockSpec((1, HEAD_DIM), const2),
            pl.BlockSpec((t_new, 6 * dw), lambda b: (b, 0)),
            cache_spec, cache_spec, cache_spec, cache_spec,
            pl.BlockSpec((None, n_heads, tpad), lambda b: (b, 0, 0)),
            pl.BlockSpec((n_heads, t_new, tpad), lambda b: (0, 0, 0)),
            pl.BlockSpec((t_new, t_new), const2),
        ],
        out_specs=[out_spec, out_spec],
        out_shape=[jax.ShapeDtypeStruct((nb * t_new, dw), BF16), jax.ShapeDtypeStruct((nb * t_new, dw), BF16)],
        compiler_params=_params("arbitrary"),
        name="sample_attention",
    )(lam_params, subln, zb, *caches, frow, bias, cmask)


def _merge_kernel(x_ref, g1_ref, od_ref, of_ref, wbd_ref, wbf_ref, wga_ref, wgb_ref, wout_ref, gp_ref,
                  o_ref, *, chunk):
    x = x_ref[...]
    xn = _rms_rows(x, g1_ref[...]).astype(BF16)
    od = od_ref[...]
    of = of_ref[...]
    o = None
    for c in range(x.shape[1] // chunk):
        cols = slice(c * chunk, (c + 1) * chunk)
        ga = jax.nn.sigmoid(_dot(xn, wga_ref[:, cols]))
        gb = jax.nn.sigmoid(_dot(xn, wgb_ref[:, cols]))
        u = (ga * _dot(od, wbd_ref[:, cols]) + gb * _dot(of, wbf_ref[:, cols])).astype(BF16)
        part = _dot(u, wout_ref[cols, :])
        o = part if o is None else o + part
    o_ref[...] = x + _rms_rows(o, gp_ref[...])


def _merge(x, g1, od, of, w_bd, w_bf, w_ga, w_gb, w_out, g_post, tm, chunk):
    m, d = x.shape
    dw = od.shape[1]
    row = lambda i: (i, 0)
    const2 = lambda i: (0, 0)
    resident = lambda shape: pl.BlockSpec(shape, const2, pipeline_mode=pl.Buffered(1))
    return pl.pallas_call(
        functools.partial(_merge_kernel, chunk=chunk),
        grid=(m // tm,),
        in_specs=[
            pl.BlockSpec((tm, d), row),
            pl.BlockSpec((1, d), const2),
            pl.BlockSpec((tm, dw), row),
            pl.BlockSpec((tm, dw), row),
            resident((dw, d)), resident((dw, d)), resident((d, d)), resident((d, d)), resident((d, d)),
            pl.BlockSpec((1, d), const2),
        ],
        out_specs=pl.BlockSpec((tm, d), row),
        out_shape=jax.ShapeDtypeStruct((m, d), F32),
        compiler_params=_params("arbitrary"),
        name="merge",
    )(x, g1, od, of, w_bd, w_bf, w_ga, w_gb, w_out, g_post)


def _gelu_tanh(x):
    k = -2.0 * math.sqrt(2.0 / math.pi) * LOG2E
    return x / (1.0 + jnp.exp2(x * (k + (k * 0.044715) * (x * x))))


def _ffn_kernel(*refs, seq, has_edges):
    if has_edges:
        (h_ref, g2_ref, wa_ref, wb_ref, cw_ref, cb_ref, wd_ref, gp_ref, e0_ref, e1_ref,
         o_ref, cs_ref, xn_ref, tail_ref) = refs
    else:
        (h_ref, g2_ref, wa_ref, wb_ref, cw_ref, cb_ref, wd_ref, gp_ref,
         o_ref, cs_ref, xn_ref, tail_ref) = refs
    i = pl.program_id(0)
    f = pl.program_id(1)
    tm = h_ref.shape[0]

    tf = wa_ref.shape[1]

    @pl.when(f == 0)
    def _():
        xn_ref[...] = _rms_rows(h_ref[...], g2_ref[...]).astype(BF16)
        o_ref[...] = jnp.zeros_like(o_ref)

    if not has_edges:
        @pl.when((i * tm) % seq == 0)
        def _():
            tail_ref[f] = jnp.zeros(tail_ref.shape[1:], F32)

    xn = xn_ref[...]

    def gated_chunk(cols):
        width = cols.stop - cols.start
        a = _dot(xn, wa_ref[:, cols])
        gate = _dot(xn, wb_ref[:, cols])
        back1 = pltpu.roll(a, 1, 0)
        back2 = pltpu.roll(a, 2, 0)
        if has_edges:
            t = lax.broadcasted_iota(jnp.int32, a.shape, 0) % seq
            am1 = jnp.where(t >= 1, back1, 0.0) + e1_ref[:, cols]
            am2 = jnp.where(t >= 2, back2, 0.0) + e0_ref[:, cols]
            cs_ref[:, :, cols] = a.reshape(tm // seq, seq, width)[:, seq - (CONV_WIDTH - 1):, :]
        else:
            prev = tail_ref[f, :, cols]
            p0 = prev[SUBLANES - 2:SUBLANES - 1]
            p1 = prev[SUBLANES - 1:SUBLANES]
            top = lax.broadcasted_iota(jnp.int32, (SUBLANES, width), 0)
            am1 = jnp.concatenate([jnp.where(top == 0, p1, back1[:SUBLANES]), back1[SUBLANES:]], axis=0)
            am2 = jnp.concatenate([jnp.where(top == 0, p0, jnp.where(top == 1, p1, back2[:SUBLANES])),
                                   back2[SUBLANES:]], axis=0)
            tail_ref[f, :, cols] = a[tm - SUBLANES:, :]
            cs_ref[:, :, cols] = a[tm - (CONV_WIDTH - 1):, :][None]
        cw = cw_ref[:, cols]
        ac = cw[0:1] * am2 + cw[1:2] * am1 + cw[2:3] * a + cb_ref[:, cols]
        return (_gelu_tanh(ac) * gate).astype(BF16)

    width = tf // FFN_CHUNKS
    contrib = None
    for c in range(FFN_CHUNKS):
        cols = slice(c * width, (c + 1) * width)
        part = _dot(gated_chunk(cols), wd_ref[cols, :])
        contrib = part if contrib is None else contrib + part
    o_ref[...] += contrib

    @pl.when(f == pl.num_programs(1) - 1)
    def _():
        o_ref[...] = h_ref[...] + _rms_rows(o_ref[...], gp_ref[...])


def _ffn(h, g2, w_up, conv_w, conv_b, w_d, g_post, edges, seq, tm, tf):
    m, d = h.shape
    dff = w_d.shape[0]
    nf = dff // tf
    has_edges = edges is not None
    row = lambda i, f: (i, 0)
    col = lambda i, f: (0, f)
    const2 = lambda i, f: (0, 0)
    in_specs = [
        pl.BlockSpec((tm, d), row),
        pl.BlockSpec((1, d), const2),
        pl.BlockSpec((d, tf), col),
        pl.BlockSpec((d, tf), lambda i, f: (0, nf + f)),
        pl.BlockSpec((CONV_WIDTH, tf), col),
        pl.BlockSpec((1, tf), col),
        pl.BlockSpec((tf, d), lambda i, f: (f, 0)),
        pl.BlockSpec((1, d), const2),
    ]
    args = [h, g2, w_up, w_up, conv_w, conv_b, w_d, g_post]
    if has_edges:
        assert tm % seq == 0
        in_specs += [pl.BlockSpec((tm, tf), lambda i, f: (i, f))] * 2
        args += list(edges)
        tails_per_tile = tm // seq
    else:
        assert seq % tm == 0
        tails_per_tile = 1
    n_tails = (m // tm) * tails_per_tile
    out, tails = pl.pallas_call(
        functools.partial(_ffn_kernel, seq=seq, has_edges=has_edges),
        grid=(m // tm, nf),
        in_specs=in_specs,
        out_specs=[pl.BlockSpec((tm, d), row),
                   pl.BlockSpec((tails_per_tile, CONV_WIDTH - 1, tf), lambda i, f: (i, 0, f))],
        out_shape=[jax.ShapeDtypeStruct((m, d), F32),
                   jax.ShapeDtypeStruct((n_tails, CONV_WIDTH - 1, dff), F32)],
        scratch_shapes=[pltpu.VMEM((tm, d), BF16), pltpu.VMEM((nf, SUBLANES, tf), F32)],
        compiler_params=_params("arbitrary", "arbitrary"),
        name="conv_ffn",
    )(*args)
    if not has_edges:
        tiles_per_seq = seq // tm
        tails = tails[tiles_per_seq - 1::tiles_per_seq]
    return out, tails


def _tile(n, cap):
    t = min(n, cap)
    assert n % t == 0
    return t


def _layer(layer, hp, hs, past, rel_table, lam_params, lam_init, p):
    (pre1, w_in, b_forget, subln, w_bd, w_bf, w_out, post1, pre2, w_up, conv_w, conv_b, w_down, post2) = p
    batch, seq, d = hp.shape
    nb, t_new, _ = hs.shape
    n_heads = d // (2 * HEAD_DIM)
    dw = n_heads * HEAD_DIM
    dff = w_down.shape[0]
    caches, plogf, conv_prev = past
    plen = plogf.shape[1]

    row2 = lambda v: v.reshape(1, -1).astype(F32)
    w_qkv = w_in[:, :6 * dw].astype(BF16)
    w_f = jnp.pad(w_in[:, 6 * dw:6 * dw + n_heads], ((0, 0), (0, LANES - n_heads))).astype(BF16)
    b_f = jnp.pad(b_forget.astype(F32), (0, LANES - n_heads)).reshape(1, LANES)
    w_ga = w_in[:, 6 * dw + n_heads:6 * dw + n_heads + d].astype(BF16)
    w_gb = w_in[:, 6 * dw + n_heads + d:].astype(BF16)
    w_bd_b, w_bf_b, w_out_b = w_bd.astype(BF16), w_bf.astype(BF16), w_out.astype(BF16)
    w_up_b = w_up.astype(BF16)
    w_d = w_down.astype(BF16)
    subln2 = row2(subln)

    def dense_tail(x2, od, of, edges, t_seq):
        m = x2.shape[0]
        h1 = _merge(x2, row2(pre1), od, of, w_bd_b, w_bf_b, w_ga, w_gb, w_out_b, row2(post1),
                    _tile(m, MERGE_ROWS), _tile(d, 512))
        return _ffn(h1, row2(pre2), w_up_b, conv_w.astype(F32), row2(conv_b), w_d, row2(post2),
                    edges, t_seq, _tile(m, 512), _tile(dff, 1024))

    heads = lambda a, n, t: a.reshape(n, t, n_heads, HEAD_DIM)

    xp = hp.reshape(batch * seq, d)
    zb, kd, vd, kf, vf, logf, flp = _inproj(xp, row2(pre1), w_qkv, w_f, b_f, n_heads,
                                            _tile(batch * seq, INPROJ_ROWS))
    fcols = _cumsum(flp.reshape(batch, seq, LANES), n_heads, _tile(seq, 256), as_key_columns=True)
    od, of = _prompt_attention(zb, fcols, rel_table, lam_params, subln2, lam_init,
                               batch, seq, n_heads, _tile(seq, ATTN_QUERY_TILE), _tile(seq, ATTN_KEY_TILE))
    hp_out, conv_p = dense_tail(xp, od, of, None, seq)
    state_p = (heads(kd, batch, seq), heads(vd, batch, seq), heads(kf, batch, seq), heads(vf, batch, seq),
               logf.reshape(batch, seq, n_heads), conv_p)

    xs = hs.reshape(nb * t_new, d)
    zb, kd, vd, kf, vf, logf, flp = _inproj(xs, row2(pre1), w_qkv, w_f, b_f, n_heads,
                                            _tile(nb * t_new, INPROJ_ROWS))
    tpad = -(-(plen + t_new) // LANES) * LANES
    flog = jnp.concatenate([jnp.pad(plogf.astype(F32), ((0, 0), (0, 0), (0, LANES - n_heads))),
                            flp.reshape(nb, t_new, LANES),
                            jnp.zeros((nb, tpad - plen - t_new, LANES), F32)], axis=1)
    frow = _cumsum(flog, n_heads, LANES, as_key_columns=False)
    od, of = _sample_attention(zb, caches, layer, frow, rel_table, lam_params, subln2, lam_init,
                               nb, t_new, plen, n_heads)
    cp = conv_prev.astype(F32)
    e0 = jnp.zeros((nb, t_new, dff), F32).at[:, 0].set(cp[:, 0]).at[:, 1].set(cp[:, 1])
    e1 = jnp.zeros((nb, t_new, dff), F32).at[:, 0].set(cp[:, 1])
    edges = (e0.reshape(nb * t_new, dff), e1.reshape(nb * t_new, dff))
    hs_out, conv_s = dense_tail(xs, od, of, edges, t_new)
    state_s = (heads(kd, nb, t_new), heads(vd, nb, t_new), heads(kf, nb, t_new), heads(vf, nb, t_new),
               logf.reshape(nb, t_new, n_heads), conv_s)
    return hp_out.reshape(batch, seq, d), hs_out.reshape(nb, t_new, d), state_p, state_s


def kernel(x_prompt, x_sample, cache_diff_k, cache_diff_v, cache_fox_k, cache_fox_v, cache_fox_logf,
           state_ffn_conv, rel_table, pre_norm1, w_in, b_forget, lam_q1, lam_k1, lam_q2, lam_k2,
           diff_subln, w_branch_diff, w_branch_fox, w_out, post_norm1, pre_norm2, w_up, conv_w, conv_b,
           w_down, post_norm2):
    depth = w_in.shape[0]
    hp, hs = x_prompt, x_sample
    caches = (cache_diff_k, cache_diff_v, cache_fox_k, cache_fox_v)
    new_p, new_s = [], []
    for l in range(depth):
        lam_init = 0.8 - 0.6 * math.exp(-0.3 * l)
        lam_params = jnp.stack([lam_q1[l], lam_k1[l], lam_q2[l], lam_k2[l]]).astype(F32)
        params = (pre_norm1[l], w_in[l], b_forget[l], diff_subln[l], w_branch_diff[l], w_branch_fox[l],
                  w_out[l], post_norm1[l], pre_norm2[l], w_up[l], conv_w[l], conv_b[l], w_down[l],
                  post_norm2[l])
        past = (caches, cache_fox_logf[l], state_ffn_conv[l])
        hp, hs, sp, ss = _layer(l, hp, hs, past, rel_table, lam_params, lam_init, params)
        new_p.append(sp)
        new_s.append(ss)
    st = lambda lst, i: jnp.stack([e[i] for e in lst])
    return (hp, hs,
            st(new_p, 0), st(new_p, 1), st(new_p, 2), st(new_p, 3), st(new_p, 4), st(new_p, 5),
            st(new_s, 0), st(new_s, 1), st(new_s, 2), st(new_s, 3), st(new_s, 4), st(new_s, 5))
```

```python
import functools
import math

import jax
import jax.numpy as jnp
from jax import lax
from jax.experimental import pallas as pl
from jax.experimental.pallas import tpu as pltpu

HEAD_DIM = 128
DIFF_QK_DIM = HEAD_DIM // 2
CHUNK = 64
CONV_WIDTH = 3
REL_BUCKETS = 32
REL_MAX_DIST = 128
EPS = 1e-6
NEG_INF = -1e30
LOG2E = math.log2(math.e)

LANES = 128
SUBLANES = 8
ONES_ROWS = 16
ATTN_QUERY_TILE = 512
ATTN_KEY_TILE = 512
FAR_UNROLL = 2
FFN_CHUNKS = 2
MERGE_ROWS = 256
INPROJ_ROWS = 256
VMEM_LIMIT_BYTES = 60 * 1024 * 1024

F32 = jnp.float32
BF16 = jnp.bfloat16


def _params(*sem, flags=None):
    return pltpu.CompilerParams(dimension_semantics=sem, vmem_limit_bytes=VMEM_LIMIT_BYTES, flags=flags)


def _rms_rows(x, g):
    return x * lax.rsqrt(jnp.mean(x * x, axis=-1, keepdims=True) + EPS) * g


def _dot(a, b):
    return jnp.dot(a, b, preferred_element_type=F32)


def _dot_nt(a, b):
    return lax.dot_general(a, b, (((1,), (1,)), ((), ())), preferred_element_type=F32)


def _split3(x):
    hi = x.astype(BF16)
    r1 = x - hi.astype(F32)
    mid = r1.astype(BF16)
    lo = (r1 - mid.astype(F32)).astype(BF16)
    return hi, mid, lo


def _split3_on_axis(x, index):
    hi, mid, lo = (v.astype(F32) for v in _split3(x))
    return jnp.where(index == 0, hi, jnp.where(index == 1, mid, jnp.where(index == 2, lo, 0.0))).astype(BF16)


def _transpose_bf16(x):
    return x.astype(F32).T.astype(BF16)


def _inproj_kernel(x_ref, g_ref, w_ref, wf_ref, bf_ref,
                   zb_ref, kd_ref, vd_ref, kf_ref, vf_ref, logf_ref, flp_ref, *, n_heads, q_scales):
    dw = kd_ref.shape[1]
    xn = _rms_rows(x_ref[...], g_ref[...]).astype(BF16)
    fl = _dot(xn, wf_ref[...]) + bf_ref[...]
    lf = jnp.minimum(fl, 0.0) - jnp.log1p(jnp.exp(-jnp.abs(fl)))
    flp_ref[...] = lf
    logf_ref[...] = lf[:, :n_heads]
    f32_outs = {1: kd_ref, 2: vd_ref, 4: kf_ref, 5: vf_ref}
    for seg in range(6):
        cols = slice(seg * dw, (seg + 1) * dw)
        z = _dot(xn, w_ref[:, cols])
        if seg in f32_outs:
            f32_outs[seg][...] = z
        else:
            z = z * q_scales[seg]
        zb_ref[:, cols] = z.astype(BF16)


def _inproj(x, g, w_qkv, w_f, b_f, n_heads, tm):
    m, d = x.shape
    dw = w_qkv.shape[1] // 6
    row = lambda i: (i, 0)
    const2 = lambda i: (0, 0)
    f32_out = jax.ShapeDtypeStruct((m, dw), F32)
    q_scales = {0: DIFF_QK_DIM ** -0.5 * LOG2E, 3: HEAD_DIM ** -0.5 * LOG2E}
    return pl.pallas_call(
        functools.partial(_inproj_kernel, n_heads=n_heads, q_scales=q_scales),
        grid=(m // tm,),
        in_specs=[
            pl.BlockSpec((tm, d), row),
            pl.BlockSpec((1, d), const2),
            pl.BlockSpec((d, 6 * dw), const2, pipeline_mode=pl.Buffered(1)),
            pl.BlockSpec((d, LANES), const2),
            pl.BlockSpec((1, LANES), const2),
        ],
        out_specs=[
            pl.BlockSpec((tm, 6 * dw), row),
            pl.BlockSpec((tm, dw), row),
            pl.BlockSpec((tm, dw), row),
            pl.BlockSpec((tm, dw), row),
            pl.BlockSpec((tm, dw), row),
            pl.BlockSpec((tm, n_heads), row),
            pl.BlockSpec((tm, LANES), row),
        ],
        out_shape=[
            jax.ShapeDtypeStruct((m, 6 * dw), BF16),
            f32_out, f32_out, f32_out, f32_out,
            jax.ShapeDtypeStruct((m, n_heads), F32),
            jax.ShapeDtypeStruct((m, LANES), F32),
        ],
        compiler_params=_params("arbitrary"),
        name="inproj",
    )(x, g, w_qkv, w_f, b_f)


def _triangle(n, lower):
    r = lax.broadcasted_iota(jnp.int32, (n, n), 0)
    c = lax.broadcasted_iota(jnp.int32, (n, n), 1)
    return ((r >= c) if lower else (r <= c)).astype(BF16)


def _cumsum_keys_kernel(x_ref, o_ref, *, chunk, n_heads):
    t = x_ref.shape[0]
    tri = _triangle(chunk, lower=True)
    lane = lax.broadcasted_iota(jnp.int32, (chunk, LANES), 1)
    carry = jnp.zeros((1, LANES), F32)
    for k in range(t // chunk):
        hi, mid, lo = _split3(x_ref[k * chunk:(k + 1) * chunk, :])
        local = _dot(tri, hi) + _dot(tri, mid) + _dot(tri, lo)
        f = local + carry
        carry = carry + local[chunk - 1:chunk, :]
        for h in range(n_heads):
            o_ref[h, k * chunk:(k + 1) * chunk, :] = _split3_on_axis(
                jnp.broadcast_to(f[:, h:h + 1] * LOG2E, (chunk, LANES)), lane)


def _cumsum_keys(flp, n_heads, chunk):
    nb, t, _ = flp.shape
    return pl.pallas_call(
        functools.partial(_cumsum_keys_kernel, chunk=chunk, n_heads=n_heads),
        grid=(nb,),
        in_specs=[pl.BlockSpec((None, t, LANES), lambda b: (b, 0, 0))],
        out_specs=pl.BlockSpec((None, n_heads, t, LANES), lambda b: (b, 0, 0, 0)),
        out_shape=jax.ShapeDtypeStruct((nb, n_heads, t, LANES), BF16),
        compiler_params=_params("arbitrary"),
        name="cumsum_keys",
    )(flp)


def _cumsum_rows_kernel(x_ref, o_ref):
    t = x_ref.shape[1]
    tri = _triangle(LANES, lower=False)
    carry = jnp.zeros((x_ref.shape[0], 1), F32)
    for k in range(t // LANES):
        cols = slice(k * LANES, (k + 1) * LANES)
        hi, mid, lo = _split3(x_ref[:, cols])
        local = _dot(hi, tri) + _dot(mid, tri) + _dot(lo, tri)
        o_ref[:, cols] = local + carry
        carry = carry + local[:, LANES - 1:LANES]


def _cumsum_rows(x):
    return pl.pallas_call(
        _cumsum_rows_kernel,
        out_shape=jax.ShapeDtypeStruct(x.shape, F32),
        compiler_params=_params(),
        name="cumsum_rows",
    )(x)


def _t5_bucket(rel):
    nb = REL_BUCKETS // 2
    max_exact = nb // 2
    n = jnp.abs(rel)
    nf = jnp.maximum(n, 1).astype(jnp.float32)
    large = max_exact + (jnp.log(nf / max_exact) / math.log(REL_MAX_DIST / max_exact)
                         * (nb - max_exact)).astype(jnp.int32)
    large = jnp.minimum(large, nb - 1)
    return jnp.where(rel > 0, nb, 0) + jnp.where(n < max_exact, n, large)


def _bias_kernel(table_ref, idx_ref, mask_ref, o_ref, *, minus_bucket):
    h = pl.program_id(0)
    idx = idx_ref[...]
    acc = mask_ref[...]
    if minus_bucket is not None:
        acc = acc - table_ref[minus_bucket, h]
    vals = [table_ref[b, h] for b in range(REL_BUCKETS)]
    bit = 1
    while len(vals) > 1:
        odd = (idx & bit) != 0
        vals = [jnp.where(odd, vals[2 * i + 1], vals[2 * i]) for i in range(len(vals) // 2)]
        bit *= 2
    o_ref[...] = (acc + vals[0]) * LOG2E


def _bias_tiles(rel_table, qpos, kpos, keys_major=False, minus_bucket=None):
    n_heads = rel_table.shape[1]
    if keys_major:
        qp, kp = qpos[:, None, :], kpos[:, :, None]
    else:
        qp, kp = qpos[:, :, None], kpos[:, None, :]
    idx = _t5_bucket(kp - qp).astype(jnp.int32)
    mask = jnp.where((kp // CHUNK) <= (qp // CHUNK), 0.0, NEG_INF).astype(F32)
    nt, r, c = idx.shape
    return pl.pallas_call(
        functools.partial(_bias_kernel, minus_bucket=minus_bucket),
        grid=(n_heads, nt),
        in_specs=[
            pl.BlockSpec(memory_space=pltpu.SMEM),
            pl.BlockSpec((None, r, c), lambda h, t: (t, 0, 0)),
            pl.BlockSpec((None, r, c), lambda h, t: (t, 0, 0)),
        ],
        out_specs=pl.BlockSpec((None, None, r, c), lambda h, t: (h, t, 0, 0)),
        out_shape=jax.ShapeDtypeStruct((n_heads, nt, r, c), F32),
        compiler_params=_params("arbitrary", "arbitrary"),
        name="bias_tiles",
    )(rel_table.astype(F32), idx, mask)


def _far_bucket(min_dist):
    nb = REL_BUCKETS // 2
    max_exact = nb // 2
    large = max_exact + math.log(min_dist / max_exact) / math.log(REL_MAX_DIST / max_exact) * (nb - max_exact)
    assert large >= nb - 1 + 0.5, "key tile too short for a constant far-field bias"
    return nb - 1


def _lambda_value(lam_ref, lam_init):
    a = lam_ref[...]
    s1 = jnp.sum(a[0:1] * a[1:2], axis=-1, keepdims=True)
    s2 = jnp.sum(a[2:3] * a[3:4], axis=-1, keepdims=True)
    return jnp.exp(s1) - jnp.exp(s2) + lam_init


def _prompt_attn_kernel(table_ref, lam_ref, subln_ref, qd_ref, kd_ref, vd_ref, qf_ref, kf_ref, vf_ref,
                        fcol_ref, bias_ref, cmask_ref, qdn_ref, qfn_ref, od_ref, of_ref,
                        kda_ref, kfa_ref, vdt_ref, vft_ref, qda_ref, qfa_ref,
                        md_ref, accd_ref, mf_ref, accf_ref,
                        sd_ref, pd_ref, ad_ref, sf_ref, pf_ref, af_ref,
                        *, tq, tk, lam_init, far_bucket):
    h = pl.program_id(1)
    qi = pl.program_id(2)
    seq = kd_ref.shape[0]
    ratio = tq // tk

    def prepare_keys_values():
        lane = lax.broadcasted_iota(jnp.int32, (seq, LANES), 1)
        kda_ref[:, :HEAD_DIM] = kd_ref[...]
        kda_ref[:, HEAD_DIM:] = jnp.where(lane < 3, 1.0, 0.0).astype(BF16)
        kfa_ref[:, :HEAD_DIM] = kf_ref[...]
        kfa_ref[:, HEAD_DIM:] = fcol_ref[...]

        ones = jnp.ones((ONES_ROWS, tk), BF16)

        def transpose_values(c, carry):
            r = pl.ds(pl.multiple_of(c * tk, tk), tk)
            vdt_ref[c] = jnp.concatenate([_transpose_bf16(vd_ref[r, :]), ones], axis=0)
            vft_ref[c] = jnp.concatenate([_transpose_bf16(vf_ref[r, :]), ones], axis=0)
            return carry

        lax.fori_loop(0, seq // tk, transpose_values, 0)

    def key_rows(ref, j):
        return ref[pl.ds(pl.multiple_of(j * tk, tk), tk), :]

    def prepare_queries(qd_src_ref, qf_src_ref):
        row = lax.broadcasted_iota(jnp.int32, (HEAD_DIM, tq), 0)
        qdt = qd_src_ref[...].astype(F32).T
        zero = jnp.zeros_like(qdt)
        top = jnp.concatenate([jnp.where(row < DIFF_QK_DIM, qdt, zero),
                               jnp.where(row >= DIFF_QK_DIM, qdt, zero)], axis=1).astype(BF16)
        row2 = lax.broadcasted_iota(jnp.int32, (HEAD_DIM, 2 * tq), 0)
        bottom = _split3_on_axis(jnp.full((HEAD_DIM, 2 * tq), table_ref[far_bucket, h] * LOG2E, F32), row2)
        qd_aug = jnp.concatenate([top, bottom], axis=0)
        qf_aug = jnp.concatenate([qf_src_ref[...].astype(F32).T.astype(BF16),
                                  jnp.where(row < 3, -1.0, 0.0).astype(BF16)], axis=0)
        qda_ref[...] = qd_aug
        qfa_ref[...] = qf_aug
        sd_ref[...] = _dot(key_rows(kda_ref, 0), qd_aug)
        sf_ref[...] = _dot(key_rows(kfa_ref, 0), qf_aug)

    @pl.when(qi == 0)
    def _():
        prepare_keys_values()
        prepare_queries(qd_ref, qf_ref)

    branches = ((kda_ref, qda_ref, vdt_ref, sd_ref, pd_ref, ad_ref, md_ref, accd_ref),
                (kfa_ref, qfa_ref, vft_ref, sf_ref, pf_ref, af_ref, mf_ref, accf_ref))

    def pipeline_step(j_prev, adds, j_next):
        for (k_ref, q_ref, vt_ref, s_ref, p_ref, a_ref, m_ref, acc_ref), add in zip(branches, adds):
            acc_ref[...] = a_ref[...] * acc_ref[...] + _dot(vt_ref[j_prev], p_ref[...])
            s = s_ref[...]
            if add is not None:
                s = s + add
            m_prev = m_ref[...]
            m_new = jnp.maximum(m_prev, jnp.max(s, axis=0, keepdims=True))
            p_ref[...] = jnp.exp2(s - m_new).astype(BF16)
            a_ref[...] = jnp.exp2(m_prev - m_new)
            m_ref[...] = m_new
            if j_next is not None:
                s_ref[...] = _dot(key_rows(k_ref, j_next), q_ref[...])

    for k_ref, q_ref, vt_ref, s_ref, p_ref, a_ref, m_ref, acc_ref in branches:
        m_ref[...] = jnp.full_like(m_ref, NEG_INF)
        acc_ref[...] = jnp.zeros_like(acc_ref)
        p_ref[...] = jnp.zeros_like(p_ref)
        a_ref[...] = jnp.ones_like(a_ref)

    def far_tiles(unroll):
        def body(i, carry):
            for u in range(unroll):
                j = i * unroll + u
                pipeline_step(jnp.maximum(j - 1, 0), (None, None), j + 1)
            return carry
        return body

    first_diag = qi * ratio
    n_far = jnp.maximum(first_diag - 1, 0)
    n_main = n_far // FAR_UNROLL
    lax.fori_loop(0, n_main, far_tiles(FAR_UNROLL), 0)
    lax.fori_loop(n_main * FAR_UNROLL, n_far, far_tiles(1), 0)

    def doubled(bias):
        return jnp.concatenate([bias, bias], axis=1)

    near = jnp.maximum(first_diag - 1, 0)
    gone = jnp.where(qi == 0, NEG_INF, 0.0)
    pipeline_step(jnp.maximum(near - 1, 0), (doubled(bias_ref[0]) + gone, gone), first_diag)
    for u in range(ratio):
        pipeline_step(near if u == 0 else first_diag + u - 1,
                      (doubled(bias_ref[u + 1]), cmask_ref[u]),
                      first_diag + u + 1 if u + 1 < ratio else None)
    last = first_diag + ratio - 1
    for k_ref, q_ref, vt_ref, s_ref, p_ref, a_ref, m_ref, acc_ref in branches:
        acc_ref[...] = a_ref[...] * acc_ref[...] + _dot(vt_ref[last], p_ref[...])

    prepare_queries(qdn_ref, qfn_ref)

    lam = _lambda_value(lam_ref, lam_init)
    acc = accd_ref[...]
    num, inv = acc[:HEAD_DIM], 1.0 / acc[HEAD_DIM:HEAD_DIM + 1]
    odt = num[:, :tq] * inv[:, :tq] - lam * (num[:, tq:] * inv[:, tq:])
    od_ref[...] = (_rms_rows(odt.T, subln_ref[...]) * (1.0 - lam_init)).astype(BF16)
    acc = accf_ref[...]
    of_ref[...] = (acc[:HEAD_DIM] * (1.0 / acc[HEAD_DIM:HEAD_DIM + 1])).T.astype(BF16)


def _prompt_attention(zb, fcols, rel_table, lam_params, subln, lam_init, batch, seq, n_heads, tq, tk):
    m = batch * seq
    dw = n_heads * HEAD_DIM
    nq = seq // tq
    ratio = tq // tk
    assert tq == ratio * tk
    far_bucket = _far_bucket(tk + 1)
    vrows = HEAD_DIM + ONES_ROWS
    qpos = tk + jnp.arange(tq, dtype=jnp.int32)
    kpos = jnp.arange((ratio + 1) * tk, dtype=jnp.int32).reshape(ratio + 1, tk)
    bias = _bias_tiles(rel_table, jnp.broadcast_to(qpos, (ratio + 1, tq)), kpos,
                       keys_major=True, minus_bucket=far_bucket)
    cmask = jnp.where(kpos[1:, :, None] <= qpos[None, None, :], 0.0, NEG_INF).astype(F32)

    def q_spec(seg, ahead=0):
        return pl.BlockSpec((tq, HEAD_DIM), lambda b, h, q, seg=seg: (b * nq + jnp.minimum(q + ahead, nq - 1),
                                                                      seg * n_heads + h))

    def kv_spec(seg):
        return pl.BlockSpec((seq, HEAD_DIM), lambda b, h, q, seg=seg: (b, seg * n_heads + h))

    out_spec = pl.BlockSpec((tq, HEAD_DIM), lambda b, h, q: (b * nq + q, h))
    const2 = lambda b, h, q: (0, 0)
    kern = functools.partial(_prompt_attn_kernel, tq=tq, tk=tk, lam_init=lam_init, far_bucket=far_bucket)
    return pl.pallas_call(
        kern,
        grid=(batch, n_heads, nq),
        in_specs=[
            pl.BlockSpec(memory_space=pltpu.SMEM),
            pl.BlockSpec((4, DIFF_QK_DIM), const2),
            pl.BlockSpec((1, HEAD_DIM), const2),
            q_spec(0), kv_spec(1), kv_spec(2), q_spec(3), kv_spec(4), kv_spec(5),
            pl.BlockSpec((None, None, seq, LANES), lambda b, h, q: (b, h, 0, 0)),
            pl.BlockSpec((None, ratio + 1, tk, tq), lambda b, h, q: (h, 0, 0, 0)),
            pl.BlockSpec((ratio, tk, tq), lambda b, h, q: (0, 0, 0)),
            q_spec(0, ahead=1), q_spec(3, ahead=1),
        ],
        out_specs=[out_spec, out_spec],
        out_shape=[jax.ShapeDtypeStruct((m, dw), BF16), jax.ShapeDtypeStruct((m, dw), BF16)],
        scratch_shapes=[
            pltpu.VMEM((seq, 2 * HEAD_DIM), BF16), pltpu.VMEM((seq, 2 * HEAD_DIM), BF16),
            pltpu.VMEM((seq // tk, vrows, tk), BF16), pltpu.VMEM((seq // tk, vrows, tk), BF16),
            pltpu.VMEM((2 * HEAD_DIM, 2 * tq), BF16), pltpu.VMEM((2 * HEAD_DIM, tq), BF16),
            pltpu.VMEM((1, 2 * tq), F32), pltpu.VMEM((vrows, 2 * tq), F32),
            pltpu.VMEM((1, tq), F32), pltpu.VMEM((vrows, tq), F32),
            pltpu.VMEM((tk, 2 * tq), F32), pltpu.VMEM((tk, 2 * tq), BF16), pltpu.VMEM((1, 2 * tq), F32),
            pltpu.VMEM((tk, tq), F32), pltpu.VMEM((tk, tq), BF16), pltpu.VMEM((1, tq), F32),
        ],
        compiler_params=_params("arbitrary", "arbitrary", "arbitrary"),
        name="prompt_attention",
    )(rel_table.astype(F32), lam_params, subln, zb, zb, zb, zb, zb, zb, fcols, bias, cmask, zb, zb)


def _split_diff_queries(q):
    lane = lax.broadcasted_iota(jnp.int32, q.shape, 1)
    zero = jnp.zeros_like(q)
    return jnp.concatenate([jnp.where(lane < DIFF_QK_DIM, q, zero),
                            jnp.where(lane >= DIFF_QK_DIM, q, zero)], axis=0)


def _sample_attn_kernel(lam_ref, subln_ref, z_ref, kdc_ref, vdc_ref, kfc_ref, vfc_ref, frow_ref, bias_ref,
                        cmask_ref, od_ref, of_ref, *, past, t_new, n_heads, lam_init):
    dw = n_heads * HEAD_DIM

    def cached(ref, h):
        return ref[pl.ds(h, past, stride=n_heads), :].astype(BF16)

    def new_rows(seg, h):
        lo = seg * dw + h * HEAD_DIM
        return z_ref[:, lo:lo + HEAD_DIM]

    def attend(q, kc, vc, kn, vn, add_c, add_n):
        sc = _dot_nt(q, kc) + add_c
        sn = _dot_nt(q, kn) + add_n
        mx = jnp.maximum(jnp.max(sc, axis=-1, keepdims=True), jnp.max(sn, axis=-1, keepdims=True))
        pc = jnp.exp2(sc - mx)
        pn = jnp.exp2(sn - mx)
        l = jnp.sum(pc, axis=-1, keepdims=True) + jnp.sum(pn, axis=-1, keepdims=True)
        acc = _dot(pc.astype(BF16), vc) + _dot(pn.astype(BF16), vn)
        return acc, l

    lam = _lambda_value(lam_ref, lam_init)
    cmask = cmask_ref[...]
    for h in range(n_heads):
        cols = slice(h * HEAD_DIM, (h + 1) * HEAD_DIM)
        qs = _split_diff_queries(new_rows(0, h))
        bias_c = bias_ref[h, :, :past]
        bias_n = bias_ref[h, :, past:past + t_new]
        acc, l = attend(qs, cached(kdc_ref, h), cached(vdc_ref, h),
                        new_rows(1, h), new_rows(2, h),
                        jnp.concatenate([bias_c, bias_c], axis=0), jnp.concatenate([bias_n, bias_n], axis=0))
        od = acc[:t_new] / l[:t_new] - lam * (acc[t_new:] / l[t_new:])
        od_ref[:, cols] = (_rms_rows(od, subln_ref[...]) * (1.0 - lam_init)).astype(BF16)

        f0 = frow_ref[h:h + 1, past:past + 1]
        dec_c = (f0 - frow_ref[h:h + 1, :past]) * LOG2E
        dec_n = (f0 - frow_ref[h:h + 1, past:past + t_new]) * LOG2E
        acc, l = attend(new_rows(3, h), cached(kfc_ref, h), cached(vfc_ref, h),
                        new_rows(4, h), new_rows(5, h), dec_c, dec_n + cmask)
        of_ref[:, cols] = (acc / l).astype(BF16)


def _sample_attention(zb, caches, layer, frow, rel_table, lam_params, subln, lam_init, nb, t_new, past, n_heads):
    dw = n_heads * HEAD_DIM
    tpad = frow.shape[-1]
    ar = jnp.arange(t_new, dtype=jnp.int32)
    bias = _bias_tiles(rel_table, (past + ar)[None], jnp.arange(tpad, dtype=jnp.int32)[None])[:, 0]
    cmask = jnp.where(ar[None, :] <= ar[:, None], 0.0, NEG_INF).astype(F32)
    caches = [c.reshape(c.shape[0], nb, past * n_heads, HEAD_DIM) for c in caches]
    cache_spec = pl.BlockSpec((None, None, past * n_heads, HEAD_DIM), lambda b: (layer, b, 0, 0))
    out_spec = pl.BlockSpec((t_new, dw), lambda b: (b, 0))
    const2 = lambda b: (0, 0)
    kern = functools.partial(_sample_attn_kernel, past=past, t_new=t_new, n_heads=n_heads, lam_init=lam_init)
    return pl.pallas_call(
        kern,
        grid=(nb,),
        in_specs=[
            pl.BlockSpec((4, DIFF_QK_DIM), const2),
            pl.BlockSpec((1, HEAD_DIM), const2),
            pl.BlockSpec((t_new, 6 * dw), lambda b: (b, 0)),
            cache_spec, cache_spec, cache_spec, cache_spec,
            pl.BlockSpec((None, n_heads, tpad), lambda b: (b, 0, 0)),
            pl.BlockSpec((n_heads, t_new, tpad), lambda b: (0, 0, 0)),
            pl.BlockSpec((t_new, t_new), const2),
        ],
        out_specs=[out_spec, out_spec],
        out_shape=[jax.ShapeDtypeStruct((nb * t_new, dw), BF16), jax.ShapeDtypeStruct((nb * t_new, dw), BF16)],
        compiler_params=_params("arbitrary"),
        name="sample_attention",
    )(lam_params, subln, zb, *caches, frow, bias, cmask)


def _merge_kernel(x_ref, g1_ref, od_ref, of_ref, wbd_ref, wbf_ref, wga_ref, wgb_ref, wout_ref, gp_ref,
                  o_ref, *, chunk):
    x = x_ref[...]
    xn = _rms_rows(x, g1_ref[...]).astype(BF16)
    od = od_ref[...]
    of = of_ref[...]
    o = None
    for c in range(x.shape[1] // chunk):
        cols = slice(c * chunk, (c + 1) * chunk)
        ga = jax.nn.sigmoid(_dot(xn, wga_ref[:, cols]))
        gb = jax.nn.sigmoid(_dot(xn, wgb_ref[:, cols]))
        u = (ga * _dot(od, wbd_ref[:, cols]) + gb * _dot(of, wbf_ref[:, cols])).astype(BF16)
        part = _dot(u, wout_ref[cols, :])
        o = part if o is None else o + part
    o_ref[...] = x + _rms_rows(o, gp_ref[...])


def _merge(x, g1, od, of, w_bd, w_bf, w_ga, w_gb, w_out, g_post, tm, chunk):
    m, d = x.shape
    dw = od.shape[1]
    row = lambda i: (i, 0)
    const2 = lambda i: (0, 0)
    resident = lambda shape: pl.BlockSpec(shape, const2, pipeline_mode=pl.Buffered(1))
    return pl.pallas_call(
        functools.partial(_merge_kernel, chunk=chunk),
        grid=(m // tm,),
        in_specs=[
            pl.BlockSpec((tm, d), row),
            pl.BlockSpec((1, d), const2),
            pl.BlockSpec((tm, dw), row),
            pl.BlockSpec((tm, dw), row),
            resident((dw, d)), resident((dw, d)), resident((d, d)), resident((d, d)), resident((d, d)),
            pl.BlockSpec((1, d), const2),
        ],
        out_specs=pl.BlockSpec((tm, d), row),
        out_shape=jax.ShapeDtypeStruct((m, d), F32),
        compiler_params=_params("arbitrary"),
        name="merge",
    )(x, g1, od, of, w_bd, w_bf, w_ga, w_gb, w_out, g_post)


def _gelu_tanh(x):
    k = -2.0 * math.sqrt(2.0 / math.pi) * LOG2E
    return x / (1.0 + jnp.exp2(x * (k + (k * 0.044715) * (x * x))))


def _ffn_kernel(*refs, seq, has_edges):
    if has_edges:
        (h_ref, g2_ref, wa_ref, wb_ref, cw_ref, cb_ref, wd_ref, gp_ref, prev_ref,
         o_ref, cs_ref, xn_ref, tail_ref) = refs
    else:
        (h_ref, g2_ref, wa_ref, wb_ref, cw_ref, cb_ref, wd_ref, gp_ref,
         o_ref, cs_ref, xn_ref, tail_ref) = refs
    i = pl.program_id(0)
    f = pl.program_id(1)
    tm = h_ref.shape[0]

    tf = wa_ref.shape[1]

    @pl.when(f == 0)
    def _():
        xn_ref[...] = _rms_rows(h_ref[...], g2_ref[...]).astype(BF16)
        o_ref[...] = jnp.zeros_like(o_ref)

    if not has_edges:
        @pl.when((i * tm) % seq == 0)
        def _():
            tail_ref[f] = jnp.zeros(tail_ref.shape[1:], F32)

    xn = xn_ref[...]

    def gated_chunk(cols):
        width = cols.stop - cols.start
        a = _dot(xn, wa_ref[:, cols])
        gate = _dot(xn, wb_ref[:, cols])
        back1 = pltpu.roll(a, 1, 0)
        back2 = pltpu.roll(a, 2, 0)
        if has_edges:
            shape3 = (tm // seq, seq, width)
            t = lax.broadcasted_iota(jnp.int32, shape3, 1)
            prev = prev_ref[:, :, cols]
            p0, p1 = prev[:, 0:1, :], prev[:, 1:2, :]
            am1 = jnp.where(t == 0, p1, back1.reshape(shape3)).reshape(tm, width)
            am2 = jnp.where(t == 0, p0, jnp.where(t == 1, p1, back2.reshape(shape3))).reshape(tm, width)
            cs_ref[:, :, cols] = a.reshape(shape3)[:, seq - (CONV_WIDTH - 1):, :]
        else:
            prev = tail_ref[f, :, cols]
            p0 = prev[SUBLANES - 2:SUBLANES - 1]
            p1 = prev[SUBLANES - 1:SUBLANES]
            top = lax.broadcasted_iota(jnp.int32, (SUBLANES, width), 0)
            am1 = jnp.concatenate([jnp.where(top == 0, p1, back1[:SUBLANES]), back1[SUBLANES:]], axis=0)
            am2 = jnp.concatenate([jnp.where(top == 0, p0, jnp.where(top == 1, p1, back2[:SUBLANES])),
                                   back2[SUBLANES:]], axis=0)
            tail_ref[f, :, cols] = a[tm - SUBLANES:, :]
            cs_ref[:, :, cols] = a[tm - (CONV_WIDTH - 1):, :][None]
        cw = cw_ref[:, cols]
        ac = cw[0:1] * am2 + cw[1:2] * am1 + cw[2:3] * a + cb_ref[:, cols]
        return (_gelu_tanh(ac) * gate).astype(BF16)

    width = tf // FFN_CHUNKS
    contrib = None
    for c in range(FFN_CHUNKS):
        cols = slice(c * width, (c + 1) * width)
        part = _dot(gated_chunk(cols), wd_ref[cols, :])
        contrib = part if contrib is None else contrib + part
    o_ref[...] += contrib

    @pl.when(f == pl.num_programs(1) - 1)
    def _():
        o_ref[...] = h_ref[...] + _rms_rows(o_ref[...], gp_ref[...])


def _ffn(h, g2, w_up, conv_w, conv_b, w_d, g_post, edges, seq, tm, tf):
    m, d = h.shape
    dff = w_d.shape[0]
    nf = dff // tf
    has_edges = edges is not None
    row = lambda i, f: (i, 0)
    col = lambda i, f: (0, f)
    const2 = lambda i, f: (0, 0)
    in_specs = [
        pl.BlockSpec((tm, d), row),
        pl.BlockSpec((1, d), const2),
        pl.BlockSpec((d, tf), col),
        pl.BlockSpec((d, tf), lambda i, f: (0, nf + f)),
        pl.BlockSpec((CONV_WIDTH, tf), col),
        pl.BlockSpec((1, tf), col),
        pl.BlockSpec((tf, d), lambda i, f: (f, 0)),
        pl.BlockSpec((1, d), const2),
    ]
    args = [h, g2, w_up, w_up, conv_w, conv_b, w_d, g_post]
    if has_edges:
        assert tm % seq == 0
        tails_per_tile = tm // seq
        in_specs.append(pl.BlockSpec((tails_per_tile, CONV_WIDTH - 1, tf), lambda i, f: (i, 0, f)))
        args.append(edges)
    else:
        assert seq % tm == 0
        tails_per_tile = 1
    n_tails = (m // tm) * tails_per_tile
    out, tails = pl.pallas_call(
        functools.partial(_ffn_kernel, seq=seq, has_edges=has_edges),
        grid=(m // tm, nf),
        in_specs=in_specs,
        out_specs=[pl.BlockSpec((tm, d), row),
                   pl.BlockSpec((tails_per_tile, CONV_WIDTH - 1, tf), lambda i, f: (i, 0, f))],
        out_shape=[jax.ShapeDtypeStruct((m, d), F32),
                   jax.ShapeDtypeStruct((n_tails, CONV_WIDTH - 1, dff), F32)],
        scratch_shapes=[pltpu.VMEM((tm, d), BF16), pltpu.VMEM((nf, SUBLANES, tf), F32)],
        compiler_params=_params("arbitrary", "arbitrary"),
        name="conv_ffn",
    )(*args)
    if not has_edges:
        tiles_per_seq = seq // tm
        tails = tails[tiles_per_seq - 1::tiles_per_seq]
    return out, tails


def _tile(n, cap):
    t = min(n, cap)
    assert n % t == 0
    return t


def _layer(layer, hp, hs, past, rel_table, lam_params, lam_init, p):
    (pre1, w_in, b_forget, subln, w_bd, w_bf, w_out, post1, pre2, w_up, conv_w, conv_b, w_down, post2) = p
    batch, seq, d = hp.shape
    nb, t_new, _ = hs.shape
    n_heads = d // (2 * HEAD_DIM)
    dw = n_heads * HEAD_DIM
    dff = w_down.shape[0]
    caches, plogf, conv_prev = past
    plen = plogf.shape[1]

    row2 = lambda v: v.reshape(1, -1).astype(F32)
    w_qkv = w_in[:, :6 * dw].astype(BF16)
    w_f = jnp.pad(w_in[:, 6 * dw:6 * dw + n_heads], ((0, 0), (0, LANES - n_heads))).astype(BF16)
    b_f = jnp.pad(b_forget.astype(F32), (0, LANES - n_heads)).reshape(1, LANES)
    w_ga = w_in[:, 6 * dw + n_heads:6 * dw + n_heads + d].astype(BF16)
    w_gb = w_in[:, 6 * dw + n_heads + d:].astype(BF16)
    w_bd_b, w_bf_b, w_out_b = w_bd.astype(BF16), w_bf.astype(BF16), w_out.astype(BF16)
    w_up_b = w_up.astype(BF16)
    w_d = w_down.astype(BF16)
    subln2 = row2(subln)

    def dense_tail(x2, od, of, edges, t_seq):
        m = x2.shape[0]
        h1 = _merge(x2, row2(pre1), od, of, w_bd_b, w_bf_b, w_ga, w_gb, w_out_b, row2(post1),
                    _tile(m, MERGE_ROWS), _tile(d, 512))
        return _ffn(h1, row2(pre2), w_up_b, conv_w.astype(F32), row2(conv_b), w_d, row2(post2),
                    edges, t_seq, _tile(m, 512), _tile(dff, 1024))

    heads = lambda a, n, t: a.reshape(n, t, n_heads, HEAD_DIM)

    xp = hp.reshape(batch * seq, d)
    zb, kd, vd, kf, vf, logf, flp = _inproj(xp, row2(pre1), w_qkv, w_f, b_f, n_heads,
                                            _tile(batch * seq, INPROJ_ROWS))
    fcols = _cumsum_keys(flp.reshape(batch, seq, LANES), n_heads, _tile(seq, 256))
    od, of = _prompt_attention(zb, fcols, rel_table, lam_params, subln2, lam_init,
                               batch, seq, n_heads, _tile(seq, ATTN_QUERY_TILE), _tile(seq, ATTN_KEY_TILE))
    hp_out, conv_p = dense_tail(xp, od, of, None, seq)
    state_p = (heads(kd, batch, seq), heads(vd, batch, seq), heads(kf, batch, seq), heads(vf, batch, seq),
               logf.reshape(batch, seq, n_heads), conv_p)

    xs = hs.reshape(nb * t_new, d)
    zb, kd, vd, kf, vf, logf, flp = _inproj(xs, row2(pre1), w_qkv, w_f, b_f, n_heads,
                                            _tile(nb * t_new, INPROJ_ROWS))
    tpad = -(-(plen + t_new) // LANES) * LANES
    flog = jnp.concatenate([jnp.swapaxes(plogf.astype(F32), 1, 2),
                            jnp.swapaxes(logf.reshape(nb, t_new, n_heads), 1, 2),
                            jnp.zeros((nb, n_heads, tpad - plen - t_new), F32)], axis=2)
    frow = _cumsum_rows(flog.reshape(nb * n_heads, tpad)).reshape(nb, n_heads, tpad)
    od, of = _sample_attention(zb, caches, layer, frow, rel_table, lam_params, subln2, lam_init,
                               nb, t_new, plen, n_heads)
    hs_out, conv_s = dense_tail(xs, od, of, conv_prev.astype(F32), t_new)
    state_s = (heads(kd, nb, t_new), heads(vd, nb, t_new), heads(kf, nb, t_new), heads(vf, nb, t_new),
               logf.reshape(nb, t_new, n_heads), conv_s)
    return hp_out.reshape(batch, seq, d), hs_out.reshape(nb, t_new, d), state_p, state_s


def kernel(x_prompt, x_sample, cache_diff_k, cache_diff_v, cache_fox_k, cache_fox_v, cache_fox_logf,
           state_ffn_conv, rel_table, pre_norm1, w_in, b_forget, lam_q1, lam_k1, lam_q2, lam_k2,
           diff_subln, w_branch_diff, w_branch_fox, w_out, post_norm1, pre_norm2, w_up, conv_w, conv_b,
           w_down, post_norm2):
    depth = w_in.shape[0]
    hp, hs = x_prompt, x_sample
    caches = (cache_diff_k, cache_diff_v, cache_fox_k, cache_fox_v)
    new_p, new_s = [], []
    for l in range(depth):
        lam_init = 0.8 - 0.6 * math.exp(-0.3 * l)
        lam_params = jnp.stack([lam_q1[l], lam_k1[l], lam_q2[l], lam_k2[l]]).astype(F32)
        params = (pre_norm1[l], w_in[l], b_forget[l], diff_subln[l], w_branch_diff[l], w_branch_fox[l],
                  w_out[l], post_norm1[l], pre_norm2[l], w_up[l], conv_w[l], conv_b[l], w_down[l],
                  post_norm2[l])
        past = (caches, cache_fox_logf[l], state_ffn_conv[l])
        hp, hs, sp, ss = _layer(l, hp, hs, past, rel_table, lam_params, lam_init, params)
        new_p.append(sp)
        new_s.append(ss)
    st = lambda lst, i: jnp.stack([e[i] for e in lst])
    return (hp, hs,
            st(new_p, 0), st(new_p, 1), st(new_p, 2), st(new_p, 3), st(new_p, 4), st(new_p, 5),
            st(new_s, 0), st(new_s, 1), st(new_s, 2), st(new_s, 3), st(new_s, 4), st(new_s, 5))
```

```python
import functools
import math

import jax
import jax.numpy as jnp
from jax import lax
from jax.experimental import pallas as pl
from jax.experimental.pallas import tpu as pltpu

HEAD_DIM = 128
DIFF_QK_DIM = HEAD_DIM // 2
CHUNK = 64
CONV_WIDTH = 3
REL_BUCKETS = 32
REL_MAX_DIST = 128
EPS = 1e-6
NEG_INF = -1e30
LOG2E = math.log2(math.e)

LANES = 128
SUBLANES = 8
ONES_ROWS = 16
ATTN_QUERY_TILE = 512
ATTN_KEY_TILE = 512
FAR_UNROLL = 2
FFN_CHUNKS = 2
MERGE_ROWS = 256
INPROJ_ROWS = 256
VMEM_LIMIT_BYTES = 60 * 1024 * 1024

F32 = jnp.float32
BF16 = jnp.bfloat16


def _params(*sem, flags=None):
    return pltpu.CompilerParams(dimension_semantics=sem, vmem_limit_bytes=VMEM_LIMIT_BYTES, flags=flags)


def _rms_rows(x, g):
    return x * lax.rsqrt(jnp.mean(x * x, axis=-1, keepdims=True) + EPS) * g


def _dot(a, b):
    return jnp.dot(a, b, preferred_element_type=F32)


def _dot_nt(a, b):
    return lax.dot_general(a, b, (((1,), (1,)), ((), ())), preferred_element_type=F32)


def _split3(x):
    hi = x.astype(BF16)
    r1 = x - hi.astype(F32)
    mid = r1.astype(BF16)
    lo = (r1 - mid.astype(F32)).astype(BF16)
    return hi, mid, lo


def _split3_on_axis(x, index):
    hi, mid, lo = (v.astype(F32) for v in _split3(x))
    return jnp.where(index == 0, hi, jnp.where(index == 1, mid, jnp.where(index == 2, lo, 0.0))).astype(BF16)


def _transpose_bf16(x):
    return x.astype(F32).T.astype(BF16)


def _inproj_kernel(x_ref, g_ref, w_ref, wf_ref, bf_ref,
                   zb_ref, kd_ref, vd_ref, kf_ref, vf_ref, logf_ref, flp_ref, *, n_heads, q_scales):
    dw = kd_ref.shape[1]
    xn = _rms_rows(x_ref[...], g_ref[...]).astype(BF16)
    fl = _dot(xn, wf_ref[...]) + bf_ref[...]
    lf = jnp.minimum(fl, 0.0) - jnp.log1p(jnp.exp(-jnp.abs(fl)))
    flp_ref[...] = lf
    logf_ref[...] = lf[:, :n_heads]
    f32_outs = {1: kd_ref, 2: vd_ref, 4: kf_ref, 5: vf_ref}
    for seg in range(6):
        cols = slice(seg * dw, (seg + 1) * dw)
        z = _dot(xn, w_ref[:, cols])
        if seg in f32_outs:
            f32_outs[seg][...] = z
        else:
            z = z * q_scales[seg]
        zb_ref[:, cols] = z.astype(BF16)


def _inproj(x, g, w_qkv, w_f, b_f, n_heads, tm):
    m, d = x.shape
    dw = w_qkv.shape[1] // 6
    row = lambda i: (i, 0)
    const2 = lambda i: (0, 0)
    f32_out = jax.ShapeDtypeStruct((m, dw), F32)
    q_scales = {0: DIFF_QK_DIM ** -0.5 * LOG2E, 3: HEAD_DIM ** -0.5 * LOG2E}
    return pl.pallas_call(
        functools.partial(_inproj_kernel, n_heads=n_heads, q_scales=q_scales),
        grid=(m // tm,),
        in_specs=[
            pl.BlockSpec((tm, d), row),
            pl.BlockSpec((1, d), const2),
            pl.BlockSpec((d, 6 * dw), const2, pipeline_mode=pl.Buffered(1)),
            pl.BlockSpec((d, LANES), const2),
            pl.BlockSpec((1, LANES), const2),
        ],
        out_specs=[
            pl.BlockSpec((tm, 6 * dw), row),
            pl.BlockSpec((tm, dw), row),
            pl.BlockSpec((tm, dw), row),
            pl.BlockSpec((tm, dw), row),
            pl.BlockSpec((tm, dw), row),
            pl.BlockSpec((tm, n_heads), row),
            pl.BlockSpec((tm, LANES), row),
        ],
        out_shape=[
            jax.ShapeDtypeStruct((m, 6 * dw), BF16),
            f32_out, f32_out, f32_out, f32_out,
            jax.ShapeDtypeStruct((m, n_heads), F32),
            jax.ShapeDtypeStruct((m, LANES), F32),
        ],
        compiler_params=_params("arbitrary"),
        name="inproj",
    )(x, g, w_qkv, w_f, b_f)


def _triangle(n, lower):
    r = lax.broadcasted_iota(jnp.int32, (n, n), 0)
    c = lax.broadcasted_iota(jnp.int32, (n, n), 1)
    return ((r >= c) if lower else (r <= c)).astype(BF16)


def _cumsum_keys_kernel(x_ref, o_ref, *, chunk, n_heads):
    t = x_ref.shape[0]
    tri = _triangle(chunk, lower=True)
    lane = lax.broadcasted_iota(jnp.int32, (chunk, LANES), 1)
    carry = jnp.zeros((1, LANES), F32)
    for k in range(t // chunk):
        hi, mid, lo = _split3(x_ref[k * chunk:(k + 1) * chunk, :])
        local = _dot(tri, hi) + _dot(tri, mid) + _dot(tri, lo)
        f = local + carry
        carry = carry + local[chunk - 1:chunk, :]
        for h in range(n_heads):
            o_ref[h, k * chunk:(k + 1) * chunk, :] = _split3_on_axis(
                jnp.broadcast_to(f[:, h:h + 1] * LOG2E, (chunk, LANES)), lane)


def _cumsum_keys(flp, n_heads, chunk):
    nb, t, _ = flp.shape
    return pl.pallas_call(
        functools.partial(_cumsum_keys_kernel, chunk=chunk, n_heads=n_heads),
        grid=(nb,),
        in_specs=[pl.BlockSpec((None, t, LANES), lambda b: (b, 0, 0))],
        out_specs=pl.BlockSpec((None, n_heads, t, LANES), lambda b: (b, 0, 0, 0)),
        out_shape=jax.ShapeDtypeStruct((nb, n_heads, t, LANES), BF16),
        compiler_params=_params("arbitrary"),
        name="cumsum_keys",
    )(flp)


def _cumsum_rows_kernel(x_ref, o_ref):
    t = x_ref.shape[1]
    tri = _triangle(LANES, lower=False)
    carry = jnp.zeros((x_ref.shape[0], 1), F32)
    for k in range(t // LANES):
        cols = slice(k * LANES, (k + 1) * LANES)
        hi, mid, lo = _split3(x_ref[:, cols])
        local = _dot(hi, tri) + _dot(mid, tri) + _dot(lo, tri)
        o_ref[:, cols] = local + carry
        carry = carry + local[:, LANES - 1:LANES]


def _cumsum_rows(x):
    return pl.pallas_call(
        _cumsum_rows_kernel,
        out_shape=jax.ShapeDtypeStruct(x.shape, F32),
        compiler_params=_params(),
        name="cumsum_rows",
    )(x)


def _t5_bucket(rel):
    nb = REL_BUCKETS // 2
    max_exact = nb // 2
    n = jnp.abs(rel)
    nf = jnp.maximum(n, 1).astype(jnp.float32)
    large = max_exact + (jnp.log(nf / max_exact) / math.log(REL_MAX_DIST / max_exact)
                         * (nb - max_exact)).astype(jnp.int32)
    large = jnp.minimum(large, nb - 1)
    return jnp.where(rel > 0, nb, 0) + jnp.where(n < max_exact, n, large)


def _bias_kernel(table_ref, idx_ref, mask_ref, o_ref, *, minus_bucket):
    h = pl.program_id(1)
    idx = idx_ref[...]
    acc = mask_ref[...]
    if minus_bucket is not None:
        acc = acc - table_ref[minus_bucket, h]
    vals = [table_ref[b, h] for b in range(REL_BUCKETS)]
    bit = 1
    while len(vals) > 1:
        odd = (idx & bit) != 0
        vals = [jnp.where(odd, vals[2 * i + 1], vals[2 * i]) for i in range(len(vals) // 2)]
        bit *= 2
    o_ref[...] = (acc + vals[0]) * LOG2E


def _bias_tiles(rel_table, qpos, kpos, keys_major=False, minus_bucket=None):
    n_heads = rel_table.shape[1]
    if keys_major:
        qp, kp = qpos[:, None, :], kpos[:, :, None]
    else:
        qp, kp = qpos[:, :, None], kpos[:, None, :]
    idx = _t5_bucket(kp - qp).astype(jnp.int32)
    mask = jnp.where((kp // CHUNK) <= (qp // CHUNK), 0.0, NEG_INF).astype(F32)
    nt, r, c = idx.shape
    return pl.pallas_call(
        functools.partial(_bias_kernel, minus_bucket=minus_bucket),
        grid=(nt, n_heads),
        in_specs=[
            pl.BlockSpec(memory_space=pltpu.SMEM),
            pl.BlockSpec((None, r, c), lambda t, h: (t, 0, 0)),
            pl.BlockSpec((None, r, c), lambda t, h: (t, 0, 0)),
        ],
        out_specs=pl.BlockSpec((None, None, r, c), lambda t, h: (h, t, 0, 0)),
        out_shape=jax.ShapeDtypeStruct((n_heads, nt, r, c), F32),
        compiler_params=_params("arbitrary", "arbitrary"),
        name="bias_tiles",
    )(rel_table.astype(F32), idx, mask)


def _far_bucket(min_dist):
    nb = REL_BUCKETS // 2
    max_exact = nb // 2
    large = max_exact + math.log(min_dist / max_exact) / math.log(REL_MAX_DIST / max_exact) * (nb - max_exact)
    assert large >= nb - 1 + 0.5, "key tile too short for a constant far-field bias"
    return nb - 1


def _lambda_value(lam_ref, lam_init):
    a = lam_ref[...]
    s1 = jnp.sum(a[0:1] * a[1:2], axis=-1, keepdims=True)
    s2 = jnp.sum(a[2:3] * a[3:4], axis=-1, keepdims=True)
    return jnp.exp(s1) - jnp.exp(s2) + lam_init


def _prompt_attn_kernel(table_ref, lam_ref, subln_ref, qd_ref, kd_ref, vd_ref, qf_ref, kf_ref, vf_ref,
                        fcol_ref, bias_ref, cmask_ref, qdn_ref, qfn_ref, od_ref, of_ref,
                        kda_ref, kfa_ref, vdt_ref, vft_ref, qda_ref, qfa_ref,
                        md_ref, accd_ref, mf_ref, accf_ref,
                        sd_ref, pd_ref, ad_ref, sf_ref, pf_ref, af_ref, cd_ref, cf_ref,
                        *, tq, tk, lam_init, far_bucket):
    h = pl.program_id(1)
    qi = pl.program_id(2)
    seq = kd_ref.shape[0]
    ratio = tq // tk

    def prepare_keys_values():
        lane = lax.broadcasted_iota(jnp.int32, (seq, LANES), 1)
        kda_ref[:, :HEAD_DIM] = kd_ref[...]
        kda_ref[:, HEAD_DIM:] = jnp.where(lane < 3, 1.0, 0.0).astype(BF16)
        kfa_ref[:, :HEAD_DIM] = kf_ref[...]
        kfa_ref[:, HEAD_DIM:] = fcol_ref[...]

        ones = jnp.ones((ONES_ROWS, tk), BF16)

        def transpose_values(c, carry):
            r = pl.ds(pl.multiple_of(c * tk, tk), tk)
            vdt_ref[c] = jnp.concatenate([_transpose_bf16(vd_ref[r, :]), ones], axis=0)
            vft_ref[c] = jnp.concatenate([_transpose_bf16(vf_ref[r, :]), ones], axis=0)
            return carry

        lax.fori_loop(0, seq // tk, transpose_values, 0)

    def key_rows(ref, j):
        return ref[pl.ds(pl.multiple_of(j * tk, tk), tk), :]

    def buffer_scores(j, k_ref, q_aug, s_ref, c_ref):
        s = _dot(key_rows(k_ref, j), q_aug)
        s_ref[...] = s
        c_ref[...] = jnp.max(s, axis=0, keepdims=True)

    def prepare_queries(qd_src_ref, qf_src_ref):
        row = lax.broadcasted_iota(jnp.int32, (HEAD_DIM, tq), 0)
        qdt = qd_src_ref[...].astype(F32).T
        zero = jnp.zeros_like(qdt)
        top = jnp.concatenate([jnp.where(row < DIFF_QK_DIM, qdt, zero),
                               jnp.where(row >= DIFF_QK_DIM, qdt, zero)], axis=1).astype(BF16)
        row2 = lax.broadcasted_iota(jnp.int32, (HEAD_DIM, 2 * tq), 0)
        bottom = _split3_on_axis(jnp.full((HEAD_DIM, 2 * tq), table_ref[far_bucket, h] * LOG2E, F32), row2)
        qd_aug = jnp.concatenate([top, bottom], axis=0)
        qf_aug = jnp.concatenate([qf_src_ref[...].astype(F32).T.astype(BF16),
                                  jnp.where(row < 3, -1.0, 0.0).astype(BF16)], axis=0)
        qda_ref[...] = qd_aug
        qfa_ref[...] = qf_aug
        buffer_scores(0, kda_ref, qd_aug, sd_ref, cd_ref)
        buffer_scores(0, kfa_ref, qf_aug, sf_ref, cf_ref)

    @pl.when(qi == 0)
    def _():
        prepare_keys_values()
        prepare_queries(qd_ref, qf_ref)

    branches = ((kda_ref, qda_ref, vdt_ref, sd_ref, pd_ref, ad_ref, md_ref, accd_ref, cd_ref),
                (kfa_ref, qfa_ref, vft_ref, sf_ref, pf_ref, af_ref, mf_ref, accf_ref, cf_ref))

    def pipeline_step(j_prev, adds, j_next):
        for (k_ref, q_ref, vt_ref, s_ref, p_ref, a_ref, m_ref, acc_ref, c_ref), add in zip(branches, adds):
            acc_ref[...] = a_ref[...] * acc_ref[...] + _dot(vt_ref[j_prev], p_ref[...])
            s = s_ref[...]
            if add is None:
                s_max = c_ref[...]
            else:
                s = s + add
                s_max = jnp.max(s, axis=0, keepdims=True)
            m_prev = m_ref[...]
            m_new = jnp.maximum(m_prev, s_max)
            p_ref[...] = jnp.exp2(s - m_new).astype(BF16)
            a_ref[...] = jnp.exp2(m_prev - m_new)
            m_ref[...] = m_new
            if j_next is not None:
                buffer_scores(j_next, k_ref, q_ref[...], s_ref, c_ref)

    for k_ref, q_ref, vt_ref, s_ref, p_ref, a_ref, m_ref, acc_ref, c_ref in branches:
        m_ref[...] = jnp.full_like(m_ref, NEG_INF)
        acc_ref[...] = jnp.zeros_like(acc_ref)
        p_ref[...] = jnp.zeros_like(p_ref)
        a_ref[...] = jnp.ones_like(a_ref)

    def far_tiles(unroll):
        def body(i, carry):
            for u in range(unroll):
                j = i * unroll + u
                pipeline_step(jnp.maximum(j - 1, 0), (None, None), j + 1)
            return carry
        return body

    first_diag = qi * ratio
    n_far = jnp.maximum(first_diag - 1, 0)
    n_main = n_far // FAR_UNROLL
    lax.fori_loop(0, n_main, far_tiles(FAR_UNROLL), 0)
    lax.fori_loop(n_main * FAR_UNROLL, n_far, far_tiles(1), 0)

    def doubled(bias):
        return jnp.concatenate([bias, bias], axis=1)

    near = jnp.maximum(first_diag - 1, 0)
    gone = jnp.where(qi == 0, NEG_INF, 0.0)
    pipeline_step(jnp.maximum(near - 1, 0), (doubled(bias_ref[0]) + gone, gone), first_diag)
    for u in range(ratio):
        pipeline_step(near if u == 0 else first_diag + u - 1,
                      (doubled(bias_ref[u + 1]), cmask_ref[u]),
                      first_diag + u + 1 if u + 1 < ratio else None)
    last = first_diag + ratio - 1
    for k_ref, q_ref, vt_ref, s_ref, p_ref, a_ref, m_ref, acc_ref, c_ref in branches:
        acc_ref[...] = a_ref[...] * acc_ref[...] + _dot(vt_ref[last], p_ref[...])

    prepare_queries(qdn_ref, qfn_ref)

    lam = _lambda_value(lam_ref, lam_init)
    acc = accd_ref[...]
    num, inv = acc[:HEAD_DIM], 1.0 / acc[HEAD_DIM:HEAD_DIM + 1]
    odt = num[:, :tq] * inv[:, :tq] - lam * (num[:, tq:] * inv[:, tq:])
    od_ref[...] = (_rms_rows(odt.T, subln_ref[...]) * (1.0 - lam_init)).astype(BF16)
    acc = accf_ref[...]
    of_ref[...] = (acc[:HEAD_DIM] * (1.0 / acc[HEAD_DIM:HEAD_DIM + 1])).T.astype(BF16)


def _prompt_attention(zb, fcols, rel_table, lam_params, subln, lam_init, batch, seq, n_heads, tq, tk):
    m = batch * seq
    dw = n_heads * HEAD_DIM
    nq = seq // tq
    ratio = tq // tk
    assert tq == ratio * tk
    far_bucket = _far_bucket(tk + 1)
    vrows = HEAD_DIM + ONES_ROWS
    qpos = tk + jnp.arange(tq, dtype=jnp.int32)
    kpos = jnp.arange((ratio + 1) * tk, dtype=jnp.int32).reshape(ratio + 1, tk)
    bias = _bias_tiles(rel_table, jnp.broadcast_to(qpos, (ratio + 1, tq)), kpos,
                       keys_major=True, minus_bucket=far_bucket)
    cmask = jnp.where(kpos[1:, :, None] <= qpos[None, None, :], 0.0, NEG_INF).astype(F32)

    def q_spec(seg, ahead=0):
        return pl.BlockSpec((tq, HEAD_DIM), lambda b, h, q, seg=seg: (b * nq + jnp.minimum(q + ahead, nq - 1),
                                                                      seg * n_heads + h))

    def kv_spec(seg):
        return pl.BlockSpec((seq, HEAD_DIM), lambda b, h, q, seg=seg: (b, seg * n_heads + h))

    out_spec = pl.BlockSpec((tq, HEAD_DIM), lambda b, h, q: (b * nq + q, h))
    const2 = lambda b, h, q: (0, 0)
    kern = functools.partial(_prompt_attn_kernel, tq=tq, tk=tk, lam_init=lam_init, far_bucket=far_bucket)
    return pl.pallas_call(
        kern,
        grid=(batch, n_heads, nq),
        in_specs=[
            pl.BlockSpec(memory_space=pltpu.SMEM),
            pl.BlockSpec((4, DIFF_QK_DIM), const2),
            pl.BlockSpec((1, HEAD_DIM), const2),
            q_spec(0), kv_spec(1), kv_spec(2), q_spec(3), kv_spec(4), kv_spec(5),
            pl.BlockSpec((None, None, seq, LANES), lambda b, h, q: (b, h, 0, 0)),
            pl.BlockSpec((None, ratio + 1, tk, tq), lambda b, h, q: (h, 0, 0, 0)),
            pl.BlockSpec((ratio, tk, tq), lambda b, h, q: (0, 0, 0)),
            q_spec(0, ahead=1), q_spec(3, ahead=1),
        ],
        out_specs=[out_spec, out_spec],
        out_shape=[jax.ShapeDtypeStruct((m, dw), BF16), jax.ShapeDtypeStruct((m, dw), BF16)],
        scratch_shapes=[
            pltpu.VMEM((seq, 2 * HEAD_DIM), BF16), pltpu.VMEM((seq, 2 * HEAD_DIM), BF16),
            pltpu.VMEM((seq // tk, vrows, tk), BF16), pltpu.VMEM((seq // tk, vrows, tk), BF16),
            pltpu.VMEM((2 * HEAD_DIM, 2 * tq), BF16), pltpu.VMEM((2 * HEAD_DIM, tq), BF16),
            pltpu.VMEM((1, 2 * tq), F32), pltpu.VMEM((vrows, 2 * tq), F32),
            pltpu.VMEM((1, tq), F32), pltpu.VMEM((vrows, tq), F32),
            pltpu.VMEM((tk, 2 * tq), F32), pltpu.VMEM((tk, 2 * tq), BF16), pltpu.VMEM((1, 2 * tq), F32),
            pltpu.VMEM((tk, tq), F32), pltpu.VMEM((tk, tq), BF16), pltpu.VMEM((1, tq), F32),
            pltpu.VMEM((1, 2 * tq), F32), pltpu.VMEM((1, tq), F32),
        ],
        compiler_params=_params("arbitrary", "arbitrary", "arbitrary"),
        name="prompt_attention",
    )(rel_table.astype(F32), lam_params, subln, zb, zb, zb, zb, zb, zb, fcols, bias, cmask, zb, zb)


def _split_diff_queries(q):
    lane = lax.broadcasted_iota(jnp.int32, q.shape, 1)
    zero = jnp.zeros_like(q)
    return jnp.concatenate([jnp.where(lane < DIFF_QK_DIM, q, zero),
                            jnp.where(lane >= DIFF_QK_DIM, q, zero)], axis=0)


def _sample_attn_kernel(lam_ref, subln_ref, z_ref, kdc_ref, vdc_ref, kfc_ref, vfc_ref, frow_ref, bias_ref,
                        cmask_ref, od_ref, of_ref, *, past, t_new, n_heads, lam_init):
    dw = n_heads * HEAD_DIM

    def cached(ref, h):
        return ref[pl.ds(h, past, stride=n_heads), :].astype(BF16)

    def new_rows(seg, h):
        lo = seg * dw + h * HEAD_DIM
        return z_ref[:, lo:lo + HEAD_DIM]

    def attend(q, kc, vc, kn, vn, add_c, add_n):
        sc = _dot_nt(q, kc) + add_c
        sn = _dot_nt(q, kn) + add_n
        mx = jnp.maximum(jnp.max(sc, axis=-1, keepdims=True), jnp.max(sn, axis=-1, keepdims=True))
        pc = jnp.exp2(sc - mx)
        pn = jnp.exp2(sn - mx)
        l = jnp.sum(pc, axis=-1, keepdims=True) + jnp.sum(pn, axis=-1, keepdims=True)
        acc = _dot(pc.astype(BF16), vc) + _dot(pn.astype(BF16), vn)
        return acc, l

    lam = _lambda_value(lam_ref, lam_init)
    cmask = cmask_ref[...]
    for h in range(n_heads):
        cols = slice(h * HEAD_DIM, (h + 1) * HEAD_DIM)
        qs = _split_diff_queries(new_rows(0, h))
        bias_c = bias_ref[h, :, :past]
        bias_n = bias_ref[h, :, past:past + t_new]
        acc, l = attend(qs, cached(kdc_ref, h), cached(vdc_ref, h),
                        new_rows(1, h), new_rows(2, h),
                        jnp.concatenate([bias_c, bias_c], axis=0), jnp.concatenate([bias_n, bias_n], axis=0))
        od = acc[:t_new] / l[:t_new] - lam * (acc[t_new:] / l[t_new:])
        od_ref[:, cols] = (_rms_rows(od, subln_ref[...]) * (1.0 - lam_init)).astype(BF16)

        f0 = frow_ref[h:h + 1, past:past + 1]
        dec_c = (f0 - frow_ref[h:h + 1, :past]) * LOG2E
        dec_n = (f0 - frow_ref[h:h + 1, past:past + t_new]) * LOG2E
        acc, l = attend(new_rows(3, h), cached(kfc_ref, h), cached(vfc_ref, h),
                        new_rows(4, h), new_rows(5, h), dec_c, dec_n + cmask)
        of_ref[:, cols] = (acc / l).astype(BF16)


def _sample_attention(zb, caches, layer, frow, rel_table, lam_params, subln, lam_init, nb, t_new, past, n_heads):
    dw = n_heads * HEAD_DIM
    tpad = frow.shape[-1]
    ar = jnp.arange(t_new, dtype=jnp.int32)
    bias = _bias_tiles(rel_table, (past + ar)[None], jnp.arange(tpad, dtype=jnp.int32)[None])[:, 0]
    cmask = jnp.where(ar[None, :] <= ar[:, None], 0.0, NEG_INF).astype(F32)
    caches = [c.reshape(c.shape[0], nb, past * n_heads, HEAD_DIM) for c in caches]
    cache_spec = pl.BlockSpec((None, None, past * n_heads, HEAD_DIM), lambda b: (layer, b, 0, 0))
    out_spec = pl.BlockSpec((t_new, dw), lambda b: (b, 0))
    const2 = lambda b: (0, 0)
    kern = functools.partial(_sample_attn_kernel, past=past, t_new=t_new, n_heads=n_heads, lam_init=lam_init)
    return pl.pallas_call(
        kern,
        grid=(nb,),
        in_specs=[
            pl.BlockSpec((4, DIFF_QK_DIM), const2),
            pl.BlockSpec((1, HEAD_DIM), const2),
            pl.BlockSpec((t_new, 6 * dw), lambda b: (b, 0)),
            cache_spec, cache_spec, cache_spec, cache_spec,
            pl.BlockSpec((None, n_heads, tpad), lambda b: (b, 0, 0)),
            pl.BlockSpec((n_heads, t_new, tpad), lambda b: (0, 0, 0)),
            pl.BlockSpec((t_new, t_new), const2),
        ],
        out_specs=[out_spec, out_spec],
        out_shape=[jax.ShapeDtypeStruct((nb * t_new, dw), BF16), jax.ShapeDtypeStruct((nb * t_new, dw), BF16)],
        compiler_params=_params("arbitrary"),
        name="sample_attention",
    )(lam_params, subln, zb, *caches, frow, bias, cmask)


def _merge_kernel(x_ref, g1_ref, od_ref, of_ref, wbd_ref, wbf_ref, wga_ref, wgb_ref, wout_ref, gp_ref,
                  o_ref, *, chunk):
    x = x_ref[...]
    xn = _rms_rows(x, g1_ref[...]).astype(BF16)
    od = od_ref[...]
    of = of_ref[...]
    o = None
    for c in range(x.shape[1] // chunk):
        cols = slice(c * chunk, (c + 1) * chunk)
        ga = jax.nn.sigmoid(_dot(xn, wga_ref[:, cols]))
        gb = jax.nn.sigmoid(_dot(xn, wgb_ref[:, cols]))
        u = (ga * _dot(od, wbd_ref[:, cols]) + gb * _dot(of, wbf_ref[:, cols])).astype(BF16)
        part = _dot(u, wout_ref[cols, :])
        o = part if o is None else o + part
    o_ref[...] = x + _rms_rows(o, gp_ref[...])


def _merge(x, g1, od, of, w_bd, w_bf, w_ga, w_gb, w_out, g_post, tm, chunk):
    m, d = x.shape
    dw = od.shape[1]
    row = lambda i: (i, 0)
    const2 = lambda i: (0, 0)
    resident = lambda shape: pl.BlockSpec(shape, const2, pipeline_mode=pl.Buffered(1))
    return pl.pallas_call(
        functools.partial(_merge_kernel, chunk=chunk),
        grid=(m // tm,),
        in_specs=[
            pl.BlockSpec((tm, d), row),
            pl.BlockSpec((1, d), const2),
            pl.BlockSpec((tm, dw), row),
            pl.BlockSpec((tm, dw), row),
            resident((dw, d)), resident((dw, d)), resident((d, d)), resident((d, d)), resident((d, d)),
            pl.BlockSpec((1, d), const2),
        ],
        out_specs=pl.BlockSpec((tm, d), row),
        out_shape=jax.ShapeDtypeStruct((m, d), F32),
        compiler_params=_params("arbitrary"),
        name="merge",
    )(x, g1, od, of, w_bd, w_bf, w_ga, w_gb, w_out, g_post)


def _gelu_tanh(x):
    k = -2.0 * math.sqrt(2.0 / math.pi) * LOG2E
    return x / (1.0 + jnp.exp2(x * (k + (k * 0.044715) * (x * x))))


def _ffn_kernel(*refs, seq, has_edges):
    if has_edges:
        (h_ref, g2_ref, wa_ref, wb_ref, cw_ref, cb_ref, wd_ref, gp_ref, prev_ref,
         o_ref, cs_ref, xn_ref, tail_ref) = refs
    else:
        (h_ref, g2_ref, wa_ref, wb_ref, cw_ref, cb_ref, wd_ref, gp_ref,
         o_ref, cs_ref, xn_ref, tail_ref) = refs
    i = pl.program_id(0)
    f = pl.program_id(1)
    tm = h_ref.shape[0]

    tf = wa_ref.shape[1]

    @pl.when(f == 0)
    def _():
        xn_ref[...] = _rms_rows(h_ref[...], g2_ref[...]).astype(BF16)
        o_ref[...] = jnp.zeros_like(o_ref)

    if not has_edges:
        @pl.when((i * tm) % seq == 0)
        def _():
            tail_ref[f] = jnp.zeros(tail_ref.shape[1:], F32)

    xn = xn_ref[...]

    def gated_chunk(cols):
        width = cols.stop - cols.start
        a = _dot(xn, wa_ref[:, cols])
        gate = _dot(xn, wb_ref[:, cols])
        back1 = pltpu.roll(a, 1, 0)
        back2 = pltpu.roll(a, 2, 0)
        if has_edges:
            shape3 = (tm // seq, seq, width)
            t = lax.broadcasted_iota(jnp.int32, shape3, 1)
            prev = prev_ref[:, :, cols]
            p0, p1 = prev[:, 0:1, :], prev[:, 1:2, :]
            am1 = jnp.where(t == 0, p1, back1.reshape(shape3)).reshape(tm, width)
            am2 = jnp.where(t == 0, p0, jnp.where(t == 1, p1, back2.reshape(shape3))).reshape(tm, width)
            cs_ref[:, :, cols] = a.reshape(shape3)[:, seq - (CONV_WIDTH - 1):, :]
        else:
            prev = tail_ref[f, :, cols]
            p0 = prev[SUBLANES - 2:SUBLANES - 1]
            p1 = prev[SUBLANES - 1:SUBLANES]
            top = lax.broadcasted_iota(jnp.int32, (SUBLANES, width), 0)
            am1 = jnp.concatenate([jnp.where(top == 0, p1, back1[:SUBLANES]), back1[SUBLANES:]], axis=0)
            am2 = jnp.concatenate([jnp.where(top == 0, p0, jnp.where(top == 1, p1, back2[:SUBLANES])),
                                   back2[SUBLANES:]], axis=0)
            tail_ref[f, :, cols] = a[tm - SUBLANES:, :]
            cs_ref[:, :, cols] = a[tm - (CONV_WIDTH - 1):, :][None]
        cw = cw_ref[:, cols]
        ac = cw[0:1] * am2 + cw[1:2] * am1 + cw[2:3] * a + cb_ref[:, cols]
        return (_gelu_tanh(ac) * gate).astype(BF16)

    width = tf // FFN_CHUNKS
    contrib = None
    for c in range(FFN_CHUNKS):
        cols = slice(c * width, (c + 1) * width)
        part = _dot(gated_chunk(cols), wd_ref[cols, :])
        contrib = part if contrib is None else contrib + part
    o_ref[...] += contrib

    @pl.when(f == pl.num_programs(1) - 1)
    def _():
        o_ref[...] = h_ref[...] + _rms_rows(o_ref[...], gp_ref[...])


def _ffn(h, g2, w_up, conv_w, conv_b, w_d, g_post, edges, seq, tm, tf):
    m, d = h.shape
    dff = w_d.shape[0]
    nf = dff // tf
    has_edges = edges is not None
    row = lambda i, f: (i, 0)
    col = lambda i, f: (0, f)
    const2 = lambda i, f: (0, 0)
    in_specs = [
        pl.BlockSpec((tm, d), row),
        pl.BlockSpec((1, d), const2),
        pl.BlockSpec((d, tf), col),
        pl.BlockSpec((d, tf), lambda i, f: (0, nf + f)),
        pl.BlockSpec((CONV_WIDTH, tf), col),
        pl.BlockSpec((1, tf), col),
        pl.BlockSpec((tf, d), lambda i, f: (f, 0)),
        pl.BlockSpec((1, d), const2),
    ]
    args = [h, g2, w_up, w_up, conv_w, conv_b, w_d, g_post]
    if has_edges:
        assert tm % seq == 0
        tails_per_tile = tm // seq
        in_specs.append(pl.BlockSpec((tails_per_tile, CONV_WIDTH - 1, tf), lambda i, f: (i, 0, f)))
        args.append(edges)
    else:
        assert seq % tm == 0
        tails_per_tile = 1
    n_tails = (m // tm) * tails_per_tile
    out, tails = pl.pallas_call(
        functools.partial(_ffn_kernel, seq=seq, has_edges=has_edges),
        grid=(m // tm, nf),
        in_specs=in_specs,
        out_specs=[pl.BlockSpec((tm, d), row),
                   pl.BlockSpec((tails_per_tile, CONV_WIDTH - 1, tf), lambda i, f: (i, 0, f))],
        out_shape=[jax.ShapeDtypeStruct((m, d), F32),
                   jax.ShapeDtypeStruct((n_tails, CONV_WIDTH - 1, dff), F32)],
        scratch_shapes=[pltpu.VMEM((tm, d), BF16), pltpu.VMEM((nf, SUBLANES, tf), F32)],
        compiler_params=_params("arbitrary", "arbitrary"),
        name="conv_ffn",
    )(*args)
    if not has_edges:
        tiles_per_seq = seq // tm
        tails = tails[tiles_per_seq - 1::tiles_per_seq]
    return out, tails


def _tile(n, cap):
    t = min(n, cap)
    assert n % t == 0
    return t


def _layer(layer, hp, hs, past, rel_table, lam_params, lam_init, p):
    (pre1, w_in, b_forget, subln, w_bd, w_bf, w_out, post1, pre2, w_up, conv_w, conv_b, w_down, post2) = p
    batch, seq, d = hp.shape
    nb, t_new, _ = hs.shape
    n_heads = d // (2 * HEAD_DIM)
    dw = n_heads * HEAD_DIM
    dff = w_down.shape[0]
    caches, plogf, conv_prev = past
    plen = plogf.shape[1]

    row2 = lambda v: v.reshape(1, -1).astype(F32)
    w_qkv = w_in[:, :6 * dw].astype(BF16)
    w_f = jnp.pad(w_in[:, 6 * dw:6 * dw + n_heads], ((0, 0), (0, LANES - n_heads))).astype(BF16)
    b_f = jnp.pad(b_forget.astype(F32), (0, LANES - n_heads)).reshape(1, LANES)
    w_ga = w_in[:, 6 * dw + n_heads:6 * dw + n_heads + d].astype(BF16)
    w_gb = w_in[:, 6 * dw + n_heads + d:].astype(BF16)
    w_bd_b, w_bf_b, w_out_b = w_bd.astype(BF16), w_bf.astype(BF16), w_out.astype(BF16)
    w_up_b = w_up.astype(BF16)
    w_d = w_down.astype(BF16)
    subln2 = row2(subln)

    def dense_tail(x2, od, of, edges, t_seq):
        m = x2.shape[0]
        h1 = _merge(x2, row2(pre1), od, of, w_bd_b, w_bf_b, w_ga, w_gb, w_out_b, row2(post1),
                    _tile(m, MERGE_ROWS), _tile(d, 512))
        return _ffn(h1, row2(pre2), w_up_b, conv_w.astype(F32), row2(conv_b), w_d, row2(post2),
                    edges, t_seq, _tile(m, 512), _tile(dff, 1024))

    heads = lambda a, n, t: a.reshape(n, t, n_heads, HEAD_DIM)

    xp = hp.reshape(batch * seq, d)
    zb, kd, vd, kf, vf, logf, flp = _inproj(xp, row2(pre1), w_qkv, w_f, b_f, n_heads,
                                            _tile(batch * seq, INPROJ_ROWS))
    fcols = _cumsum_keys(flp.reshape(batch, seq, LANES), n_heads, _tile(seq, 256))
    od, of = _prompt_attention(zb, fcols, rel_table, lam_params, subln2, lam_init,
                               batch, seq, n_heads, _tile(seq, ATTN_QUERY_TILE), _tile(seq, ATTN_KEY_TILE))
    hp_out, conv_p = dense_tail(xp, od, of, None, seq)
    state_p = (heads(kd, batch, seq), heads(vd, batch, seq), heads(kf, batch, seq), heads(vf, batch, seq),
               logf.reshape(batch, seq, n_heads), conv_p)

    xs = hs.reshape(nb * t_new, d)
    zb, kd, vd, kf, vf, logf, flp = _inproj(xs, row2(pre1), w_qkv, w_f, b_f, n_heads,
                                            _tile(nb * t_new, INPROJ_ROWS))
    tpad = -(-(plen + t_new) // LANES) * LANES
    flog = jnp.concatenate([jnp.swapaxes(plogf.astype(F32), 1, 2),
                            jnp.swapaxes(logf.reshape(nb, t_new, n_heads), 1, 2),
                            jnp.zeros((nb, n_heads, tpad - plen - t_new), F32)], axis=2)
    frow = _cumsum_rows(flog.reshape(nb * n_heads, tpad)).reshape(nb, n_heads, tpad)
    od, of = _sample_attention(zb, caches, layer, frow, rel_table, lam_params, subln2, lam_init,
                               nb, t_new, plen, n_heads)
    hs_out, conv_s = dense_tail(xs, od, of, conv_prev.astype(F32), t_new)
    state_s = (heads(kd, nb, t_new), heads(vd, nb, t_new), heads(kf, nb, t_new), heads(vf, nb, t_new),
               logf.reshape(nb, t_new, n_heads), conv_s)
    return hp_out.reshape(batch, seq, d), hs_out.reshape(nb, t_new, d), state_p, state_s


def kernel(x_prompt, x_sample, cache_diff_k, cache_diff_v, cache_fox_k, cache_fox_v, cache_fox_logf,
           state_ffn_conv, rel_table, pre_norm1, w_in, b_forget, lam_q1, lam_k1, lam_q2, lam_k2,
           diff_subln, w_branch_diff, w_branch_fox, w_out, post_norm1, pre_norm2, w_up, conv_w, conv_b,
           w_down, post_norm2):
    depth = w_in.shape[0]
    hp, hs = x_prompt, x_sample
    caches = (cache_diff_k, cache_diff_v, cache_fox_k, cache_fox_v)
    new_p, new_s = [], []
    for l in range(depth):
        lam_init = 0.8 - 0.6 * math.exp(-0.3 * l)
        lam_params = jnp.stack([lam_q1[l], lam_k1[l], lam_q2[l], lam_k2[l]]).astype(F32)
        params = (pre_norm1[l], w_in[l], b_forget[l], diff_subln[l], w_branch_diff[l], w_branch_fox[l],
                  w_out[l], post_norm1[l], pre_norm2[l], w_up[l], conv_w[l], conv_b[l], w_down[l],
                  post_norm2[l])
        past = (caches, cache_fox_logf[l], state_ffn_conv[l])
        hp, hs, sp, ss = _layer(l, hp, hs, past, rel_table, lam_params, lam_init, params)
        new_p.append(sp)
        new_s.append(ss)
    st = lambda lst, i: jnp.stack([e[i] for e in lst])
    return (hp, hs,
            st(new_p, 0), st(new_p, 1), st(new_p, 2), st(new_p, 3), st(new_p, 4), st(new_p, 5),
            st(new_s, 0), st(new_s, 1), st(new_s, 2), st(new_s, 3), st(new_s, 4), st(new_s, 5))
```

```python
import functools
import math

import jax
import jax.numpy as jnp
from jax import lax
from jax.experimental import pallas as pl
from jax.experimental.pallas import tpu as pltpu

HEAD_DIM = 128
DIFF_QK_DIM = HEAD_DIM // 2
CHUNK = 64
CONV_WIDTH = 3
REL_BUCKETS = 32
REL_MAX_DIST = 128
EPS = 1e-6
NEG_INF = -1e30
LOG2E = math.log2(math.e)

LANES = 128
SUBLANES = 8
ONES_ROWS = 16
ATTN_QUERY_TILE = 512
ATTN_KEY_TILE = 512
FAR_UNROLL = 2
FFN_CHUNKS = 2
MERGE_ROWS = 256
INPROJ_ROWS = 256
VMEM_LIMIT_BYTES = 60 * 1024 * 1024

F32 = jnp.float32
BF16 = jnp.bfloat16


def _params(*sem, flags=None):
    return pltpu.CompilerParams(dimension_semantics=sem, vmem_limit_bytes=VMEM_LIMIT_BYTES, flags=flags)


def _rms_rows(x, g):
    return x * lax.rsqrt(jnp.mean(x * x, axis=-1, keepdims=True) + EPS) * g


def _dot(a, b):
    return jnp.dot(a, b, preferred_element_type=F32)


def _dot_nt(a, b):
    return lax.dot_general(a, b, (((1,), (1,)), ((), ())), preferred_element_type=F32)


def _split3(x):
    hi = x.astype(BF16)
    r1 = x - hi.astype(F32)
    mid = r1.astype(BF16)
    lo = (r1 - mid.astype(F32)).astype(BF16)
    return hi, mid, lo


def _split3_on_axis(x, index):
    hi, mid, lo = (v.astype(F32) for v in _split3(x))
    return jnp.where(index == 0, hi, jnp.where(index == 1, mid, jnp.where(index == 2, lo, 0.0))).astype(BF16)


def _transpose_bf16(x):
    return x.astype(F32).T.astype(BF16)


def _inproj_kernel(x_ref, g_ref, w_ref, wf_ref, bf_ref,
                   zb_ref, kd_ref, vd_ref, kf_ref, vf_ref, logf_ref, flp_ref, *, n_heads, q_scales):
    dw = kd_ref.shape[1]
    xn = _rms_rows(x_ref[...], g_ref[...]).astype(BF16)
    fl = _dot(xn, wf_ref[...]) + bf_ref[...]
    lf = jnp.minimum(fl, 0.0) - jnp.log1p(jnp.exp(-jnp.abs(fl)))
    flp_ref[...] = lf
    logf_ref[...] = lf[:, :n_heads]
    f32_outs = {1: kd_ref, 2: vd_ref, 4: kf_ref, 5: vf_ref}
    for seg in range(6):
        cols = slice(seg * dw, (seg + 1) * dw)
        z = _dot(xn, w_ref[:, cols])
        if seg in f32_outs:
            f32_outs[seg][...] = z
        else:
            z = z * q_scales[seg]
        zb_ref[:, cols] = z.astype(BF16)


def _inproj(x, g, w_in, b_f, n_heads, tm):
    m, d = x.shape
    dw = n_heads * HEAD_DIM
    assert (6 * dw) % LANES == 0
    row = lambda i: (i, 0)
    const2 = lambda i: (0, 0)
    f32_out = jax.ShapeDtypeStruct((m, dw), F32)
    q_scales = {0: DIFF_QK_DIM ** -0.5 * LOG2E, 3: HEAD_DIM ** -0.5 * LOG2E}
    return pl.pallas_call(
        functools.partial(_inproj_kernel, n_heads=n_heads, q_scales=q_scales),
        grid=(m // tm,),
        in_specs=[
            pl.BlockSpec((tm, d), row),
            pl.BlockSpec((1, d), const2),
            pl.BlockSpec((d, 6 * dw), const2, pipeline_mode=pl.Buffered(1)),
            pl.BlockSpec((d, LANES), lambda i: (0, 6 * dw // LANES)),
            pl.BlockSpec((1, LANES), const2),
        ],
        out_specs=[
            pl.BlockSpec((tm, 6 * dw), row),
            pl.BlockSpec((tm, dw), row),
            pl.BlockSpec((tm, dw), row),
            pl.BlockSpec((tm, dw), row),
            pl.BlockSpec((tm, dw), row),
            pl.BlockSpec((tm, n_heads), row),
            pl.BlockSpec((tm, LANES), row),
        ],
        out_shape=[
            jax.ShapeDtypeStruct((m, 6 * dw), BF16),
            f32_out, f32_out, f32_out, f32_out,
            jax.ShapeDtypeStruct((m, n_heads), F32),
            jax.ShapeDtypeStruct((m, LANES), F32),
        ],
        compiler_params=_params("arbitrary"),
        name="inproj",
    )(x, g, w_in, w_in, b_f)


def _triangle(n, lower):
    r = lax.broadcasted_iota(jnp.int32, (n, n), 0)
    c = lax.broadcasted_iota(jnp.int32, (n, n), 1)
    return ((r >= c) if lower else (r <= c)).astype(BF16)


def _cumsum_keys_kernel(x_ref, o_ref, *, chunk, n_heads):
    t = x_ref.shape[0]
    tri = _triangle(chunk, lower=True)
    lane = lax.broadcasted_iota(jnp.int32, (chunk, LANES), 1)
    carry = jnp.zeros((1, LANES), F32)
    for k in range(t // chunk):
        hi, mid, lo = _split3(x_ref[k * chunk:(k + 1) * chunk, :])
        local = _dot(tri, hi) + _dot(tri, mid) + _dot(tri, lo)
        f = local + carry
        carry = carry + local[chunk - 1:chunk, :]
        for h in range(n_heads):
            o_ref[h, k * chunk:(k + 1) * chunk, :] = _split3_on_axis(
                jnp.broadcast_to(f[:, h:h + 1] * LOG2E, (chunk, LANES)), lane)


def _cumsum_keys(flp, n_heads, chunk):
    nb, t, _ = flp.shape
    return pl.pallas_call(
        functools.partial(_cumsum_keys_kernel, chunk=chunk, n_heads=n_heads),
        grid=(nb,),
        in_specs=[pl.BlockSpec((None, t, LANES), lambda b: (b, 0, 0))],
        out_specs=pl.BlockSpec((None, n_heads, t, LANES), lambda b: (b, 0, 0, 0)),
        out_shape=jax.ShapeDtypeStruct((nb, n_heads, t, LANES), BF16),
        compiler_params=_params("arbitrary"),
        name="cumsum_keys",
    )(flp)


def _cumsum_rows_kernel(x_ref, o_ref):
    t = x_ref.shape[1]
    tri = _triangle(LANES, lower=False)
    carry = jnp.zeros((x_ref.shape[0], 1), F32)
    for k in range(t // LANES):
        cols = slice(k * LANES, (k + 1) * LANES)
        hi, mid, lo = _split3(x_ref[:, cols])
        local = _dot(hi, tri) + _dot(mid, tri) + _dot(lo, tri)
        o_ref[:, cols] = local + carry
        carry = carry + local[:, LANES - 1:LANES]


def _cumsum_rows(x):
    return pl.pallas_call(
        _cumsum_rows_kernel,
        out_shape=jax.ShapeDtypeStruct(x.shape, F32),
        compiler_params=_params(),
        name="cumsum_rows",
    )(x)


def _t5_bucket(rel):
    nb = REL_BUCKETS // 2
    max_exact = nb // 2
    n = jnp.abs(rel)
    nf = jnp.maximum(n, 1).astype(jnp.float32)
    large = max_exact + (jnp.log(nf / max_exact) / math.log(REL_MAX_DIST / max_exact)
                         * (nb - max_exact)).astype(jnp.int32)
    large = jnp.minimum(large, nb - 1)
    return jnp.where(rel > 0, nb, 0) + jnp.where(n < max_exact, n, large)


def _bias_kernel(table_ref, idx_ref, mask_ref, o_ref, *, minus_bucket):
    h = pl.program_id(1)
    idx = idx_ref[...]
    acc = mask_ref[...]
    if minus_bucket is not None:
        acc = acc - table_ref[minus_bucket, h]
    vals = [table_ref[b, h] for b in range(REL_BUCKETS)]
    bit = 1
    while len(vals) > 1:
        odd = (idx & bit) != 0
        vals = [jnp.where(odd, vals[2 * i + 1], vals[2 * i]) for i in range(len(vals) // 2)]
        bit *= 2
    o_ref[...] = (acc + vals[0]) * LOG2E


def _bias_tiles(rel_table, qpos, kpos, keys_major=False, minus_bucket=None):
    n_heads = rel_table.shape[1]
    if keys_major:
        qp, kp = qpos[:, None, :], kpos[:, :, None]
    else:
        qp, kp = qpos[:, :, None], kpos[:, None, :]
    idx = _t5_bucket(kp - qp).astype(jnp.int32)
    mask = jnp.where((kp // CHUNK) <= (qp // CHUNK), 0.0, NEG_INF).astype(F32)
    nt, r, c = idx.shape
    return pl.pallas_call(
        functools.partial(_bias_kernel, minus_bucket=minus_bucket),
        grid=(nt, n_heads),
        in_specs=[
            pl.BlockSpec(memory_space=pltpu.SMEM),
            pl.BlockSpec((None, r, c), lambda t, h: (t, 0, 0)),
            pl.BlockSpec((None, r, c), lambda t, h: (t, 0, 0)),
        ],
        out_specs=pl.BlockSpec((None, None, r, c), lambda t, h: (h, t, 0, 0)),
        out_shape=jax.ShapeDtypeStruct((n_heads, nt, r, c), F32),
        compiler_params=_params("arbitrary", "arbitrary"),
        name="bias_tiles",
    )(rel_table.astype(F32), idx, mask)


def _far_bucket(min_dist):
    nb = REL_BUCKETS // 2
    max_exact = nb // 2
    large = max_exact + math.log(min_dist / max_exact) / math.log(REL_MAX_DIST / max_exact) * (nb - max_exact)
    assert large >= nb - 1 + 0.5, "key tile too short for a constant far-field bias"
    return nb - 1


def _lambda_value(lam_ref, lam_init):
    a = lam_ref[...]
    s1 = jnp.sum(a[0:1] * a[1:2], axis=-1, keepdims=True)
    s2 = jnp.sum(a[2:3] * a[3:4], axis=-1, keepdims=True)
    return jnp.exp(s1) - jnp.exp(s2) + lam_init


def _prompt_attn_kernel(table_ref, lam_ref, subln_ref, qd_ref, kd_ref, vd_ref, qf_ref, kf_ref, vf_ref,
                        fcol_ref, bias_ref, cmask_ref, qdn_ref, qfn_ref, od_ref, of_ref,
                        kda_ref, kfa_ref, vdt_ref, vft_ref, qda_ref, qfa_ref,
                        md_ref, accd_ref, mf_ref, accf_ref,
                        sd_ref, pd_ref, ad_ref, sf_ref, pf_ref, af_ref, cd_ref, cf_ref,
                        *, tq, tk, lam_init, far_bucket):
    h = pl.program_id(1)
    qi = pl.program_id(2)
    seq = kd_ref.shape[0]
    ratio = tq // tk

    def prepare_keys_values():
        lane = lax.broadcasted_iota(jnp.int32, (seq, LANES), 1)
        kda_ref[:, :HEAD_DIM] = kd_ref[...]
        kda_ref[:, HEAD_DIM:] = jnp.where(lane < 3, 1.0, 0.0).astype(BF16)
        kfa_ref[:, :HEAD_DIM] = kf_ref[...]
        kfa_ref[:, HEAD_DIM:] = fcol_ref[...]

        ones = jnp.ones((ONES_ROWS, tk), BF16)

        def transpose_values(c, carry):
            r = pl.ds(pl.multiple_of(c * tk, tk), tk)
            vdt_ref[c] = jnp.concatenate([_transpose_bf16(vd_ref[r, :]), ones], axis=0)
            vft_ref[c] = jnp.concatenate([_transpose_bf16(vf_ref[r, :]), ones], axis=0)
            return carry

        lax.fori_loop(0, seq // tk, transpose_values, 0)

    def key_rows(ref, j):
        return ref[pl.ds(pl.multiple_of(j * tk, tk), tk), :]

    def buffer_scores(j, k_ref, q_aug, s_ref, c_ref):
        s = _dot(key_rows(k_ref, j), q_aug)
        s_ref[...] = s
        c_ref[...] = jnp.max(s, axis=0, keepdims=True)

    def prepare_queries(qd_src_ref, qf_src_ref):
        row = lax.broadcasted_iota(jnp.int32, (HEAD_DIM, tq), 0)
        qdt = qd_src_ref[...].astype(F32).T
        zero = jnp.zeros_like(qdt)
        top = jnp.concatenate([jnp.where(row < DIFF_QK_DIM, qdt, zero),
                               jnp.where(row >= DIFF_QK_DIM, qdt, zero)], axis=1).astype(BF16)
        row2 = lax.broadcasted_iota(jnp.int32, (HEAD_DIM, 2 * tq), 0)
        bottom = _split3_on_axis(jnp.full((HEAD_DIM, 2 * tq), table_ref[far_bucket, h] * LOG2E, F32), row2)
        qd_aug = jnp.concatenate([top, bottom], axis=0)
        qf_aug = jnp.concatenate([qf_src_ref[...].astype(F32).T.astype(BF16),
                                  jnp.where(row < 3, -1.0, 0.0).astype(BF16)], axis=0)
        qda_ref[...] = qd_aug
        qfa_ref[...] = qf_aug
        buffer_scores(0, kda_ref, qd_aug, sd_ref, cd_ref)
        buffer_scores(0, kfa_ref, qf_aug, sf_ref, cf_ref)

    @pl.when(qi == 0)
    def _():
        prepare_keys_values()
        prepare_queries(qd_ref, qf_ref)

    branches = ((kda_ref, qda_ref, vdt_ref, sd_ref, pd_ref, ad_ref, md_ref, accd_ref, cd_ref),
                (kfa_ref, qfa_ref, vft_ref, sf_ref, pf_ref, af_ref, mf_ref, accf_ref, cf_ref))

    def pipeline_step(j_prev, adds, j_next):
        for (k_ref, q_ref, vt_ref, s_ref, p_ref, a_ref, m_ref, acc_ref, c_ref), add in zip(branches, adds):
            acc_ref[...] = a_ref[...] * acc_ref[...] + _dot(vt_ref[j_prev], p_ref[...])
            s = s_ref[...]
            if add is None:
                s_max = c_ref[...]
            else:
                if add.ndim == 2 and 2 * add.shape[1] == s.shape[1]:
                    half = add.shape[1]
                    s = jnp.concatenate([s[:, :half] + add, s[:, half:] + add], axis=1)
                else:
                    s = s + add
                s_max = jnp.max(s, axis=0, keepdims=True)
            m_prev = m_ref[...]
            m_new = jnp.maximum(m_prev, s_max)
            p_ref[...] = jnp.exp2(s - m_new).astype(BF16)
            a_ref[...] = jnp.exp2(m_prev - m_new)
            m_ref[...] = m_new
            if j_next is not None:
                buffer_scores(j_next, k_ref, q_ref[...], s_ref, c_ref)

    for k_ref, q_ref, vt_ref, s_ref, p_ref, a_ref, m_ref, acc_ref, c_ref in branches:
        m_ref[...] = jnp.full_like(m_ref, NEG_INF)
        acc_ref[...] = jnp.zeros_like(acc_ref)
        p_ref[...] = jnp.zeros_like(p_ref)
        a_ref[...] = jnp.ones_like(a_ref)

    def far_tiles(unroll):
        def body(i, carry):
            for u in range(unroll):
                j = i * unroll + u
                pipeline_step(jnp.maximum(j - 1, 0), (None, None), j + 1)
            return carry
        return body

    first_diag = qi * ratio
    n_far = jnp.maximum(first_diag - 1, 0)
    n_main = n_far // FAR_UNROLL
    lax.fori_loop(0, n_main, far_tiles(FAR_UNROLL), 0)
    lax.fori_loop(n_main * FAR_UNROLL, n_far, far_tiles(1), 0)

    near = jnp.maximum(first_diag - 1, 0)
    gone = jnp.where(qi == 0, NEG_INF, 0.0)
    pipeline_step(jnp.maximum(near - 1, 0), (bias_ref[0] + gone, gone), first_diag)
    for u in range(ratio):
        pipeline_step(near if u == 0 else first_diag + u - 1,
                      (bias_ref[u + 1], cmask_ref[u]),
                      first_diag + u + 1 if u + 1 < ratio else None)
    last = first_diag + ratio - 1
    for k_ref, q_ref, vt_ref, s_ref, p_ref, a_ref, m_ref, acc_ref, c_ref in branches:
        acc_ref[...] = a_ref[...] * acc_ref[...] + _dot(vt_ref[last], p_ref[...])

    prepare_queries(qdn_ref, qfn_ref)

    lam = _lambda_value(lam_ref, lam_init)
    acc = accd_ref[...]
    num, inv = acc[:HEAD_DIM], 1.0 / acc[HEAD_DIM:HEAD_DIM + 1]
    odt = num[:, :tq] * inv[:, :tq] - lam * (num[:, tq:] * inv[:, tq:])
    od_ref[...] = (_rms_rows(odt.T, subln_ref[...]) * (1.0 - lam_init)).astype(BF16)
    acc = accf_ref[...]
    of_ref[...] = (acc[:HEAD_DIM] * (1.0 / acc[HEAD_DIM:HEAD_DIM + 1])).T.astype(BF16)


def _prompt_attention(zb, fcols, rel_table, lam_params, subln, lam_init, batch, seq, n_heads, tq, tk):
    m = batch * seq
    dw = n_heads * HEAD_DIM
    nq = seq // tq
    ratio = tq // tk
    assert tq == ratio * tk
    far_bucket = _far_bucket(tk + 1)
    vrows = HEAD_DIM + ONES_ROWS
    qpos = tk + jnp.arange(tq, dtype=jnp.int32)
    kpos = jnp.arange((ratio + 1) * tk, dtype=jnp.int32).reshape(ratio + 1, tk)
    bias = _bias_tiles(rel_table, jnp.broadcast_to(qpos, (ratio + 1, tq)), kpos,
                       keys_major=True, minus_bucket=far_bucket)
    cmask = jnp.where(kpos[1:, :, None] <= qpos[None, None, :], 0.0, NEG_INF).astype(F32)

    def q_spec(seg, ahead=0):
        return pl.BlockSpec((tq, HEAD_DIM), lambda b, h, q, seg=seg: (b * nq + jnp.minimum(q + ahead, nq - 1),
                                                                      seg * n_heads + h))

    def kv_spec(seg):
        return pl.BlockSpec((seq, HEAD_DIM), lambda b, h, q, seg=seg: (b, seg * n_heads + h))

    out_spec = pl.BlockSpec((tq, HEAD_DIM), lambda b, h, q: (b * nq + q, h))
    const2 = lambda b, h, q: (0, 0)
    kern = functools.partial(_prompt_attn_kernel, tq=tq, tk=tk, lam_init=lam_init, far_bucket=far_bucket)
    return pl.pallas_call(
        kern,
        grid=(batch, n_heads, nq),
        in_specs=[
            pl.BlockSpec(memory_space=pltpu.SMEM),
            pl.BlockSpec((4, DIFF_QK_DIM), const2),
            pl.BlockSpec((1, HEAD_DIM), const2),
            q_spec(0), kv_spec(1), kv_spec(2), q_spec(3), kv_spec(4), kv_spec(5),
            pl.BlockSpec((None, None, seq, LANES), lambda b, h, q: (b, h, 0, 0)),
            pl.BlockSpec((None, ratio + 1, tk, tq), lambda b, h, q: (h, 0, 0, 0)),
            pl.BlockSpec((ratio, tk, tq), lambda b, h, q: (0, 0, 0)),
            q_spec(0, ahead=1), q_spec(3, ahead=1),
        ],
        out_specs=[out_spec, out_spec],
        out_shape=[jax.ShapeDtypeStruct((m, dw), BF16), jax.ShapeDtypeStruct((m, dw), BF16)],
        scratch_shapes=[
            pltpu.VMEM((seq, 2 * HEAD_DIM), BF16), pltpu.VMEM((seq, 2 * HEAD_DIM), BF16),
            pltpu.VMEM((seq // tk, vrows, tk), BF16), pltpu.VMEM((seq // tk, vrows, tk), BF16),
            pltpu.VMEM((2 * HEAD_DIM, 2 * tq), BF16), pltpu.VMEM((2 * HEAD_DIM, tq), BF16),
            pltpu.VMEM((1, 2 * tq), F32), pltpu.VMEM((vrows, 2 * tq), F32),
            pltpu.VMEM((1, tq), F32), pltpu.VMEM((vrows, tq), F32),
            pltpu.VMEM((tk, 2 * tq), F32), pltpu.VMEM((tk, 2 * tq), BF16), pltpu.VMEM((1, 2 * tq), F32),
            pltpu.VMEM((tk, tq), F32), pltpu.VMEM((tk, tq), BF16), pltpu.VMEM((1, tq), F32),
            pltpu.VMEM((1, 2 * tq), F32), pltpu.VMEM((1, tq), F32),
        ],
        compiler_params=_params("arbitrary", "arbitrary", "arbitrary"),
        name="prompt_attention",
    )(rel_table.astype(F32), lam_params, subln, zb, zb, zb, zb, zb, zb, fcols, bias, cmask, zb, zb)


def _split_diff_queries(q):
    lane = lax.broadcasted_iota(jnp.int32, q.shape, 1)
    zero = jnp.zeros_like(q)
    return jnp.concatenate([jnp.where(lane < DIFF_QK_DIM, q, zero),
                            jnp.where(lane >= DIFF_QK_DIM, q, zero)], axis=0)


def _sample_attn_kernel(lam_ref, subln_ref, z_ref, kdc_ref, vdc_ref, kfc_ref, vfc_ref, frow_ref, bias_ref,
                        cmask_ref, od_ref, of_ref, *, past, t_new, n_heads, lam_init):
    dw = n_heads * HEAD_DIM

    def cached(ref, h):
        return ref[pl.ds(h, past, stride=n_heads), :].astype(BF16)

    def new_rows(seg, h):
        lo = seg * dw + h * HEAD_DIM
        return z_ref[:, lo:lo + HEAD_DIM]

    def attend(q, kc, vc, kn, vn, add_c, add_n):
        sc = _dot_nt(q, kc) + add_c
        sn = _dot_nt(q, kn) + add_n
        mx = jnp.maximum(jnp.max(sc, axis=-1, keepdims=True), jnp.max(sn, axis=-1, keepdims=True))
        pc = jnp.exp2(sc - mx)
        pn = jnp.exp2(sn - mx)
        l = jnp.sum(pc, axis=-1, keepdims=True) + jnp.sum(pn, axis=-1, keepdims=True)
        acc = _dot(pc.astype(BF16), vc) + _dot(pn.astype(BF16), vn)
        return acc, l

    lam = _lambda_value(lam_ref, lam_init)
    cmask = cmask_ref[...]
    for h in range(n_heads):
        cols = slice(h * HEAD_DIM, (h + 1) * HEAD_DIM)
        qs = _split_diff_queries(new_rows(0, h))
        bias_c = bias_ref[h, :, :past]
        bias_n = bias_ref[h, :, past:past + t_new]
        acc, l = attend(qs, cached(kdc_ref, h), cached(vdc_ref, h),
                        new_rows(1, h), new_rows(2, h),
                        jnp.concatenate([bias_c, bias_c], axis=0), jnp.concatenate([bias_n, bias_n], axis=0))
        od = acc[:t_new] / l[:t_new] - lam * (acc[t_new:] / l[t_new:])
        od_ref[:, cols] = (_rms_rows(od, subln_ref[...]) * (1.0 - lam_init)).astype(BF16)

        f0 = frow_ref[h:h + 1, past:past + 1]
        dec_c = (f0 - frow_ref[h:h + 1, :past]) * LOG2E
        dec_n = (f0 - frow_ref[h:h + 1, past:past + t_new]) * LOG2E
        acc, l = attend(new_rows(3, h), cached(kfc_ref, h), cached(vfc_ref, h),
                        new_rows(4, h), new_rows(5, h), dec_c, dec_n + cmask)
        of_ref[:, cols] = (acc / l).astype(BF16)


def _sample_attention(zb, caches, layer, frow, rel_table, lam_params, subln, lam_init, nb, t_new, past, n_heads):
    dw = n_heads * HEAD_DIM
    tpad = frow.shape[-1]
    ar = jnp.arange(t_new, dtype=jnp.int32)
    bias = _bias_tiles(rel_table, (past + ar)[None], jnp.arange(tpad, dtype=jnp.int32)[None])[:, 0]
    cmask = jnp.where(ar[None, :] <= ar[:, None], 0.0, NEG_INF).astype(F32)
    caches = [c.reshape(c.shape[0], nb, past * n_heads, HEAD_DIM) for c in caches]
    cache_spec = pl.BlockSpec((None, None, past * n_heads, HEAD_DIM), lambda b: (layer, b, 0, 0))
    out_spec = pl.BlockSpec((t_new, dw), lambda b: (b, 0))
    const2 = lambda b: (0, 0)
    kern = functools.partial(_sample_attn_kernel, past=past, t_new=t_new, n_heads=n_heads, lam_init=lam_init)
    return pl.pallas_call(
        kern,
        grid=(nb,),
        in_specs=[
            pl.BlockSpec((4, DIFF_QK_DIM), const2),
            pl.BlockSpec((1, HEAD_DIM), const2),
            pl.BlockSpec((t_new, 6 * dw), lambda b: (b, 0)),
            cache_spec, cache_spec, cache_spec, cache_spec,
            pl.BlockSpec((None, n_heads, tpad), lambda b: (b, 0, 0)),
            pl.BlockSpec((n_heads, t_new, tpad), lambda b: (0, 0, 0)),
            pl.BlockSpec((t_new, t_new), const2),
        ],
        out_specs=[out_spec, out_spec],
        out_shape=[jax.ShapeDtypeStruct((nb * t_new, dw), BF16), jax.ShapeDtypeStruct((nb * t_new, dw), BF16)],
        compiler_params=_params("arbitrary"),
        name="sample_attention",
    )(lam_params, subln, zb, *caches, frow, bias, cmask)


def _merge_kernel(x_ref, g1_ref, od_ref, of_ref, wbd_ref, wbf_ref, wga_ref, wgb_ref, wout_ref, gp_ref,
                  o_ref, *, chunk):
    x = x_ref[...]
    xn = _rms_rows(x, g1_ref[...]).astype(BF16)
    od = od_ref[...]
    of = of_ref[...]
    o = None
    for c in range(x.shape[1] // chunk):
        cols = slice(c * chunk, (c + 1) * chunk)
        ga = jax.nn.sigmoid(_dot(xn, wga_ref[:, cols]))
        gb = jax.nn.sigmoid(_dot(xn, wgb_ref[:, cols]))
        u = (ga * _dot(od, wbd_ref[:, cols]) + gb * _dot(of, wbf_ref[:, cols])).astype(BF16)
        part = _dot(u, wout_ref[cols, :])
        o = part if o is None else o + part
    o_ref[...] = x + _rms_rows(o, gp_ref[...])


def _merge(x, g1, od, of, w_bd, w_bf, w_ga, w_gb, w_out, g_post, tm, chunk):
    m, d = x.shape
    dw = od.shape[1]
    row = lambda i: (i, 0)
    const2 = lambda i: (0, 0)
    resident = lambda shape: pl.BlockSpec(shape, const2, pipeline_mode=pl.Buffered(1))
    return pl.pallas_call(
        functools.partial(_merge_kernel, chunk=chunk),
        grid=(m // tm,),
        in_specs=[
            pl.BlockSpec((tm, d), row),
            pl.BlockSpec((1, d), const2),
            pl.BlockSpec((tm, dw), row),
            pl.BlockSpec((tm, dw), row),
            resident((dw, d)), resident((dw, d)), resident((d, d)), resident((d, d)), resident((d, d)),
            pl.BlockSpec((1, d), const2),
        ],
        out_specs=pl.BlockSpec((tm, d), row),
        out_shape=jax.ShapeDtypeStruct((m, d), F32),
        compiler_params=_params("arbitrary"),
        name="merge",
    )(x, g1, od, of, w_bd, w_bf, w_ga, w_gb, w_out, g_post)


def _gelu_tanh(x):
    k = -2.0 * math.sqrt(2.0 / math.pi) * LOG2E
    return x / (1.0 + jnp.exp2(x * (k + (k * 0.044715) * (x * x))))


def _ffn_kernel(*refs, seq, has_edges):
    if has_edges:
        (h_ref, g2_ref, wa_ref, wb_ref, cw_ref, cb_ref, wd_ref, gp_ref, prev_ref,
         o_ref, cs_ref, xn_ref, tail_ref) = refs
    else:
        (h_ref, g2_ref, wa_ref, wb_ref, cw_ref, cb_ref, wd_ref, gp_ref,
         o_ref, cs_ref, xn_ref, tail_ref) = refs
    i = pl.program_id(0)
    f = pl.program_id(1)
    tm = h_ref.shape[0]

    tf = wa_ref.shape[1]

    @pl.when(f == 0)
    def _():
        xn_ref[...] = _rms_rows(h_ref[...], g2_ref[...]).astype(BF16)
        o_ref[...] = jnp.zeros_like(o_ref)

    if not has_edges:
        @pl.when((i * tm) % seq == 0)
        def _():
            tail_ref[f] = jnp.zeros(tail_ref.shape[1:], F32)

    xn = xn_ref[...]

    def gated_chunk(cols):
        width = cols.stop - cols.start
        a = _dot(xn, wa_ref[:, cols])
        gate = _dot(xn, wb_ref[:, cols])
        back1 = pltpu.roll(a, 1, 0)
        back2 = pltpu.roll(a, 2, 0)
        if has_edges:
            shape3 = (tm // seq, seq, width)
            t = lax.broadcasted_iota(jnp.int32, shape3, 1)
            prev = prev_ref[:, :, cols]
            p0, p1 = prev[:, 0:1, :], prev[:, 1:2, :]
            am1 = jnp.where(t == 0, p1, back1.reshape(shape3)).reshape(tm, width)
            am2 = jnp.where(t == 0, p0, jnp.where(t == 1, p1, back2.reshape(shape3))).reshape(tm, width)
            cs_ref[:, :, cols] = a.reshape(shape3)[:, seq - (CONV_WIDTH - 1):, :]
        else:
            prev = tail_ref[f, :, cols]
            p0 = prev[SUBLANES - 2:SUBLANES - 1]
            p1 = prev[SUBLANES - 1:SUBLANES]
            top = lax.broadcasted_iota(jnp.int32, (SUBLANES, width), 0)
            am1 = jnp.concatenate([jnp.where(top == 0, p1, back1[:SUBLANES]), back1[SUBLANES:]], axis=0)
            am2 = jnp.concatenate([jnp.where(top == 0, p0, jnp.where(top == 1, p1, back2[:SUBLANES])),
                                   back2[SUBLANES:]], axis=0)
            tail_ref[f, :, cols] = a[tm - SUBLANES:, :]
            cs_ref[:, :, cols] = a[tm - (CONV_WIDTH - 1):, :][None]
        cw = cw_ref[:, cols]
        ac = cw[0:1] * am2 + cw[1:2] * am1 + cw[2:3] * a + cb_ref[:, cols]
        return (_gelu_tanh(ac) * gate).astype(BF16)

    width = tf // FFN_CHUNKS
    contrib = None
    for c in range(FFN_CHUNKS):
        cols = slice(c * width, (c + 1) * width)
        part = _dot(gated_chunk(cols), wd_ref[cols, :])
        contrib = part if contrib is None else contrib + part
    o_ref[...] += contrib

    @pl.when(f == pl.num_programs(1) - 1)
    def _():
        o_ref[...] = h_ref[...] + _rms_rows(o_ref[...], gp_ref[...])


def _ffn(h, g2, w_up, conv_w, conv_b, w_d, g_post, edges, seq, tm, tf):
    m, d = h.shape
    dff = w_d.shape[0]
    nf = dff // tf
    has_edges = edges is not None
    row = lambda i, f: (i, 0)
    col = lambda i, f: (0, f)
    const2 = lambda i, f: (0, 0)
    in_specs = [
        pl.BlockSpec((tm, d), row),
        pl.BlockSpec((1, d), const2),
        pl.BlockSpec((d, tf), col),
        pl.BlockSpec((d, tf), lambda i, f: (0, nf + f)),
        pl.BlockSpec((CONV_WIDTH, tf), col),
        pl.BlockSpec((1, tf), col),
        pl.BlockSpec((tf, d), lambda i, f: (f, 0)),
        pl.BlockSpec((1, d), const2),
    ]
    args = [h, g2, w_up, w_up, conv_w, conv_b, w_d, g_post]
    if has_edges:
        assert tm % seq == 0
        tails_per_tile = tm // seq
        in_specs.append(pl.BlockSpec((tails_per_tile, CONV_WIDTH - 1, tf), lambda i, f: (i, 0, f)))
        args.append(edges)
    else:
        assert seq % tm == 0
        tails_per_tile = 1
    n_tails = (m // tm) * tails_per_tile
    out, tails = pl.pallas_call(
        functools.partial(_ffn_kernel, seq=seq, has_edges=has_edges),
        grid=(m // tm, nf),
        in_specs=in_specs,
        out_specs=[pl.BlockSpec((tm, d), row),
                   pl.BlockSpec((tails_per_tile, CONV_WIDTH - 1, tf), lambda i, f: (i, 0, f))],
        out_shape=[jax.ShapeDtypeStruct((m, d), F32),
                   jax.ShapeDtypeStruct((n_tails, CONV_WIDTH - 1, dff), F32)],
        scratch_shapes=[pltpu.VMEM((tm, d), BF16), pltpu.VMEM((nf, SUBLANES, tf), F32)],
        compiler_params=_params("arbitrary", "arbitrary"),
        name="conv_ffn",
    )(*args)
    if not has_edges:
        tiles_per_seq = seq // tm
        tails = tails[tiles_per_seq - 1::tiles_per_seq]
    return out, tails


def _tile(n, cap):
    t = min(n, cap)
    assert n % t == 0
    return t


def _layer(layer, hp, hs, past, rel_table, lam_params, lam_init, p):
    (pre1, w_in, b_forget, subln, w_bd, w_bf, w_out, post1, pre2, w_up, conv_w, conv_b, w_down, post2) = p
    batch, seq, d = hp.shape
    nb, t_new, _ = hs.shape
    n_heads = d // (2 * HEAD_DIM)
    dw = n_heads * HEAD_DIM
    dff = w_down.shape[0]
    caches, plogf, conv_prev = past
    plen = plogf.shape[1]

    row2 = lambda v: v.reshape(1, -1).astype(F32)
    w_in_b = w_in.astype(BF16)
    b_f = jnp.pad(b_forget.astype(F32), (0, LANES - n_heads)).reshape(1, LANES)
    w_ga = w_in_b[:, 6 * dw + n_heads:6 * dw + n_heads + d]
    w_gb = w_in_b[:, 6 * dw + n_heads + d:]
    w_bd_b, w_bf_b, w_out_b = w_bd.astype(BF16), w_bf.astype(BF16), w_out.astype(BF16)
    w_up_b = w_up.astype(BF16)
    w_d = w_down.astype(BF16)
    subln2 = row2(subln)

    def dense_tail(x2, od, of, edges, t_seq):
        m = x2.shape[0]
        h1 = _merge(x2, row2(pre1), od, of, w_bd_b, w_bf_b, w_ga, w_gb, w_out_b, row2(post1),
                    _tile(m, MERGE_ROWS), _tile(d, 1024))
        return _ffn(h1, row2(pre2), w_up_b, conv_w.astype(F32), row2(conv_b), w_d, row2(post2),
                    edges, t_seq, _tile(m, 512), _tile(dff, 1024))

    heads = lambda a, n, t: a.reshape(n, t, n_heads, HEAD_DIM)

    xp = hp.reshape(batch * seq, d)
    zb, kd, vd, kf, vf, logf, flp = _inproj(xp, row2(pre1), w_in_b, b_f, n_heads,
                                            _tile(batch * seq, INPROJ_ROWS))
    fcols = _cumsum_keys(flp.reshape(batch, seq, LANES), n_heads, _tile(seq, 256))
    od, of = _prompt_attention(zb, fcols, rel_table, lam_params, subln2, lam_init,
                               batch, seq, n_heads, _tile(seq, ATTN_QUERY_TILE), _tile(seq, ATTN_KEY_TILE))
    hp_out, conv_p = dense_tail(xp, od, of, None, seq)
    state_p = (heads(kd, batch, seq), heads(vd, batch, seq), heads(kf, batch, seq), heads(vf, batch, seq),
               logf.reshape(batch, seq, n_heads), conv_p)

    xs = hs.reshape(nb * t_new, d)
    zb, kd, vd, kf, vf, logf, flp = _inproj(xs, row2(pre1), w_in_b, b_f, n_heads,
                                            _tile(nb * t_new, INPROJ_ROWS))
    tpad = -(-(plen + t_new) // LANES) * LANES
    flog = jnp.concatenate([jnp.swapaxes(plogf.astype(F32), 1, 2),
                            jnp.swapaxes(logf.reshape(nb, t_new, n_heads), 1, 2),
                            jnp.zeros((nb, n_heads, tpad - plen - t_new), F32)], axis=2)
    frow = _cumsum_rows(flog.reshape(nb * n_heads, tpad)).reshape(nb, n_heads, tpad)
    od, of = _sample_attention(zb, caches, layer, frow, rel_table, lam_params, subln2, lam_init,
                               nb, t_new, plen, n_heads)
    hs_out, conv_s = dense_tail(xs, od, of, conv_prev.astype(F32), t_new)
    state_s = (heads(kd, nb, t_new), heads(vd, nb, t_new), heads(kf, nb, t_new), heads(vf, nb, t_new),
               logf.reshape(nb, t_new, n_heads), conv_s)
    return hp_out.reshape(batch, seq, d), hs_out.reshape(nb, t_new, d), state_p, state_s


def kernel(x_prompt, x_sample, cache_diff_k, cache_diff_v, cache_fox_k, cache_fox_v, cache_fox_logf,
           state_ffn_conv, rel_table, pre_norm1, w_in, b_forget, lam_q1, lam_k1, lam_q2, lam_k2,
           diff_subln, w_branch_diff, w_branch_fox, w_out, post_norm1, pre_norm2, w_up, conv_w, conv_b,
           w_down, post_norm2):
    depth = w_in.shape[0]
    hp, hs = x_prompt, x_sample
    caches = (cache_diff_k, cache_diff_v, cache_fox_k, cache_fox_v)
    new_p, new_s = [], []
    for l in range(depth):
        lam_init = 0.8 - 0.6 * math.exp(-0.3 * l)
        lam_params = jnp.stack([lam_q1[l], lam_k1[l], lam_q2[l], lam_k2[l]]).astype(F32)
        params = (pre_norm1[l], w_in[l], b_forget[l], diff_subln[l], w_branch_diff[l], w_branch_fox[l],
                  w_out[l], post_norm1[l], pre_norm2[l], w_up[l], conv_w[l], conv_b[l], w_down[l],
                  post_norm2[l])
        past = (caches, cache_fox_logf[l], state_ffn_conv[l])
        hp, hs, sp, ss = _layer(l, hp, hs, past, rel_table, lam_params, lam_init, params)
        new_p.append(sp)
        new_s.append(ss)
    st = lambda lst, i: jnp.stack([e[i] for e in lst])
    return (hp, hs,
            st(new_p, 0), st(new_p, 1), st(new_p, 2), st(new_p, 3), st(new_p, 4), st(new_p, 5),
            st(new_s, 0), st(new_s, 1), st(new_s, 2), st(new_s, 3), st(new_s, 4), st(new_s, 5))
```

```python
import functools
import math

import jax
import jax.numpy as jnp
from jax import lax
from jax.experimental import pallas as pl
from jax.experimental.pallas import tpu as pltpu

HEAD_DIM = 128
DIFF_QK_DIM = HEAD_DIM // 2
CHUNK = 64
CONV_WIDTH = 3
REL_BUCKETS = 32
REL_MAX_DIST = 128
EPS = 1e-6
NEG_INF = -1e30
LOG2E = math.log2(math.e)

LANES = 128
SUBLANES = 8
ONES_ROWS = 16
ATTN_QUERY_TILE = 512
ATTN_KEY_TILE = 512
FAR_UNROLL = 2
FFN_CHUNKS = 2
MERGE_ROWS = 256
INPROJ_ROWS = 256
VMEM_LIMIT_BYTES = 60 * 1024 * 1024

F32 = jnp.float32
BF16 = jnp.bfloat16


def _params(*sem, flags=None):
    return pltpu.CompilerParams(dimension_semantics=sem, vmem_limit_bytes=VMEM_LIMIT_BYTES, flags=flags)


def _rms_rows(x, g):
    return x * lax.rsqrt(jnp.mean(x * x, axis=-1, keepdims=True) + EPS) * g


def _dot(a, b):
    return jnp.dot(a, b, preferred_element_type=F32)


def _dot_nt(a, b):
    return lax.dot_general(a, b, (((1,), (1,)), ((), ())), preferred_element_type=F32)


def _split3(x):
    hi = x.astype(BF16)
    r1 = x - hi.astype(F32)
    mid = r1.astype(BF16)
    lo = (r1 - mid.astype(F32)).astype(BF16)
    return hi, mid, lo


def _split3_on_axis(x, index):
    hi, mid, lo = (v.astype(F32) for v in _split3(x))
    return jnp.where(index == 0, hi, jnp.where(index == 1, mid, jnp.where(index == 2, lo, 0.0))).astype(BF16)


def _transpose_bf16(x):
    return x.astype(F32).T.astype(BF16)


def _inproj_kernel(x_ref, g_ref, w_ref, wf_ref, bf_ref,
                   zb_ref, kd_ref, vd_ref, kf_ref, vf_ref, logf_ref, *rest, n_heads, q_scales, seq):
    dw = kd_ref.shape[1]
    tm = x_ref.shape[0]
    xn = _rms_rows(x_ref[...], g_ref[...]).astype(BF16)
    fl = _dot(xn, wf_ref[...]) + bf_ref[...]
    lf = jnp.minimum(fl, 0.0) - jnp.log1p(jnp.exp(-jnp.abs(fl)))
    logf_ref[...] = lf[:, :n_heads]
    if seq is not None:
        fcol_ref, carry_ref = rest

        @pl.when((pl.program_id(0) * tm) % seq == 0)
        def _():
            carry_ref[...] = jnp.zeros_like(carry_ref)

        hi, mid, lo = _split3(lf)
        tri = _triangle(tm, lower=True)
        local = _dot(tri, hi) + _dot(tri, mid) + _dot(tri, lo)
        f = (local + carry_ref[...]) * LOG2E
        carry_ref[...] = carry_ref[...] + local[tm - 1:tm, :]
        lane = lax.broadcasted_iota(jnp.int32, (tm, LANES), 1)
        for h in range(n_heads):
            fcol_ref[h] = _split3_on_axis(jnp.broadcast_to(f[:, h:h + 1], (tm, LANES)), lane)
    f32_outs = {1: kd_ref, 2: vd_ref, 4: kf_ref, 5: vf_ref}
    for seg in range(6):
        cols = slice(seg * dw, (seg + 1) * dw)
        z = _dot(xn, w_ref[:, cols])
        if seg in f32_outs:
            f32_outs[seg][...] = z
        else:
            z = z * q_scales[seg]
        zb_ref[:, cols] = z.astype(BF16)


def _inproj(x, g, w_in, b_f, n_heads, tm, seq=None):
    m, d = x.shape
    dw = n_heads * HEAD_DIM
    assert (6 * dw) % LANES == 0
    row = lambda i: (i, 0)
    const2 = lambda i: (0, 0)
    f32_out = jax.ShapeDtypeStruct((m, dw), F32)
    q_scales = {0: DIFF_QK_DIM ** -0.5 * LOG2E, 3: HEAD_DIM ** -0.5 * LOG2E}
    extra_specs, extra_shapes, scratch = [], [], []
    if seq is not None:
        assert seq % tm == 0
        tiles = seq // tm
        extra_specs = [pl.BlockSpec((None, n_heads, tm, LANES), lambda i: (i // tiles, 0, i % tiles, 0))]
        extra_shapes = [jax.ShapeDtypeStruct((m // seq, n_heads, seq, LANES), BF16)]
        scratch = [pltpu.VMEM((1, LANES), F32)]
    return pl.pallas_call(
        functools.partial(_inproj_kernel, n_heads=n_heads, q_scales=q_scales, seq=seq),
        grid=(m // tm,),
        in_specs=[
            pl.BlockSpec((tm, d), row),
            pl.BlockSpec((1, d), const2),
            pl.BlockSpec((d, 6 * dw), const2, pipeline_mode=pl.Buffered(1)),
            pl.BlockSpec((d, LANES), lambda i: (0, 6 * dw // LANES)),
            pl.BlockSpec((1, LANES), const2),
        ],
        out_specs=[
            pl.BlockSpec((tm, 6 * dw), row),
            pl.BlockSpec((tm, dw), row),
            pl.BlockSpec((tm, dw), row),
            pl.BlockSpec((tm, dw), row),
            pl.BlockSpec((tm, dw), row),
            pl.BlockSpec((tm, n_heads), row),
        ] + extra_specs,
        out_shape=[
            jax.ShapeDtypeStruct((m, 6 * dw), BF16),
            f32_out, f32_out, f32_out, f32_out,
            jax.ShapeDtypeStruct((m, n_heads), F32),
        ] + extra_shapes,
        scratch_shapes=scratch,
        compiler_params=_params("arbitrary"),
        name="inproj",
    )(x, g, w_in, w_in, b_f)


def _triangle(n, lower):
    r = lax.broadcasted_iota(jnp.int32, (n, n), 0)
    c = lax.broadcasted_iota(jnp.int32, (n, n), 1)
    return ((r >= c) if lower else (r <= c)).astype(BF16)


def _cumsum_rows_kernel(x_ref, o_ref):
    t = x_ref.shape[1]
    tri = _triangle(LANES, lower=False)
    carry = jnp.zeros((x_ref.shape[0], 1), F32)
    for k in range(t // LANES):
        cols = slice(k * LANES, (k + 1) * LANES)
        hi, mid, lo = _split3(x_ref[:, cols])
        local = _dot(hi, tri) + _dot(mid, tri) + _dot(lo, tri)
        o_ref[:, cols] = local + carry
        carry = carry + local[:, LANES - 1:LANES]


def _cumsum_rows(x):
    return pl.pallas_call(
        _cumsum_rows_kernel,
        out_shape=jax.ShapeDtypeStruct(x.shape, F32),
        compiler_params=_params(),
        name="cumsum_rows",
    )(x)


def _t5_bucket(rel):
    nb = REL_BUCKETS // 2
    max_exact = nb // 2
    n = jnp.abs(rel)
    nf = jnp.maximum(n, 1).astype(jnp.float32)
    large = max_exact + (jnp.log(nf / max_exact) / math.log(REL_MAX_DIST / max_exact)
                         * (nb - max_exact)).astype(jnp.int32)
    large = jnp.minimum(large, nb - 1)
    return jnp.where(rel > 0, nb, 0) + jnp.where(n < max_exact, n, large)


def _bias_kernel(table_ref, idx_ref, mask_ref, o_ref, *, minus_bucket):
    h = pl.program_id(1)
    idx = idx_ref[...]
    acc = mask_ref[...]
    if minus_bucket is not None:
        acc = acc - table_ref[minus_bucket, h]
    vals = [table_ref[b, h] for b in range(REL_BUCKETS)]
    bit = 1
    while len(vals) > 1:
        odd = (idx & bit) != 0
        vals = [jnp.where(odd, vals[2 * i + 1], vals[2 * i]) for i in range(len(vals) // 2)]
        bit *= 2
    o_ref[...] = (acc + vals[0]) * LOG2E


def _bias_tiles(rel_table, qpos, kpos, keys_major=False, minus_bucket=None):
    n_heads = rel_table.shape[1]
    if keys_major:
        qp, kp = qpos[:, None, :], kpos[:, :, None]
    else:
        qp, kp = qpos[:, :, None], kpos[:, None, :]
    idx = _t5_bucket(kp - qp).astype(jnp.int32)
    mask = jnp.where((kp // CHUNK) <= (qp // CHUNK), 0.0, NEG_INF).astype(F32)
    nt, r, c = idx.shape
    return pl.pallas_call(
        functools.partial(_bias_kernel, minus_bucket=minus_bucket),
        grid=(nt, n_heads),
        in_specs=[
            pl.BlockSpec(memory_space=pltpu.SMEM),
            pl.BlockSpec((None, r, c), lambda t, h: (t, 0, 0)),
            pl.BlockSpec((None, r, c), lambda t, h: (t, 0, 0)),
        ],
        out_specs=pl.BlockSpec((None, None, r, c), lambda t, h: (h, t, 0, 0)),
        out_shape=jax.ShapeDtypeStruct((n_heads, nt, r, c), F32),
        compiler_params=_params("arbitrary", "arbitrary"),
        name="bias_tiles",
    )(rel_table.astype(F32), idx, mask)


def _far_bucket(min_dist):
    nb = REL_BUCKETS // 2
    max_exact = nb // 2
    large = max_exact + math.log(min_dist / max_exact) / math.log(REL_MAX_DIST / max_exact) * (nb - max_exact)
    assert large >= nb - 1 + 0.5, "key tile too short for a constant far-field bias"
    return nb - 1


def _lambda_value(lam_ref, lam_init):
    a = lam_ref[...]
    s1 = jnp.sum(a[0:1] * a[1:2], axis=-1, keepdims=True)
    s2 = jnp.sum(a[2:3] * a[3:4], axis=-1, keepdims=True)
    return jnp.exp(s1) - jnp.exp(s2) + lam_init


def _prompt_attn_kernel(table_ref, lam_ref, subln_ref, qd_ref, kd_ref, vd_ref, qf_ref, kf_ref, vf_ref,
                        fcol_ref, bias_ref, cmask_ref, qdn_ref, qfn_ref, od_ref, of_ref,
                        kda_ref, kfa_ref, vdt_ref, vft_ref, qda_ref, qfa_ref,
                        md_ref, accd_ref, mf_ref, accf_ref,
                        sd_ref, pd_ref, ad_ref, sf_ref, pf_ref, af_ref, cd_ref, cf_ref,
                        *, tq, tk, lam_init, far_bucket):
    h = pl.program_id(1)
    qi = pl.program_id(2)
    seq = kd_ref.shape[0]
    ratio = tq // tk

    def prepare_keys_values():
        lane = lax.broadcasted_iota(jnp.int32, (seq, LANES), 1)
        kda_ref[:, :HEAD_DIM] = kd_ref[...]
        kda_ref[:, HEAD_DIM:] = jnp.where(lane < 3, 1.0, 0.0).astype(BF16)
        kfa_ref[:, :HEAD_DIM] = kf_ref[...]
        kfa_ref[:, HEAD_DIM:] = fcol_ref[...]

        ones = jnp.ones((ONES_ROWS, tk), BF16)

        def transpose_values(c, carry):
            r = pl.ds(pl.multiple_of(c * tk, tk), tk)
            vdt_ref[c] = jnp.concatenate([_transpose_bf16(vd_ref[r, :]), ones], axis=0)
            vft_ref[c] = jnp.concatenate([_transpose_bf16(vf_ref[r, :]), ones], axis=0)
            return carry

        lax.fori_loop(0, seq // tk, transpose_values, 0)

    def key_rows(ref, j):
        return ref[pl.ds(pl.multiple_of(j * tk, tk), tk), :]

    def buffer_scores(j, k_ref, q_aug, s_ref, c_ref, add=None):
        s = _dot(key_rows(k_ref, j), q_aug)
        if add is not None:
            s = s + add
        s_ref[...] = s
        c_ref[...] = jnp.max(s, axis=0, keepdims=True)

    def prepare_queries(qd_src_ref, qf_src_ref):
        row = lax.broadcasted_iota(jnp.int32, (HEAD_DIM, tq), 0)
        qdt = qd_src_ref[...].astype(F32).T
        zero = jnp.zeros_like(qdt)
        top = jnp.concatenate([jnp.where(row < DIFF_QK_DIM, qdt, zero),
                               jnp.where(row >= DIFF_QK_DIM, qdt, zero)], axis=1).astype(BF16)
        row2 = lax.broadcasted_iota(jnp.int32, (HEAD_DIM, 2 * tq), 0)
        bottom = _split3_on_axis(jnp.full((HEAD_DIM, 2 * tq), table_ref[far_bucket, h] * LOG2E, F32), row2)
        qd_aug = jnp.concatenate([top, bottom], axis=0)
        qf_aug = jnp.concatenate([qf_src_ref[...].astype(F32).T.astype(BF16),
                                  jnp.where(row < 3, -1.0, 0.0).astype(BF16)], axis=0)
        qda_ref[...] = qd_aug
        qfa_ref[...] = qf_aug
        buffer_scores(0, kda_ref, qd_aug, sd_ref, cd_ref)
        buffer_scores(0, kfa_ref, qf_aug, sf_ref, cf_ref)

    @pl.when(qi == 0)
    def _():
        prepare_keys_values()
        prepare_queries(qd_ref, qf_ref)

    branches = ((kda_ref, qda_ref, vdt_ref, sd_ref, pd_ref, ad_ref, md_ref, accd_ref, cd_ref),
                (kfa_ref, qfa_ref, vft_ref, sf_ref, pf_ref, af_ref, mf_ref, accf_ref, cf_ref))

    def pipeline_step(j_prev, adds, j_next, next_adds=(None, None)):
        for (k_ref, q_ref, vt_ref, s_ref, p_ref, a_ref, m_ref, acc_ref, c_ref), add, next_add in zip(
                branches, adds, next_adds):
            acc_ref[...] = a_ref[...] * acc_ref[...] + _dot(vt_ref[j_prev], p_ref[...])
            s = s_ref[...]
            if add is None:
                s_max = c_ref[...]
            else:
                s = s + add
                s_max = jnp.max(s, axis=0, keepdims=True)
            m_prev = m_ref[...]
            m_new = jnp.maximum(m_prev, s_max)
            p_ref[...] = jnp.exp2(s - m_new).astype(BF16)
            a_ref[...] = jnp.exp2(m_prev - m_new)
            m_ref[...] = m_new
            if j_next is not None:
                buffer_scores(j_next, k_ref, q_ref[...], s_ref, c_ref, next_add)

    for k_ref, q_ref, vt_ref, s_ref, p_ref, a_ref, m_ref, acc_ref, c_ref in branches:
        m_ref[...] = jnp.full_like(m_ref, NEG_INF)
        acc_ref[...] = jnp.zeros_like(acc_ref)
        p_ref[...] = jnp.zeros_like(p_ref)
        a_ref[...] = jnp.ones_like(a_ref)

    def far_tiles(unroll):
        def body(i, carry):
            for u in range(unroll):
                j = i * unroll + u
                pipeline_step(jnp.maximum(j - 1, 0), (None, None), j + 1)
            return carry
        return body

    first_diag = qi * ratio
    n_far = jnp.maximum(first_diag - 1, 0)
    n_main = n_far // FAR_UNROLL
    lax.fori_loop(0, n_main, far_tiles(FAR_UNROLL), 0)
    lax.fori_loop(n_main * FAR_UNROLL, n_far, far_tiles(1), 0)

    near = jnp.maximum(first_diag - 1, 0)
    gone = jnp.where(qi == 0, NEG_INF, 0.0)

    def diag_adds(u):
        bias = bias_ref[u + 1]
        return jnp.concatenate([bias, bias], axis=1), cmask_ref[u]

    bias = bias_ref[0] + gone
    pipeline_step(jnp.maximum(near - 1, 0), (jnp.concatenate([bias, bias], axis=1), gone),
                  first_diag, diag_adds(0))
    for u in range(ratio):
        more = u + 1 < ratio
        pipeline_step(near if u == 0 else first_diag + u - 1, (None, None),
                      first_diag + u + 1 if more else None, diag_adds(u + 1) if more else (None, None))
    last = first_diag + ratio - 1
    for k_ref, q_ref, vt_ref, s_ref, p_ref, a_ref, m_ref, acc_ref, c_ref in branches:
        acc_ref[...] = a_ref[...] * acc_ref[...] + _dot(vt_ref[last], p_ref[...])

    prepare_queries(qdn_ref, qfn_ref)

    lam = _lambda_value(lam_ref, lam_init)
    acc = accd_ref[...]
    num, inv = acc[:HEAD_DIM], 1.0 / acc[HEAD_DIM:HEAD_DIM + 1]
    odt = num[:, :tq] * inv[:, :tq] - lam * (num[:, tq:] * inv[:, tq:])
    od_ref[...] = (_rms_rows(odt.T, subln_ref[...]) * (1.0 - lam_init)).astype(BF16)
    acc = accf_ref[...]
    of_ref[...] = (acc[:HEAD_DIM] * (1.0 / acc[HEAD_DIM:HEAD_DIM + 1])).T.astype(BF16)


def _prompt_attention(zb, fcols, rel_table, lam_params, subln, lam_init, batch, seq, n_heads, tq, tk):
    m = batch * seq
    dw = n_heads * HEAD_DIM
    nq = seq // tq
    ratio = tq // tk
    assert tq == ratio * tk
    far_bucket = _far_bucket(tk + 1)
    vrows = HEAD_DIM + ONES_ROWS
    qpos = tk + jnp.arange(tq, dtype=jnp.int32)
    kpos = jnp.arange((ratio + 1) * tk, dtype=jnp.int32).reshape(ratio + 1, tk)
    bias = _bias_tiles(rel_table, jnp.broadcast_to(qpos, (ratio + 1, tq)), kpos,
                       keys_major=True, minus_bucket=far_bucket)
    cmask = jnp.where(kpos[1:, :, None] <= qpos[None, None, :], 0.0, NEG_INF).astype(F32)

    def q_spec(seg, ahead=0):
        return pl.BlockSpec((tq, HEAD_DIM), lambda b, h, q, seg=seg: (b * nq + jnp.minimum(q + ahead, nq - 1),
                                                                      seg * n_heads + h))

    def kv_spec(seg):
        return pl.BlockSpec((seq, HEAD_DIM), lambda b, h, q, seg=seg: (b, seg * n_heads + h))

    out_spec = pl.BlockSpec((tq, HEAD_DIM), lambda b, h, q: (b * nq + q, h))
    const2 = lambda b, h, q: (0, 0)
    kern = functools.partial(_prompt_attn_kernel, tq=tq, tk=tk, lam_init=lam_init, far_bucket=far_bucket)
    return pl.pallas_call(
        kern,
        grid=(batch, n_heads, nq),
        in_specs=[
            pl.BlockSpec(memory_space=pltpu.SMEM),
            pl.BlockSpec((4, DIFF_QK_DIM), const2),
            pl.BlockSpec((1, HEAD_DIM), const2),
            q_spec(0), kv_spec(1), kv_spec(2), q_spec(3), kv_spec(4), kv_spec(5),
            pl.BlockSpec((None, None, seq, LANES), lambda b, h, q: (b, h, 0, 0)),
            pl.BlockSpec((None, ratio + 1, tk, tq), lambda b, h, q: (h, 0, 0, 0)),
            pl.BlockSpec((ratio, tk, tq), lambda b, h, q: (0, 0, 0)),
            q_spec(0, ahead=1), q_spec(3, ahead=1),
        ],
        out_specs=[out_spec, out_spec],
        out_shape=[jax.ShapeDtypeStruct((m, dw), BF16), jax.ShapeDtypeStruct((m, dw), BF16)],
        scratch_shapes=[
            pltpu.VMEM((seq, 2 * HEAD_DIM), BF16), pltpu.VMEM((seq, 2 * HEAD_DIM), BF16),
            pltpu.VMEM((seq // tk, vrows, tk), BF16), pltpu.VMEM((seq // tk, vrows, tk), BF16),
            pltpu.VMEM((2 * HEAD_DIM, 2 * tq), BF16), pltpu.VMEM((2 * HEAD_DIM, tq), BF16),
            pltpu.VMEM((1, 2 * tq), F32), pltpu.VMEM((vrows, 2 * tq), F32),
            pltpu.VMEM((1, tq), F32), pltpu.VMEM((vrows, tq), F32),
            pltpu.VMEM((tk, 2 * tq), F32), pltpu.VMEM((tk, 2 * tq), BF16), pltpu.VMEM((1, 2 * tq), F32),
            pltpu.VMEM((tk, tq), F32), pltpu.VMEM((tk, tq), BF16), pltpu.VMEM((1, tq), F32),
            pltpu.VMEM((1, 2 * tq), F32), pltpu.VMEM((1, tq), F32),
        ],
        compiler_params=_params("arbitrary", "arbitrary", "arbitrary"),
        name="prompt_attention",
    )(rel_table.astype(F32), lam_params, subln, zb, zb, zb, zb, zb, zb, fcols, bias, cmask, zb, zb)


def _split_diff_queries(q):
    lane = lax.broadcasted_iota(jnp.int32, q.shape, 1)
    zero = jnp.zeros_like(q)
    return jnp.concatenate([jnp.where(lane < DIFF_QK_DIM, q, zero),
                            jnp.where(lane >= DIFF_QK_DIM, q, zero)], axis=0)


def _sample_attn_kernel(lam_ref, subln_ref, z_ref, kdc_ref, vdc_ref, kfc_ref, vfc_ref, frow_ref, bias_ref,
                        cmask_ref, od_ref, of_ref, *, past, t_new, n_heads, lam_init):
    dw = n_heads * HEAD_DIM

    def cached(ref, h):
        return ref[pl.ds(h, past, stride=n_heads), :].astype(BF16)

    def new_rows(seg, h):
        lo = seg * dw + h * HEAD_DIM
        return z_ref[:, lo:lo + HEAD_DIM]

    def attend(q, kc, vc, kn, vn, add_c, add_n):
        sc = _dot_nt(q, kc) + add_c
        sn = _dot_nt(q, kn) + add_n
        mx = jnp.maximum(jnp.max(sc, axis=-1, keepdims=True), jnp.max(sn, axis=-1, keepdims=True))
        pc = jnp.exp2(sc - mx)
        pn = jnp.exp2(sn - mx)
        l = jnp.sum(pc, axis=-1, keepdims=True) + jnp.sum(pn, axis=-1, keepdims=True)
        acc = _dot(pc.astype(BF16), vc) + _dot(pn.astype(BF16), vn)
        return acc, l

    lam = _lambda_value(lam_ref, lam_init)
    cmask = cmask_ref[...]
    for h in range(n_heads):
        cols = slice(h * HEAD_DIM, (h + 1) * HEAD_DIM)
        qs = _split_diff_queries(new_rows(0, h))
        bias_c = bias_ref[h, :, :past]
        bias_n = bias_ref[h, :, past:past + t_new]
        acc, l = attend(qs, cached(kdc_ref, h), cached(vdc_ref, h),
                        new_rows(1, h), new_rows(2, h),
                        jnp.concatenate([bias_c, bias_c], axis=0), jnp.concatenate([bias_n, bias_n], axis=0))
        od = acc[:t_new] / l[:t_new] - lam * (acc[t_new:] / l[t_new:])
        od_ref[:, cols] = (_rms_rows(od, subln_ref[...]) * (1.0 - lam_init)).astype(BF16)

        f0 = frow_ref[h:h + 1, past:past + 1]
        dec_c = (f0 - frow_ref[h:h + 1, :past]) * LOG2E
        dec_n = (f0 - frow_ref[h:h + 1, past:past + t_new]) * LOG2E
        acc, l = attend(new_rows(3, h), cached(kfc_ref, h), cached(vfc_ref, h),
                        new_rows(4, h), new_rows(5, h), dec_c, dec_n + cmask)
        of_ref[:, cols] = (acc / l).astype(BF16)


def _sample_attention(zb, caches, layer, frow, rel_table, lam_params, subln, lam_init, nb, t_new, past, n_heads):
    dw = n_heads * HEAD_DIM
    tpad = frow.shape[-1]
    ar = jnp.arange(t_new, dtype=jnp.int32)
    bias = _bias_tiles(rel_table, (past + ar)[None], jnp.arange(tpad, dtype=jnp.int32)[None])[:, 0]
    cmask = jnp.where(ar[None, :] <= ar[:, None], 0.0, NEG_INF).astype(F32)
    caches = [c.reshape(c.shape[0], nb, past * n_heads, HEAD_DIM) for c in caches]
    cache_spec = pl.BlockSpec((None, None, past * n_heads, HEAD_DIM), lambda b: (layer, b, 0, 0))
    out_spec = pl.BlockSpec((t_new, dw), lambda b: (b, 0))
    const2 = lambda b: (0, 0)
    kern = functools.partial(_sample_attn_kernel, past=past, t_new=t_new, n_heads=n_heads, lam_init=lam_init)
    return pl.pallas_call(
        kern,
        grid=(nb,),
        in_specs=[
            pl.BlockSpec((4, DIFF_QK_DIM), const2),
            pl.BlockSpec((1, HEAD_DIM), const2),
            pl.BlockSpec((t_new, 6 * dw), lambda b: (b, 0)),
            cache_spec, cache_spec, cache_spec, cache_spec,
            pl.BlockSpec((None, n_heads, tpad), lambda b: (b, 0, 0)),
            pl.BlockSpec((n_heads, t_new, tpad), lambda b: (0, 0, 0)),
            pl.BlockSpec((t_new, t_new), const2),
        ],
        out_specs=[out_spec, out_spec],
        out_shape=[jax.ShapeDtypeStruct((nb * t_new, dw), BF16), jax.ShapeDtypeStruct((nb * t_new, dw), BF16)],
        compiler_params=_params("arbitrary"),
        name="sample_attention",
    )(lam_params, subln, zb, *caches, frow, bias, cmask)


def _merge_kernel(x_ref, g1_ref, od_ref, of_ref, wbd_ref, wbf_ref, wga_ref, wgb_ref, wout_ref, gp_ref,
                  o_ref, *, chunk):
    x = x_ref[...]
    xn = _rms_rows(x, g1_ref[...]).astype(BF16)
    od = od_ref[...]
    of = of_ref[...]
    o = None
    for c in range(x.shape[1] // chunk):
        cols = slice(c * chunk, (c + 1) * chunk)
        ga = jax.nn.sigmoid(_dot(xn, wga_ref[:, cols]))
        gb = jax.nn.sigmoid(_dot(xn, wgb_ref[:, cols]))
        u = (ga * _dot(od, wbd_ref[:, cols]) + gb * _dot(of, wbf_ref[:, cols])).astype(BF16)
        part = _dot(u, wout_ref[cols, :])
        o = part if o is None else o + part
    o_ref[...] = x + _rms_rows(o, gp_ref[...])


def _merge(x, g1, od, of, w_bd, w_bf, w_ga, w_gb, w_out, g_post, tm, chunk):
    m, d = x.shape
    dw = od.shape[1]
    row = lambda i: (i, 0)
    const2 = lambda i: (0, 0)
    resident = lambda shape: pl.BlockSpec(shape, const2, pipeline_mode=pl.Buffered(1))
    return pl.pallas_call(
        functools.partial(_merge_kernel, chunk=chunk),
        grid=(m // tm,),
        in_specs=[
            pl.BlockSpec((tm, d), row),
            pl.BlockSpec((1, d), const2),
            pl.BlockSpec((tm, dw), row),
            pl.BlockSpec((tm, dw), row),
            resident((dw, d)), resident((dw, d)), resident((d, d)), resident((d, d)), resident((d, d)),
            pl.BlockSpec((1, d), const2),
        ],
        out_specs=pl.BlockSpec((tm, d), row),
        out_shape=jax.ShapeDtypeStruct((m, d), F32),
        compiler_params=_params("arbitrary"),
        name="merge",
    )(x, g1, od, of, w_bd, w_bf, w_ga, w_gb, w_out, g_post)


def _gelu_tanh(x):
    k = -2.0 * math.sqrt(2.0 / math.pi) * LOG2E
    return x / (1.0 + jnp.exp2(x * (k + (k * 0.044715) * (x * x))))


def _ffn_kernel(*refs, seq, has_edges):
    if has_edges:
        (h_ref, g2_ref, wa_ref, wb_ref, cw_ref, cb_ref, wd_ref, gp_ref, prev_ref,
         o_ref, cs_ref, xn_ref, tail_ref) = refs
    else:
        (h_ref, g2_ref, wa_ref, wb_ref, cw_ref, cb_ref, wd_ref, gp_ref,
         o_ref, cs_ref, xn_ref, tail_ref) = refs
    i = pl.program_id(0)
    f = pl.program_id(1)
    tm = h_ref.shape[0]

    tf = wa_ref.shape[1]

    @pl.when(f == 0)
    def _():
        xn_ref[...] = _rms_rows(h_ref[...], g2_ref[...]).astype(BF16)
        o_ref[...] = jnp.zeros_like(o_ref)

    if not has_edges:
        @pl.when((i * tm) % seq == 0)
        def _():
            tail_ref[f] = jnp.zeros(tail_ref.shape[1:], F32)

    xn = xn_ref[...]

    def gated_chunk(cols):
        width = cols.stop - cols.start
        a = _dot(xn, wa_ref[:, cols])
        gate = _dot(xn, wb_ref[:, cols])
        back1 = pltpu.roll(a, 1, 0)
        back2 = pltpu.roll(a, 2, 0)
        if has_edges:
            shape3 = (tm // seq, seq, width)
            t = lax.broadcasted_iota(jnp.int32, shape3, 1)
            prev = prev_ref[:, :, cols]
            p0, p1 = prev[:, 0:1, :], prev[:, 1:2, :]
            am1 = jnp.where(t == 0, p1, back1.reshape(shape3)).reshape(tm, width)
            am2 = jnp.where(t == 0, p0, jnp.where(t == 1, p1, back2.reshape(shape3))).reshape(tm, width)
            cs_ref[:, :, cols] = a.reshape(shape3)[:, seq - (CONV_WIDTH - 1):, :]
        else:
            prev = tail_ref[f, :, cols]
            p0 = prev[SUBLANES - 2:SUBLANES - 1]
            p1 = prev[SUBLANES - 1:SUBLANES]
            top = lax.broadcasted_iota(jnp.int32, (SUBLANES, width), 0)
            am1 = jnp.concatenate([jnp.where(top == 0, p1, back1[:SUBLANES]), back1[SUBLANES:]], axis=0)
            am2 = jnp.concatenate([jnp.where(top == 0, p0, jnp.where(top == 1, p1, back2[:SUBLANES])),
                                   back2[SUBLANES:]], axis=0)
            tail_ref[f, :, cols] = a[tm - SUBLANES:, :]
            cs_ref[:, :, cols] = a[tm - (CONV_WIDTH - 1):, :][None]
        cw = cw_ref[:, cols]
        ac = cw[0:1] * am2 + cw[1:2] * am1 + cw[2:3] * a + cb_ref[:, cols]
        return (_gelu_tanh(ac) * gate).astype(BF16)

    width = tf // FFN_CHUNKS
    contrib = None
    for c in range(FFN_CHUNKS):
        cols = slice(c * width, (c + 1) * width)
        part = _dot(gated_chunk(cols), wd_ref[cols, :])
        contrib = part if contrib is None else contrib + part
    o_ref[...] += contrib

    @pl.when(f == pl.num_programs(1) - 1)
    def _():
        o_ref[...] = h_ref[...] + _rms_rows(o_ref[...], gp_ref[...])


def _ffn(h, g2, w_up, conv_w, conv_b, w_d, g_post, edges, seq, tm, tf):
    m, d = h.shape
    dff = w_d.shape[0]
    nf = dff // tf
    has_edges = edges is not None
    row = lambda i, f: (i, 0)
    col = lambda i, f: (0, f)
    const2 = lambda i, f: (0, 0)
    in_specs = [
        pl.BlockSpec((tm, d), row),
        pl.BlockSpec((1, d), const2),
        pl.BlockSpec((d, tf), col),
        pl.BlockSpec((d, tf), lambda i, f: (0, nf + f)),
        pl.BlockSpec((CONV_WIDTH, tf), col),
        pl.BlockSpec((1, tf), col),
        pl.BlockSpec((tf, d), lambda i, f: (f, 0)),
        pl.BlockSpec((1, d), const2),
    ]
    args = [h, g2, w_up, w_up, conv_w, conv_b, w_d, g_post]
    if has_edges:
        assert tm % seq == 0
        tails_per_tile = tm // seq
        in_specs.append(pl.BlockSpec((tails_per_tile, CONV_WIDTH - 1, tf), lambda i, f: (i, 0, f)))
        args.append(edges)
    else:
        assert seq % tm == 0
        tails_per_tile = 1
    n_tails = (m // tm) * tails_per_tile
    out, tails = pl.pallas_call(
        functools.partial(_ffn_kernel, seq=seq, has_edges=has_edges),
        grid=(m // tm, nf),
        in_specs=in_specs,
        out_specs=[pl.BlockSpec((tm, d), row),
                   pl.BlockSpec((tails_per_tile, CONV_WIDTH - 1, tf), lambda i, f: (i, 0, f))],
        out_shape=[jax.ShapeDtypeStruct((m, d), F32),
                   jax.ShapeDtypeStruct((n_tails, CONV_WIDTH - 1, dff), F32)],
        scratch_shapes=[pltpu.VMEM((tm, d), BF16), pltpu.VMEM((nf, SUBLANES, tf), F32)],
        compiler_params=_params("arbitrary", "arbitrary"),
        name="conv_ffn",
    )(*args)
    if not has_edges:
        tiles_per_seq = seq // tm
        tails = tails[tiles_per_seq - 1::tiles_per_seq]
    return out, tails


def _tile(n, cap):
    t = min(n, cap)
    assert n % t == 0
    return t


def _layer(layer, hp, hs, past, rel_table, lam_params, lam_init, p):
    (pre1, w_in, b_forget, subln, w_bd, w_bf, w_out, post1, pre2, w_up, conv_w, conv_b, w_down, post2) = p
    batch, seq, d = hp.shape
    nb, t_new, _ = hs.shape
    n_heads = d // (2 * HEAD_DIM)
    dw = n_heads * HEAD_DIM
    dff = w_down.shape[0]
    caches, plogf, conv_prev = past
    plen = plogf.shape[1]

    row2 = lambda v: v.reshape(1, -1).astype(F32)
    w_in_b = w_in.astype(BF16)
    b_f = jnp.pad(b_forget.astype(F32), (0, LANES - n_heads)).reshape(1, LANES)
    w_ga = w_in_b[:, 6 * dw + n_heads:6 * dw + n_heads + d]
    w_gb = w_in_b[:, 6 * dw + n_heads + d:]
    w_bd_b, w_bf_b, w_out_b = w_bd.astype(BF16), w_bf.astype(BF16), w_out.astype(BF16)
    w_up_b = w_up.astype(BF16)
    w_d = w_down.astype(BF16)
    subln2 = row2(subln)

    def dense_tail(x2, od, of, edges, t_seq):
        m = x2.shape[0]
        h1 = _merge(x2, row2(pre1), od, of, w_bd_b, w_bf_b, w_ga, w_gb, w_out_b, row2(post1),
                    _tile(m, MERGE_ROWS), _tile(d, 1024))
        return _ffn(h1, row2(pre2), w_up_b, conv_w.astype(F32), row2(conv_b), w_d, row2(post2),
                    edges, t_seq, _tile(m, 512), _tile(dff, 1024))

    heads = lambda a, n, t: a.reshape(n, t, n_heads, HEAD_DIM)

    xp = hp.reshape(batch * seq, d)
    zb, kd, vd, kf, vf, logf, fcols = _inproj(xp, row2(pre1), w_in_b, b_f, n_heads,
                                              _tile(seq, INPROJ_ROWS), seq=seq)
    od, of = _prompt_attention(zb, fcols, rel_table, lam_params, subln2, lam_init,
                               batch, seq, n_heads, _tile(seq, ATTN_QUERY_TILE), _tile(seq, ATTN_KEY_TILE))
    hp_out, conv_p = dense_tail(xp, od, of, None, seq)
    state_p = (heads(kd, batch, seq), heads(vd, batch, seq), heads(kf, batch, seq), heads(vf, batch, seq),
               logf.reshape(batch, seq, n_heads), conv_p)

    xs = hs.reshape(nb * t_new, d)
    zb, kd, vd, kf, vf, logf = _inproj(xs, row2(pre1), w_in_b, b_f, n_heads,
                                       _tile(nb * t_new, INPROJ_ROWS))
    tpad = -(-(plen + t_new) // LANES) * LANES
    flog = jnp.concatenate([jnp.swapaxes(plogf.astype(F32), 1, 2),
                            jnp.swapaxes(logf.reshape(nb, t_new, n_heads), 1, 2),
                            jnp.zeros((nb, n_heads, tpad - plen - t_new), F32)], axis=2)
    frow = _cumsum_rows(flog.reshape(nb * n_heads, tpad)).reshape(nb, n_heads, tpad)
    od, of = _sample_attention(zb, caches, layer, frow, rel_table, lam_params, subln2, lam_init,
                               nb, t_new, plen, n_heads)
    hs_out, conv_s = dense_tail(xs, od, of, conv_prev.astype(F32), t_new)
    state_s = (heads(kd, nb, t_new), heads(vd, nb, t_new), heads(kf, nb, t_new), heads(vf, nb, t_new),
               logf.reshape(nb, t_new, n_heads), conv_s)
    return hp_out.reshape(batch, seq, d), hs_out.reshape(nb, t_new, d), state_p, state_s


def kernel(x_prompt, x_sample, cache_diff_k, cache_diff_v, cache_fox_k, cache_fox_v, cache_fox_logf,
           state_ffn_conv, rel_table, pre_norm1, w_in, b_forget, lam_q1, lam_k1, lam_q2, lam_k2,
           diff_subln, w_branch_diff, w_branch_fox, w_out, post_norm1, pre_norm2, w_up, conv_w, conv_b,
           w_down, post_norm2):
    depth = w_in.shape[0]
    hp, hs = x_prompt, x_sample
    caches = (cache_diff_k, cache_diff_v, cache_fox_k, cache_fox_v)
    new_p, new_s = [], []
    for l in range(depth):
        lam_init = 0.8 - 0.6 * math.exp(-0.3 * l)
        lam_params = jnp.stack([lam_q1[l], lam_k1[l], lam_q2[l], lam_k2[l]]).astype(F32)
        params = (pre_norm1[l], w_in[l], b_forget[l], diff_subln[l], w_branch_diff[l], w_branch_fox[l],
                  w_out[l], post_norm1[l], pre_norm2[l], w_up[l], conv_w[l], conv_b[l], w_down[l],
                  post_norm2[l])
        past = (caches, cache_fox_logf[l], state_ffn_conv[l])
        hp, hs, sp, ss = _layer(l, hp, hs, past, rel_table, lam_params, lam_init, params)
        new_p.append(sp)
        new_s.append(ss)
    st = lambda lst, i: jnp.stack([e[i] for e in lst])
    return (hp, hs,
            st(new_p, 0), st(new_p, 1), st(new_p, 2), st(new_p, 3), st(new_p, 4), st(new_p, 5),
            st(new_s, 0), st(new_s, 1), st(new_s, 2), st(new_s, 3), st(new_s, 4), st(new_s, 5))
```

```python
import functools
import math

import jax
import jax.numpy as jnp
from jax import lax
from jax.experimental import pallas as pl
from jax.experimental.pallas import tpu as pltpu

HEAD_DIM = 128
DIFF_QK_DIM = HEAD_DIM // 2
CHUNK = 64
CONV_WIDTH = 3
REL_BUCKETS = 32
REL_MAX_DIST = 128
EPS = 1e-6
NEG_INF = -1e30
LOG2E = math.log2(math.e)

LANES = 128
SUBLANES = 8
ONES_ROWS = 16
ATTN_QUERY_TILE = 512
ATTN_KEY_TILE = 512
FAR_UNROLL = 2
FFN_CHUNKS = 2
MERGE_ROWS = 256
INPROJ_ROWS = 256
VMEM_LIMIT_BYTES = 60 * 1024 * 1024

F32 = jnp.float32
BF16 = jnp.bfloat16


def _params(*sem, flags=None):
    return pltpu.CompilerParams(dimension_semantics=sem, vmem_limit_bytes=VMEM_LIMIT_BYTES, flags=flags)


def _rms_rows(x, g):
    return x * lax.rsqrt(jnp.mean(x * x, axis=-1, keepdims=True) + EPS) * g


def _dot(a, b):
    return jnp.dot(a, b, preferred_element_type=F32)


def _dot_nt(a, b):
    return lax.dot_general(a, b, (((1,), (1,)), ((), ())), preferred_element_type=F32)


def _split3(x):
    hi = x.astype(BF16)
    r1 = x - hi.astype(F32)
    mid = r1.astype(BF16)
    lo = (r1 - mid.astype(F32)).astype(BF16)
    return hi, mid, lo


def _split3_on_axis(x, index, ones_after=False):
    hi, mid, lo = (v.astype(F32) for v in _split3(x))
    rest = jnp.where(index < 6, 1.0, 0.0) if ones_after else 0.0
    return jnp.where(index == 0, hi, jnp.where(index == 1, mid, jnp.where(index == 2, lo, rest))).astype(BF16)


def _transpose_bf16(x):
    return x.astype(F32).T.astype(BF16)


def _inproj_kernel(x_ref, g_ref, w_ref, wf_ref, bf_ref,
                   zb_ref, kd_ref, vd_ref, kf_ref, vf_ref, logf_ref, *rest, n_heads, q_scales, seq):
    dw = kd_ref.shape[1]
    tm = x_ref.shape[0]
    xn = _rms_rows(x_ref[...], g_ref[...]).astype(BF16)
    fl = _dot(xn, wf_ref[...]) + bf_ref[...]
    lf = jnp.minimum(fl, 0.0) - jnp.log1p(jnp.exp(-jnp.abs(fl)))
    logf_ref[...] = lf[:, :n_heads]
    if seq is not None:
        fcol_ref, carry_ref = rest

        @pl.when((pl.program_id(0) * tm) % seq == 0)
        def _():
            carry_ref[...] = jnp.zeros_like(carry_ref)

        hi, mid, lo = _split3(lf)
        tri = _triangle(tm, lower=True)
        local = _dot(tri, hi) + _dot(tri, mid) + _dot(tri, lo)
        f = (local + carry_ref[...]) * LOG2E
        carry_ref[...] = carry_ref[...] + local[tm - 1:tm, :]
        lane = lax.broadcasted_iota(jnp.int32, (tm, LANES), 1)
        for h in range(n_heads):
            fcol_ref[h] = _split3_on_axis(jnp.broadcast_to(f[:, h:h + 1], (tm, LANES)), lane, ones_after=True)
    f32_outs = {1: kd_ref, 2: vd_ref, 4: kf_ref, 5: vf_ref}
    for seg in range(6):
        cols = slice(seg * dw, (seg + 1) * dw)
        z = _dot(xn, w_ref[:, cols])
        if seg in f32_outs:
            f32_outs[seg][...] = z
        else:
            z = z * q_scales[seg]
        zb_ref[:, cols] = z.astype(BF16)


def _inproj(x, g, w_in, b_f, n_heads, tm, seq=None):
    m, d = x.shape
    dw = n_heads * HEAD_DIM
    assert (6 * dw) % LANES == 0
    row = lambda i: (i, 0)
    const2 = lambda i: (0, 0)
    f32_out = jax.ShapeDtypeStruct((m, dw), F32)
    q_scales = {0: DIFF_QK_DIM ** -0.5 * LOG2E, 3: HEAD_DIM ** -0.5 * LOG2E}
    extra_specs, extra_shapes, scratch = [], [], []
    if seq is not None:
        assert seq % tm == 0
        tiles = seq // tm
        extra_specs = [pl.BlockSpec((None, n_heads, tm, LANES), lambda i: (i // tiles, 0, i % tiles, 0))]
        extra_shapes = [jax.ShapeDtypeStruct((m // seq, n_heads, seq, LANES), BF16)]
        scratch = [pltpu.VMEM((1, LANES), F32)]
    return pl.pallas_call(
        functools.partial(_inproj_kernel, n_heads=n_heads, q_scales=q_scales, seq=seq),
        grid=(m // tm,),
        in_specs=[
            pl.BlockSpec((tm, d), row),
            pl.BlockSpec((1, d), const2),
            pl.BlockSpec((d, 6 * dw), const2, pipeline_mode=pl.Buffered(1)),
            pl.BlockSpec((d, LANES), lambda i: (0, 6 * dw // LANES)),
            pl.BlockSpec((1, LANES), const2),
        ],
        out_specs=[
            pl.BlockSpec((tm, 6 * dw), row),
            pl.BlockSpec((tm, dw), row),
            pl.BlockSpec((tm, dw), row),
            pl.BlockSpec((tm, dw), row),
            pl.BlockSpec((tm, dw), row),
            pl.BlockSpec((tm, n_heads), row),
        ] + extra_specs,
        out_shape=[
            jax.ShapeDtypeStruct((m, 6 * dw), BF16),
            f32_out, f32_out, f32_out, f32_out,
            jax.ShapeDtypeStruct((m, n_heads), F32),
        ] + extra_shapes,
        scratch_shapes=scratch,
        compiler_params=_params("arbitrary"),
        name="inproj",
    )(x, g, w_in, w_in, b_f)


def _triangle(n, lower):
    r = lax.broadcasted_iota(jnp.int32, (n, n), 0)
    c = lax.broadcasted_iota(jnp.int32, (n, n), 1)
    return ((r >= c) if lower else (r <= c)).astype(BF16)


def _cumsum_rows_kernel(x_ref, o_ref):
    t = x_ref.shape[1]
    tri = _triangle(LANES, lower=False)
    carry = jnp.zeros((x_ref.shape[0], 1), F32)
    for k in range(t // LANES):
        cols = slice(k * LANES, (k + 1) * LANES)
        hi, mid, lo = _split3(x_ref[:, cols])
        local = _dot(hi, tri) + _dot(mid, tri) + _dot(lo, tri)
        o_ref[:, cols] = local + carry
        carry = carry + local[:, LANES - 1:LANES]


def _cumsum_rows(x):
    return pl.pallas_call(
        _cumsum_rows_kernel,
        out_shape=jax.ShapeDtypeStruct(x.shape, F32),
        compiler_params=_params(),
        name="cumsum_rows",
    )(x)


def _t5_bucket(rel):
    nb = REL_BUCKETS // 2
    max_exact = nb // 2
    n = jnp.abs(rel)
    nf = jnp.maximum(n, 1).astype(jnp.float32)
    large = max_exact + (jnp.log(nf / max_exact) / math.log(REL_MAX_DIST / max_exact)
                         * (nb - max_exact)).astype(jnp.int32)
    large = jnp.minimum(large, nb - 1)
    return jnp.where(rel > 0, nb, 0) + jnp.where(n < max_exact, n, large)


def _bias_kernel(table_ref, idx_ref, mask_ref, o_ref, *, minus_bucket):
    h = pl.program_id(1)
    idx = idx_ref[...]
    acc = mask_ref[...]
    if minus_bucket is not None:
        acc = acc - table_ref[minus_bucket, h]
    vals = [table_ref[b, h] for b in range(REL_BUCKETS)]
    bit = 1
    while len(vals) > 1:
        odd = (idx & bit) != 0
        vals = [jnp.where(odd, vals[2 * i + 1], vals[2 * i]) for i in range(len(vals) // 2)]
        bit *= 2
    o_ref[...] = (acc + vals[0]) * LOG2E


def _bias_tiles(rel_table, qpos, kpos, keys_major=False, minus_bucket=None):
    n_heads = rel_table.shape[1]
    if keys_major:
        qp, kp = qpos[:, None, :], kpos[:, :, None]
    else:
        qp, kp = qpos[:, :, None], kpos[:, None, :]
    idx = _t5_bucket(kp - qp).astype(jnp.int32)
    mask = jnp.where((kp // CHUNK) <= (qp // CHUNK), 0.0, NEG_INF).astype(F32)
    nt, r, c = idx.shape
    return pl.pallas_call(
        functools.partial(_bias_kernel, minus_bucket=minus_bucket),
        grid=(nt, n_heads),
        in_specs=[
            pl.BlockSpec(memory_space=pltpu.SMEM),
            pl.BlockSpec((None, r, c), lambda t, h: (t, 0, 0)),
            pl.BlockSpec((None, r, c), lambda t, h: (t, 0, 0)),
        ],
        out_specs=pl.BlockSpec((None, None, r, c), lambda t, h: (h, t, 0, 0)),
        out_shape=jax.ShapeDtypeStruct((n_heads, nt, r, c), F32),
        compiler_params=_params("arbitrary", "arbitrary"),
        name="bias_tiles",
    )(rel_table.astype(F32), idx, mask)


def _far_bucket(min_dist):
    nb = REL_BUCKETS // 2
    max_exact = nb // 2
    large = max_exact + math.log(min_dist / max_exact) / math.log(REL_MAX_DIST / max_exact) * (nb - max_exact)
    assert large >= nb - 1 + 0.5, "key tile too short for a constant far-field bias"
    return nb - 1


def _lambda_value(lam_ref, lam_init):
    a = lam_ref[...]
    s1 = jnp.sum(a[0:1] * a[1:2], axis=-1, keepdims=True)
    s2 = jnp.sum(a[2:3] * a[3:4], axis=-1, keepdims=True)
    return jnp.exp(s1) - jnp.exp(s2) + lam_init


def _prompt_attn_kernel(table_ref, lam_ref, subln_ref, qd_ref, kd_ref, vd_ref, qf_ref, kf_ref, vf_ref,
                        fcol_ref, bias_ref, cmask_ref, qdn_ref, qfn_ref, od_ref, of_ref,
                        kda_ref, kfa_ref, vdt_ref, vft_ref, qda_ref, qfa_ref,
                        md_ref, accd_ref, mf_ref, accf_ref,
                        sd_ref, pd_ref, ad_ref, sf_ref, pf_ref, af_ref, cd_ref, cf_ref,
                        *, tq, tk, lam_init, far_bucket):
    h = pl.program_id(1)
    qi = pl.program_id(2)
    seq = kd_ref.shape[0]
    ratio = tq // tk

    def prepare_keys_values():
        lane = lax.broadcasted_iota(jnp.int32, (seq, LANES), 1)
        kda_ref[:, :HEAD_DIM] = kd_ref[...]
        kda_ref[:, HEAD_DIM:] = jnp.where(lane < 3, 1.0, 0.0).astype(BF16)
        kfa_ref[:, :HEAD_DIM] = kf_ref[...]
        kfa_ref[:, HEAD_DIM:] = fcol_ref[...]

        ones = jnp.ones((ONES_ROWS, tk), BF16)

        def transpose_values(c, carry):
            r = pl.ds(pl.multiple_of(c * tk, tk), tk)
            vdt_ref[c] = jnp.concatenate([_transpose_bf16(vd_ref[r, :]), ones], axis=0)
            vft_ref[c] = jnp.concatenate([_transpose_bf16(vf_ref[r, :]), ones], axis=0)
            return carry

        lax.fori_loop(0, seq // tk, transpose_values, 0)

    def key_rows(ref, j):
        return ref[pl.ds(pl.multiple_of(j * tk, tk), tk), :]

    def buffer_scores(j, k_ref, q_aug, s_ref, c_ref, add=None):
        s = _dot(key_rows(k_ref, j), q_aug)
        if add is not None:
            s = s + add
        s_ref[...] = s
        c_ref[...] = jnp.max(s, axis=0, keepdims=True)

    def prepare_queries(qd_src_ref, qf_src_ref, tile):
        row = lax.broadcasted_iota(jnp.int32, (HEAD_DIM, tq), 0)
        qdt = qd_src_ref[...].astype(F32).T
        zero = jnp.zeros_like(qdt)
        top = jnp.concatenate([jnp.where(row < DIFF_QK_DIM, qdt, zero),
                               jnp.where(row >= DIFF_QK_DIM, qdt, zero)], axis=1).astype(BF16)
        row2 = lax.broadcasted_iota(jnp.int32, (HEAD_DIM, 2 * tq), 0)
        bottom = _split3_on_axis(jnp.full((HEAD_DIM, 2 * tq), table_ref[far_bucket, h] * LOG2E, F32), row2)
        qd_aug = jnp.concatenate([top, bottom], axis=0)
        fq = fcol_ref[pl.ds(pl.multiple_of(tile * tq, tq), tq), :].astype(F32).T
        extra = jnp.where(row < 3, -1.0, jnp.where(row < 6, pltpu.roll(fq, 3, 0), 0.0))
        qf_aug = jnp.concatenate([qf_src_ref[...].astype(F32).T.astype(BF16),
                                  extra.astype(BF16)], axis=0)
        qda_ref[...] = qd_aug
        qfa_ref[...] = qf_aug
        buffer_scores(0, kda_ref, qd_aug, sd_ref, cd_ref)
        buffer_scores(0, kfa_ref, qf_aug, sf_ref, cf_ref)

    @pl.when(qi == 0)
    def _():
        prepare_keys_values()
        prepare_queries(qd_ref, qf_ref, 0)

    branches = ((kda_ref, qda_ref, vdt_ref, sd_ref, pd_ref, ad_ref, md_ref, accd_ref, cd_ref),
                (kfa_ref, qfa_ref, vft_ref, sf_ref, pf_ref, af_ref, mf_ref, accf_ref, cf_ref))

    def pipeline_step(j_prev, adds, j_next, next_adds=(None, None)):
        for (k_ref, q_ref, vt_ref, s_ref, p_ref, a_ref, m_ref, acc_ref, c_ref), add, next_add in zip(
                branches, adds, next_adds):
            acc_ref[...] = a_ref[...] * acc_ref[...] + _dot(vt_ref[j_prev], p_ref[...])
            s = s_ref[...]
            if add is None:
                s_max = c_ref[...]
            else:
                s = s + add
                s_max = jnp.max(s, axis=0, keepdims=True)
            m_prev = m_ref[...]
            m_new = jnp.maximum(m_prev, s_max)
            p_ref[...] = jnp.exp2(s - m_new).astype(BF16)
            a_ref[...] = jnp.exp2(m_prev - m_new)
            m_ref[...] = m_new
            if j_next is not None:
                buffer_scores(j_next, k_ref, q_ref[...], s_ref, c_ref, next_add)

    for k_ref, q_ref, vt_ref, s_ref, p_ref, a_ref, m_ref, acc_ref, c_ref in branches:
        m_ref[...] = jnp.full_like(m_ref, NEG_INF)
        acc_ref[...] = jnp.zeros_like(acc_ref)
        p_ref[...] = jnp.zeros_like(p_ref)
        a_ref[...] = jnp.ones_like(a_ref)

    def far_tiles(unroll):
        def body(i, carry):
            for u in range(unroll):
                j = i * unroll + u
                pipeline_step(jnp.maximum(j - 1, 0), (None, None), j + 1)
            return carry
        return body

    first_diag = qi * ratio
    n_far = jnp.maximum(first_diag - 1, 0)
    n_main = n_far // FAR_UNROLL
    lax.fori_loop(0, n_main, far_tiles(FAR_UNROLL), 0)
    lax.fori_loop(n_main * FAR_UNROLL, n_far, far_tiles(1), 0)

    near = jnp.maximum(first_diag - 1, 0)
    gone = jnp.where(qi == 0, NEG_INF, 0.0)

    def diag_adds(u):
        bias = bias_ref[u + 1]
        return jnp.concatenate([bias, bias], axis=1), cmask_ref[u]

    bias = bias_ref[0] + gone
    pipeline_step(jnp.maximum(near - 1, 0), (jnp.concatenate([bias, bias], axis=1), gone),
                  first_diag, diag_adds(0))
    for u in range(ratio):
        more = u + 1 < ratio
        pipeline_step(near if u == 0 else first_diag + u - 1, (None, None),
                      first_diag + u + 1 if more else None, diag_adds(u + 1) if more else (None, None))
    last = first_diag + ratio - 1
    for k_ref, q_ref, vt_ref, s_ref, p_ref, a_ref, m_ref, acc_ref, c_ref in branches:
        acc_ref[...] = a_ref[...] * acc_ref[...] + _dot(vt_ref[last], p_ref[...])

    prepare_queries(qdn_ref, qfn_ref, jnp.minimum(qi + 1, pl.num_programs(2) - 1))

    lam = _lambda_value(lam_ref, lam_init)
    acc = accd_ref[...]
    num, inv = acc[:HEAD_DIM], 1.0 / acc[HEAD_DIM:HEAD_DIM + 1]
    odt = num[:, :tq] * inv[:, :tq] - lam * (num[:, tq:] * inv[:, tq:])
    od_ref[...] = (_rms_rows(odt.T, subln_ref[...]) * (1.0 - lam_init)).astype(BF16)
    acc = accf_ref[...]
    of_ref[...] = (acc[:HEAD_DIM] * (1.0 / acc[HEAD_DIM:HEAD_DIM + 1])).T.astype(BF16)


def _prompt_attention(zb, fcols, rel_table, lam_params, subln, lam_init, batch, seq, n_heads, tq, tk):
    m = batch * seq
    dw = n_heads * HEAD_DIM
    nq = seq // tq
    ratio = tq // tk
    assert tq == ratio * tk
    far_bucket = _far_bucket(tk + 1)
    vrows = HEAD_DIM + ONES_ROWS
    qpos = tk + jnp.arange(tq, dtype=jnp.int32)
    kpos = jnp.arange((ratio + 1) * tk, dtype=jnp.int32).reshape(ratio + 1, tk)
    bias = _bias_tiles(rel_table, jnp.broadcast_to(qpos, (ratio + 1, tq)), kpos,
                       keys_major=True, minus_bucket=far_bucket)
    cmask = jnp.where(kpos[1:, :, None] <= qpos[None, None, :], 0.0, NEG_INF).astype(F32)

    def q_spec(seg, ahead=0):
        return pl.BlockSpec((tq, HEAD_DIM), lambda b, h, q, seg=seg: (b * nq + jnp.minimum(q + ahead, nq - 1),
                                                                      seg * n_heads + h))

    def kv_spec(seg):
        return pl.BlockSpec((seq, HEAD_DIM), lambda b, h, q, seg=seg: (b, seg * n_heads + h))

    out_spec = pl.BlockSpec((tq, HEAD_DIM), lambda b, h, q: (b * nq + q, h))
    const2 = lambda b, h, q: (0, 0)
    kern = functools.partial(_prompt_attn_kernel, tq=tq, tk=tk, lam_init=lam_init, far_bucket=far_bucket)
    return pl.pallas_call(
        kern,
        grid=(batch, n_heads, nq),
        in_specs=[
            pl.BlockSpec(memory_space=pltpu.SMEM),
            pl.BlockSpec((4, DIFF_QK_DIM), const2),
            pl.BlockSpec((1, HEAD_DIM), const2),
            q_spec(0), kv_spec(1), kv_spec(2), q_spec(3), kv_spec(4), kv_spec(5),
            pl.BlockSpec((None, None, seq, LANES), lambda b, h, q: (b, h, 0, 0)),
            pl.BlockSpec((None, ratio + 1, tk, tq), lambda b, h, q: (h, 0, 0, 0)),
            pl.BlockSpec((ratio, tk, tq), lambda b, h, q: (0, 0, 0)),
            q_spec(0, ahead=1), q_spec(3, ahead=1),
        ],
        out_specs=[out_spec, out_spec],
        out_shape=[jax.ShapeDtypeStruct((m, dw), BF16), jax.ShapeDtypeStruct((m, dw), BF16)],
        scratch_shapes=[
            pltpu.VMEM((seq, 2 * HEAD_DIM), BF16), pltpu.VMEM((seq, 2 * HEAD_DIM), BF16),
            pltpu.VMEM((seq // tk, vrows, tk), BF16), pltpu.VMEM((seq // tk, vrows, tk), BF16),
            pltpu.VMEM((2 * HEAD_DIM, 2 * tq), BF16), pltpu.VMEM((2 * HEAD_DIM, tq), BF16),
            pltpu.VMEM((1, 2 * tq), F32), pltpu.VMEM((vrows, 2 * tq), F32),
            pltpu.VMEM((1, tq), F32), pltpu.VMEM((vrows, tq), F32),
            pltpu.VMEM((tk, 2 * tq), F32), pltpu.VMEM((tk, 2 * tq), BF16), pltpu.VMEM((1, 2 * tq), F32),
            pltpu.VMEM((tk, tq), F32), pltpu.VMEM((tk, tq), BF16), pltpu.VMEM((1, tq), F32),
            pltpu.VMEM((1, 2 * tq), F32), pltpu.VMEM((1, tq), F32),
        ],
        compiler_params=_params("arbitrary", "arbitrary", "arbitrary"),
        name="prompt_attention",
    )(rel_table.astype(F32), lam_params, subln, zb, zb, zb, zb, zb, zb, fcols, bias, cmask, zb, zb)


def _split_diff_queries(q):
    lane = lax.broadcasted_iota(jnp.int32, q.shape, 1)
    zero = jnp.zeros_like(q)
    return jnp.concatenate([jnp.where(lane < DIFF_QK_DIM, q, zero),
                            jnp.where(lane >= DIFF_QK_DIM, q, zero)], axis=0)


def _sample_attn_kernel(lam_ref, subln_ref, z_ref, kdc_ref, vdc_ref, kfc_ref, vfc_ref, frow_ref, bias_ref,
                        cmask_ref, od_ref, of_ref, *, past, t_new, n_heads, lam_init):
    dw = n_heads * HEAD_DIM

    def cached(ref, h):
        return ref[pl.ds(h, past, stride=n_heads), :].astype(BF16)

    def new_rows(seg, h):
        lo = seg * dw + h * HEAD_DIM
        return z_ref[:, lo:lo + HEAD_DIM]

    def attend(q, kc, vc, kn, vn, add_c, add_n):
        sc = _dot_nt(q, kc) + add_c
        sn = _dot_nt(q, kn) + add_n
        mx = jnp.maximum(jnp.max(sc, axis=-1, keepdims=True), jnp.max(sn, axis=-1, keepdims=True))
        pc = jnp.exp2(sc - mx)
        pn = jnp.exp2(sn - mx)
        l = jnp.sum(pc, axis=-1, keepdims=True) + jnp.sum(pn, axis=-1, keepdims=True)
        acc = _dot(pc.astype(BF16), vc) + _dot(pn.astype(BF16), vn)
        return acc, l

    lam = _lambda_value(lam_ref, lam_init)
    cmask = cmask_ref[...]
    eye = (lax.broadcasted_iota(jnp.int32, (t_new, t_new), 0)
           == lax.broadcasted_iota(jnp.int32, (t_new, t_new), 1))
    for h in range(n_heads):
        cols = slice(h * HEAD_DIM, (h + 1) * HEAD_DIM)
        qs = _split_diff_queries(new_rows(0, h))
        bias_c = bias_ref[h, :, :past]
        bias_n = bias_ref[h, :, past:past + t_new]
        acc, l = attend(qs, cached(kdc_ref, h), cached(vdc_ref, h),
                        new_rows(1, h), new_rows(2, h),
                        jnp.concatenate([bias_c, bias_c], axis=0), jnp.concatenate([bias_n, bias_n], axis=0))
        od = acc[:t_new] / l[:t_new] - lam * (acc[t_new:] / l[t_new:])
        od_ref[:, cols] = (_rms_rows(od, subln_ref[...]) * (1.0 - lam_init)).astype(BF16)

        f_new = frow_ref[h:h + 1, past:past + t_new]
        fq = jnp.sum(jnp.where(eye, jnp.broadcast_to(f_new, (t_new, t_new)), 0.0), axis=1, keepdims=True)
        dec_c = (fq - frow_ref[h:h + 1, :past]) * LOG2E
        dec_n = (fq - f_new) * LOG2E
        acc, l = attend(new_rows(3, h), cached(kfc_ref, h), cached(vfc_ref, h),
                        new_rows(4, h), new_rows(5, h), dec_c, dec_n + cmask)
        of_ref[:, cols] = (acc / l).astype(BF16)


def _sample_attention(zb, caches, layer, frow, rel_table, lam_params, subln, lam_init, nb, t_new, past, n_heads):
    dw = n_heads * HEAD_DIM
    tpad = frow.shape[-1]
    ar = jnp.arange(t_new, dtype=jnp.int32)
    bias = _bias_tiles(rel_table, (past + ar)[None], jnp.arange(tpad, dtype=jnp.int32)[None])[:, 0]
    cmask = jnp.where(ar[None, :] <= ar[:, None], 0.0, NEG_INF).astype(F32)
    caches = [c.reshape(c.shape[0], nb, past * n_heads, HEAD_DIM) for c in caches]
    cache_spec = pl.BlockSpec((None, None, past * n_heads, HEAD_DIM), lambda b: (layer, b, 0, 0))
    out_spec = pl.BlockSpec((t_new, dw), lambda b: (b, 0))
    const2 = lambda b: (0, 0)
    kern = functools.partial(_sample_attn_kernel, past=past, t_new=t_new, n_heads=n_heads, lam_init=lam_init)
    return pl.pallas_call(
        kern,
        grid=(nb,),
        in_specs=[
            pl.BlockSpec((4, DIFF_QK_DIM), const2),
            pl.BlockSpec((1, HEAD_DIM), const2),
            pl.BlockSpec((t_new, 6 * dw), lambda b: (b, 0)),
            cache_spec, cache_spec, cache_spec, cache_spec,
            pl.BlockSpec((None, n_heads, tpad), lambda b: (b, 0, 0)),
            pl.BlockSpec((n_heads, t_new, tpad), lambda b: (0, 0, 0)),
            pl.BlockSpec((t_new, t_new), const2),
        ],
        out_specs=[out_spec, out_spec],
        out_shape=[jax.ShapeDtypeStruct((nb * t_new, dw), BF16), jax.ShapeDtypeStruct((nb * t_new, dw), BF16)],
        compiler_params=_params("arbitrary"),
        name="sample_attention",
    )(lam_params, subln, zb, *caches, frow, bias, cmask)


def _merge_kernel(x_ref, g1_ref, od_ref, of_ref, wbd_ref, wbf_ref, wga_ref, wgb_ref, wout_ref, gp_ref,
                  o_ref, *, chunk):
    x = x_ref[...]
    xn = _rms_rows(x, g1_ref[...]).astype(BF16)
    od = od_ref[...]
    of = of_ref[...]
    o = None
    for c in range(x.shape[1] // chunk):
        cols = slice(c * chunk, (c + 1) * chunk)
        ga = jax.nn.sigmoid(_dot(xn, wga_ref[:, cols]))
        gb = jax.nn.sigmoid(_dot(xn, wgb_ref[:, cols]))
        u = (ga * _dot(od, wbd_ref[:, cols]) + gb * _dot(of, wbf_ref[:, cols])).astype(BF16)
        part = _dot(u, wout_ref[cols, :])
        o = part if o is None else o + part
    o_ref[...] = x + _rms_rows(o, gp_ref[...])


def _merge(x, g1, od, of, w_bd, w_bf, w_ga, w_gb, w_out, g_post, tm, chunk):
    m, d = x.shape
    dw = od.shape[1]
    row = lambda i: (i, 0)
    const2 = lambda i: (0, 0)
    resident = lambda shape: pl.BlockSpec(shape, const2, pipeline_mode=pl.Buffered(1))
    return pl.pallas_call(
        functools.partial(_merge_kernel, chunk=chunk),
        grid=(m // tm,),
        in_specs=[
            pl.BlockSpec((tm, d), row),
            pl.BlockSpec((1, d), const2),
            pl.BlockSpec((tm, dw), row),
            pl.BlockSpec((tm, dw), row),
            resident((dw, d)), resident((dw, d)), resident((d, d)), resident((d, d)), resident((d, d)),
            pl.BlockSpec((1, d), const2),
        ],
        out_specs=pl.BlockSpec((tm, d), row),
        out_shape=jax.ShapeDtypeStruct((m, d), F32),
        compiler_params=_params("arbitrary"),
        name="merge",
    )(x, g1, od, of, w_bd, w_bf, w_ga, w_gb, w_out, g_post)


def _gelu_tanh(x):
    k = -2.0 * math.sqrt(2.0 / math.pi) * LOG2E
    return x / (1.0 + jnp.exp2(x * (k + (k * 0.044715) * (x * x))))


def _ffn_kernel(*refs, seq, has_edges):
    if has_edges:
        (h_ref, g2_ref, wa_ref, wb_ref, cw_ref, cb_ref, wd_ref, gp_ref, prev_ref,
         o_ref, cs_ref, xn_ref, tail_ref) = refs
    else:
        (h_ref, g2_ref, wa_ref, wb_ref, cw_ref, cb_ref, wd_ref, gp_ref,
         o_ref, cs_ref, xn_ref, tail_ref) = refs
    i = pl.program_id(0)
    f = pl.program_id(1)
    tm = h_ref.shape[0]

    tf = wa_ref.shape[1]

    @pl.when(f == 0)
    def _():
        xn_ref[...] = _rms_rows(h_ref[...], g2_ref[...]).astype(BF16)
        o_ref[...] = jnp.zeros_like(o_ref)

    if not has_edges:
        @pl.when((i * tm) % seq == 0)
        def _():
            tail_ref[f] = jnp.zeros(tail_ref.shape[1:], F32)

    xn = xn_ref[...]

    def gated_chunk(cols):
        width = cols.stop - cols.start
        a = _dot(xn, wa_ref[:, cols])
        gate = _dot(xn, wb_ref[:, cols])
        back1 = pltpu.roll(a, 1, 0)
        back2 = pltpu.roll(a, 2, 0)
        if has_edges:
            shape3 = (tm // seq, seq, width)
            t = lax.broadcasted_iota(jnp.int32, shape3, 1)
            prev = prev_ref[:, :, cols]
            p0, p1 = prev[:, 0:1, :], prev[:, 1:2, :]
            am1 = jnp.where(t == 0, p1, back1.reshape(shape3)).reshape(tm, width)
            am2 = jnp.where(t == 0, p0, jnp.where(t == 1, p1, back2.reshape(shape3))).reshape(tm, width)
            cs_ref[:, :, cols] = a.reshape(shape3)[:, seq - (CONV_WIDTH - 1):, :]
        else:
            prev = tail_ref[f, :, cols]
            p0 = prev[SUBLANES - 2:SUBLANES - 1]
            p1 = prev[SUBLANES - 1:SUBLANES]
            top = lax.broadcasted_iota(jnp.int32, (SUBLANES, width), 0)
            am1 = jnp.concatenate([jnp.where(top == 0, p1, back1[:SUBLANES]), back1[SUBLANES:]], axis=0)
            am2 = jnp.concatenate([jnp.where(top == 0, p0, jnp.where(top == 1, p1, back2[:SUBLANES])),
                                   back2[SUBLANES:]], axis=0)
            tail_ref[f, :, cols] = a[tm - SUBLANES:, :]
            cs_ref[:, :, cols] = a[tm - (CONV_WIDTH - 1):, :][None]
        cw = cw_ref[:, cols]
        ac = cw[0:1] * am2 + cw[1:2] * am1 + cw[2:3] * a + cb_ref[:, cols]
        return (_gelu_tanh(ac) * gate).astype(BF16)

    width = tf // FFN_CHUNKS
    contrib = None
    for c in range(FFN_CHUNKS):
        cols = slice(c * width, (c + 1) * width)
        part = _dot(gated_chunk(cols), wd_ref[cols, :])
        contrib = part if contrib is None else contrib + part
    o_ref[...] += contrib

    @pl.when(f == pl.num_programs(1) - 1)
    def _():
        o_ref[...] = h_ref[...] + _rms_rows(o_ref[...], gp_ref[...])


def _ffn(h, g2, w_up, conv_w, conv_b, w_d, g_post, edges, seq, tm, tf):
    m, d = h.shape
    dff = w_d.shape[0]
    nf = dff // tf
    has_edges = edges is not None
    row = lambda i, f: (i, 0)
    col = lambda i, f: (0, f)
    const2 = lambda i, f: (0, 0)
    in_specs = [
        pl.BlockSpec((tm, d), row),
        pl.BlockSpec((1, d), const2),
        pl.BlockSpec((d, tf), col),
        pl.BlockSpec((d, tf), lambda i, f: (0, nf + f)),
        pl.BlockSpec((CONV_WIDTH, tf), col),
        pl.BlockSpec((1, tf), col),
        pl.BlockSpec((tf, d), lambda i, f: (f, 0)),
        pl.BlockSpec((1, d), const2),
    ]
    args = [h, g2, w_up, w_up, conv_w, conv_b, w_d, g_post]
    if has_edges:
        assert tm % seq == 0
        tails_per_tile = tm // seq
        in_specs.append(pl.BlockSpec((tails_per_tile, CONV_WIDTH - 1, tf), lambda i, f: (i, 0, f)))
        args.append(edges)
    else:
        assert seq % tm == 0
        tails_per_tile = 1
    n_tails = (m // tm) * tails_per_tile
    out, tails = pl.pallas_call(
        functools.partial(_ffn_kernel, seq=seq, has_edges=has_edges),
        grid=(m // tm, nf),
        in_specs=in_specs,
        out_specs=[pl.BlockSpec((tm, d), row),
                   pl.BlockSpec((tails_per_tile, CONV_WIDTH - 1, tf), lambda i, f: (i, 0, f))],
        out_shape=[jax.ShapeDtypeStruct((m, d), F32),
                   jax.ShapeDtypeStruct((n_tails, CONV_WIDTH - 1, dff), F32)],
        scratch_shapes=[pltpu.VMEM((tm, d), BF16), pltpu.VMEM((nf, SUBLANES, tf), F32)],
        compiler_params=_params("arbitrary", "arbitrary"),
        name="conv_ffn",
    )(*args)
    if not has_edges:
        tiles_per_seq = seq // tm
        tails = tails[tiles_per_seq - 1::tiles_per_seq]
    return out, tails


def _tile(n, cap):
    t = min(n, cap)
    assert n % t == 0
    return t


def _layer(layer, hp, hs, past, rel_table, lam_params, lam_init, p):
    (pre1, w_in, b_forget, subln, w_bd, w_bf, w_out, post1, pre2, w_up, conv_w, conv_b, w_down, post2) = p
    batch, seq, d = hp.shape
    nb, t_new, _ = hs.shape
    n_heads = d // (2 * HEAD_DIM)
    dw = n_heads * HEAD_DIM
    dff = w_down.shape[0]
    caches, plogf, conv_prev = past
    plen = plogf.shape[1]

    row2 = lambda v: v.reshape(1, -1).astype(F32)
    w_in_b = w_in.astype(BF16)
    b_f = jnp.pad(b_forget.astype(F32), (0, LANES - n_heads)).reshape(1, LANES)
    w_ga = w_in_b[:, 6 * dw + n_heads:6 * dw + n_heads + d]
    w_gb = w_in_b[:, 6 * dw + n_heads + d:]
    w_bd_b, w_bf_b, w_out_b = w_bd.astype(BF16), w_bf.astype(BF16), w_out.astype(BF16)
    w_up_b = w_up.astype(BF16)
    w_d = w_down.astype(BF16)
    subln2 = row2(subln)

    def dense_tail(x2, od, of, edges, t_seq):
        m = x2.shape[0]
        h1 = _merge(x2, row2(pre1), od, of, w_bd_b, w_bf_b, w_ga, w_gb, w_out_b, row2(post1),
                    _tile(m, MERGE_ROWS), _tile(d, 1024))
        return _ffn(h1, row2(pre2), w_up_b, conv_w.astype(F32), row2(conv_b), w_d, row2(post2),
                    edges, t_seq, _tile(m, 512), _tile(dff, 1024))

    heads = lambda a, n, t: a.reshape(n, t, n_heads, HEAD_DIM)

    xp = hp.reshape(batch * seq, d)
    zb, kd, vd, kf, vf, logf, fcols = _inproj(xp, row2(pre1), w_in_b, b_f, n_heads,
                                              _tile(seq, INPROJ_ROWS), seq=seq)
    od, of = _prompt_attention(zb, fcols, rel_table, lam_params, subln2, lam_init,
                               batch, seq, n_heads, _tile(seq, ATTN_QUERY_TILE), _tile(seq, ATTN_KEY_TILE))
    hp_out, conv_p = dense_tail(xp, od, of, None, seq)
    state_p = (heads(kd, batch, seq), heads(vd, batch, seq), heads(kf, batch, seq), heads(vf, batch, seq),
               logf.reshape(batch, seq, n_heads), conv_p)

    xs = hs.reshape(nb * t_new, d)
    zb, kd, vd, kf, vf, logf = _inproj(xs, row2(pre1), w_in_b, b_f, n_heads,
                                       _tile(nb * t_new, INPROJ_ROWS))
    tpad = -(-(plen + t_new) // LANES) * LANES
    flog = jnp.concatenate([jnp.swapaxes(plogf.astype(F32), 1, 2),
                            jnp.swapaxes(logf.reshape(nb, t_new, n_heads), 1, 2),
                            jnp.zeros((nb, n_heads, tpad - plen - t_new), F32)], axis=2)
    frow = _cumsum_rows(flog.reshape(nb * n_heads, tpad)).reshape(nb, n_heads, tpad)
    od, of = _sample_attention(zb, caches, layer, frow, rel_table, lam_params, subln2, lam_init,
                               nb, t_new, plen, n_heads)
    hs_out, conv_s = dense_tail(xs, od, of, conv_prev.astype(F32), t_new)
    state_s = (heads(kd, nb, t_new), heads(vd, nb, t_new), heads(kf, nb, t_new), heads(vf, nb, t_new),
               logf.reshape(nb, t_new, n_heads), conv_s)
    return hp_out.reshape(batch, seq, d), hs_out.reshape(nb, t_new, d), state_p, state_s


def kernel(x_prompt, x_sample, cache_diff_k, cache_diff_v, cache_fox_k, cache_fox_v, cache_fox_logf,
           state_ffn_conv, rel_table, pre_norm1, w_in, b_forget, lam_q1, lam_k1, lam_q2, lam_k2,
           diff_subln, w_branch_diff, w_branch_fox, w_out, post_norm1, pre_norm2, w_up, conv_w, conv_b,
           w_down, post_norm2):
    depth = w_in.shape[0]
    hp, hs = x_prompt, x_sample
    caches = (cache_diff_k, cache_diff_v, cache_fox_k, cache_fox_v)
    new_p, new_s = [], []
    for l in range(depth):
        lam_init = 0.8 - 0.6 * math.exp(-0.3 * l)
        lam_params = jnp.stack([lam_q1[l], lam_k1[l], lam_q2[l], lam_k2[l]]).astype(F32)
        params = (pre_norm1[l], w_in[l], b_forget[l], diff_subln[l], w_branch_diff[l], w_branch_fox[l],
                  w_out[l], post_norm1[l], pre_norm2[l], w_up[l], conv_w[l], conv_b[l], w_down[l],
                  post_norm2[l])
        past = (caches, cache_fox_logf[l], state_ffn_conv[l])
        hp, hs, sp, ss = _layer(l, hp, hs, past, rel_table, lam_params, lam_init, params)
        new_p.append(sp)
        new_s.append(ss)
    st = lambda lst, i: jnp.stack([e[i] for e in lst])
    return (hp, hs,
            st(new_p, 0), st(new_p, 1), st(new_p, 2), st(new_p, 3), st(new_p, 4), st(new_p, 5),
            st(new_s, 0), st(new_s, 1), st(new_s, 2), st(new_s, 3), st(new_s, 4), st(new_s, 5))
```

```python
import functools
import math

import jax
import jax.numpy as jnp
from jax import lax
from jax.experimental import pallas as pl
from jax.experimental.pallas import tpu as pltpu

HEAD_DIM = 128
DIFF_QK_DIM = HEAD_DIM // 2
CHUNK = 64
CONV_WIDTH = 3
REL_BUCKETS = 32
REL_MAX_DIST = 128
EPS = 1e-6
NEG_INF = -1e30
LOG2E = math.log2(math.e)

LANES = 128
SUBLANES = 8
ONES_ROWS = 16
ATTN_QUERY_TILE = 512
ATTN_KEY_TILE = 512
FAR_UNROLL = 3
FFN_CHUNKS = 2
MERGE_ROWS = 256
INPROJ_ROWS = 256
VMEM_LIMIT_BYTES = 60 * 1024 * 1024

F32 = jnp.float32
BF16 = jnp.bfloat16


def _params(*sem, flags=None):
    return pltpu.CompilerParams(dimension_semantics=sem, vmem_limit_bytes=VMEM_LIMIT_BYTES, flags=flags)


def _rms_rows(x, g):
    return x * lax.rsqrt(jnp.mean(x * x, axis=-1, keepdims=True) + EPS) * g


def _dot(a, b):
    return jnp.dot(a, b, preferred_element_type=F32)


def _dot_nt(a, b):
    return lax.dot_general(a, b, (((1,), (1,)), ((), ())), preferred_element_type=F32)


def _split3(x):
    hi = x.astype(BF16)
    r1 = x - hi.astype(F32)
    mid = r1.astype(BF16)
    lo = (r1 - mid.astype(F32)).astype(BF16)
    return hi, mid, lo


def _split3_on_axis(x, index, ones_after=False):
    hi, mid, lo = (v.astype(F32) for v in _split3(x))
    rest = jnp.where(index < 6, 1.0, 0.0) if ones_after else 0.0
    return jnp.where(index == 0, hi, jnp.where(index == 1, mid, jnp.where(index == 2, lo, rest))).astype(BF16)


def _transpose_bf16(x):
    return x.astype(F32).T.astype(BF16)


def _inproj_kernel(x_ref, g_ref, w_ref, wf_ref, bf_ref,
                   zb_ref, kd_ref, vd_ref, kf_ref, vf_ref, logf_ref, *rest, n_heads, q_scales, seq):
    dw = kd_ref.shape[1]
    tm = x_ref.shape[0]
    xn = _rms_rows(x_ref[...], g_ref[...]).astype(BF16)
    fl = _dot(xn, wf_ref[...]) + bf_ref[...]
    lf = jnp.minimum(fl, 0.0) - jnp.log1p(jnp.exp(-jnp.abs(fl)))
    logf_ref[...] = lf[:, :n_heads]
    if seq is not None:
        fcol_ref, carry_ref = rest

        @pl.when((pl.program_id(0) * tm) % seq == 0)
        def _():
            carry_ref[...] = jnp.zeros_like(carry_ref)

        hi, mid, lo = _split3(lf)
        tri = _triangle(tm, lower=True)
        local = _dot(tri, hi) + _dot(tri, mid) + _dot(tri, lo)
        f = (local + carry_ref[...]) * LOG2E
        carry_ref[...] = carry_ref[...] + local[tm - 1:tm, :]
        lane = lax.broadcasted_iota(jnp.int32, (tm, LANES), 1)
        for h in range(n_heads):
            fcol_ref[h] = _split3_on_axis(jnp.broadcast_to(f[:, h:h + 1], (tm, LANES)), lane, ones_after=True)
    f32_outs = {1: kd_ref, 2: vd_ref, 4: kf_ref, 5: vf_ref}
    for seg in range(6):
        cols = slice(seg * dw, (seg + 1) * dw)
        z = _dot(xn, w_ref[:, cols])
        if seg in f32_outs:
            f32_outs[seg][...] = z
        else:
            z = z * q_scales[seg]
        zb_ref[:, cols] = z.astype(BF16)


def _inproj(x, g, w_in, b_f, n_heads, tm, seq=None):
    m, d = x.shape
    dw = n_heads * HEAD_DIM
    assert (6 * dw) % LANES == 0
    row = lambda i: (i, 0)
    const2 = lambda i: (0, 0)
    f32_out = jax.ShapeDtypeStruct((m, dw), F32)
    q_scales = {0: DIFF_QK_DIM ** -0.5 * LOG2E, 3: HEAD_DIM ** -0.5 * LOG2E}
    extra_specs, extra_shapes, scratch = [], [], []
    if seq is not None:
        assert seq % tm == 0
        tiles = seq // tm
        extra_specs = [pl.BlockSpec((None, n_heads, tm, LANES), lambda i: (i // tiles, 0, i % tiles, 0))]
        extra_shapes = [jax.ShapeDtypeStruct((m // seq, n_heads, seq, LANES), BF16)]
        scratch = [pltpu.VMEM((1, LANES), F32)]
    return pl.pallas_call(
        functools.partial(_inproj_kernel, n_heads=n_heads, q_scales=q_scales, seq=seq),
        grid=(m // tm,),
        in_specs=[
            pl.BlockSpec((tm, d), row),
            pl.BlockSpec((1, d), const2),
            pl.BlockSpec((d, 6 * dw), const2, pipeline_mode=pl.Buffered(1)),
            pl.BlockSpec((d, LANES), lambda i: (0, 6 * dw // LANES)),
            pl.BlockSpec((1, LANES), const2),
        ],
        out_specs=[
            pl.BlockSpec((tm, 6 * dw), row),
            pl.BlockSpec((tm, dw), row),
            pl.BlockSpec((tm, dw), row),
            pl.BlockSpec((tm, dw), row),
            pl.BlockSpec((tm, dw), row),
            pl.BlockSpec((tm, n_heads), row),
        ] + extra_specs,
        out_shape=[
            jax.ShapeDtypeStruct((m, 6 * dw), BF16),
            f32_out, f32_out, f32_out, f32_out,
            jax.ShapeDtypeStruct((m, n_heads), F32),
        ] + extra_shapes,
        scratch_shapes=scratch,
        compiler_params=_params("arbitrary"),
        name="inproj",
    )(x, g, w_in, w_in, b_f)


def _triangle(n, lower):
    r = lax.broadcasted_iota(jnp.int32, (n, n), 0)
    c = lax.broadcasted_iota(jnp.int32, (n, n), 1)
    return ((r >= c) if lower else (r <= c)).astype(BF16)


def _cumsum_rows_kernel(x_ref, o_ref):
    t = x_ref.shape[1]
    tri = _triangle(LANES, lower=False)
    carry = jnp.zeros((x_ref.shape[0], 1), F32)
    for k in range(t // LANES):
        cols = slice(k * LANES, (k + 1) * LANES)
        hi, mid, lo = _split3(x_ref[:, cols])
        local = _dot(hi, tri) + _dot(mid, tri) + _dot(lo, tri)
        o_ref[:, cols] = local + carry
        carry = carry + local[:, LANES - 1:LANES]


def _cumsum_rows(x):
    return pl.pallas_call(
        _cumsum_rows_kernel,
        out_shape=jax.ShapeDtypeStruct(x.shape, F32),
        compiler_params=_params(),
        name="cumsum_rows",
    )(x)


def _t5_bucket(rel):
    nb = REL_BUCKETS // 2
    max_exact = nb // 2
    n = jnp.abs(rel)
    nf = jnp.maximum(n, 1).astype(jnp.float32)
    large = max_exact + (jnp.log(nf / max_exact) / math.log(REL_MAX_DIST / max_exact)
                         * (nb - max_exact)).astype(jnp.int32)
    large = jnp.minimum(large, nb - 1)
    return jnp.where(rel > 0, nb, 0) + jnp.where(n < max_exact, n, large)


def _bias_kernel(table_ref, idx_ref, mask_ref, o_ref, *, minus_bucket):
    h = pl.program_id(1)
    idx = idx_ref[...]
    acc = mask_ref[...]
    if minus_bucket is not None:
        acc = acc - table_ref[minus_bucket, h]
    vals = [table_ref[b, h] for b in range(REL_BUCKETS)]
    bit = 1
    while len(vals) > 1:
        odd = (idx & bit) != 0
        vals = [jnp.where(odd, vals[2 * i + 1], vals[2 * i]) for i in range(len(vals) // 2)]
        bit *= 2
    o_ref[...] = (acc + vals[0]) * LOG2E


def _bias_tiles(rel_table, qpos, kpos, keys_major=False, minus_bucket=None):
    n_heads = rel_table.shape[1]
    if keys_major:
        qp, kp = qpos[:, None, :], kpos[:, :, None]
    else:
        qp, kp = qpos[:, :, None], kpos[:, None, :]
    idx = _t5_bucket(kp - qp).astype(jnp.int32)
    mask = jnp.where((kp // CHUNK) <= (qp // CHUNK), 0.0, NEG_INF).astype(F32)
    nt, r, c = idx.shape
    return pl.pallas_call(
        functools.partial(_bias_kernel, minus_bucket=minus_bucket),
        grid=(nt, n_heads),
        in_specs=[
            pl.BlockSpec(memory_space=pltpu.SMEM),
            pl.BlockSpec((None, r, c), lambda t, h: (t, 0, 0)),
            pl.BlockSpec((None, r, c), lambda t, h: (t, 0, 0)),
        ],
        out_specs=pl.BlockSpec((None, None, r, c), lambda t, h: (h, t, 0, 0)),
        out_shape=jax.ShapeDtypeStruct((n_heads, nt, r, c), F32),
        compiler_params=_params("arbitrary", "arbitrary"),
        name="bias_tiles",
    )(rel_table.astype(F32), idx, mask)


def _far_bucket(min_dist):
    nb = REL_BUCKETS // 2
    max_exact = nb // 2
    large = max_exact + math.log(min_dist / max_exact) / math.log(REL_MAX_DIST / max_exact) * (nb - max_exact)
    assert large >= nb - 1 + 0.5, "key tile too short for a constant far-field bias"
    return nb - 1


def _lambda_value(lam_ref, lam_init):
    a = lam_ref[...]
    s1 = jnp.sum(a[0:1] * a[1:2], axis=-1, keepdims=True)
    s2 = jnp.sum(a[2:3] * a[3:4], axis=-1, keepdims=True)
    return jnp.exp(s1) - jnp.exp(s2) + lam_init


def _prompt_attn_kernel(table_ref, lam_ref, subln_ref, qd_ref, kd_ref, vd_ref, qf_ref, kf_ref, vf_ref,
                        fcol_ref, bias_ref, cmask_ref, qdn_ref, qfn_ref, od_ref, of_ref,
                        kda_ref, kfa_ref, vdt_ref, vft_ref, qda_ref, qfa_ref,
                        md_ref, accd_ref, mf_ref, accf_ref,
                        sd_ref, pd_ref, ad_ref, sf_ref, pf_ref, af_ref, cd_ref, cf_ref,
                        *, tq, tk, lam_init, far_bucket):
    h = pl.program_id(1)
    qi = pl.program_id(2)
    seq = kd_ref.shape[0]
    ratio = tq // tk

    def prepare_keys_values():
        lane = lax.broadcasted_iota(jnp.int32, (seq, LANES), 1)
        kda_ref[:, :HEAD_DIM] = kd_ref[...]
        kda_ref[:, HEAD_DIM:] = jnp.where(lane < 3, 1.0, 0.0).astype(BF16)
        kfa_ref[:, :HEAD_DIM] = kf_ref[...]
        kfa_ref[:, HEAD_DIM:] = fcol_ref[...]
        row2 = lax.broadcasted_iota(jnp.int32, (HEAD_DIM, 2 * tq), 0)
        qda_ref[HEAD_DIM:, :] = _split3_on_axis(
            jnp.full((HEAD_DIM, 2 * tq), table_ref[far_bucket, h] * LOG2E, F32), row2)

        ones = jnp.ones((ONES_ROWS, tk), BF16)

        def transpose_values(c, carry):
            r = pl.ds(pl.multiple_of(c * tk, tk), tk)
            vdt_ref[c] = jnp.concatenate([_transpose_bf16(vd_ref[r, :]), ones], axis=0)
            vft_ref[c] = jnp.concatenate([_transpose_bf16(vf_ref[r, :]), ones], axis=0)
            return carry

        lax.fori_loop(0, seq // tk, transpose_values, 0)

    def key_rows(ref, j):
        return ref[pl.ds(pl.multiple_of(j * tk, tk), tk), :]

    def buffer_scores(j, k_ref, q_aug, s_ref, c_ref, add=None):
        s = _dot(key_rows(k_ref, j), q_aug)
        if add is not None:
            s = s + add
        s_ref[...] = s
        c_ref[...] = jnp.max(s, axis=0, keepdims=True)

    def prepare_queries(qd_src_ref, qf_src_ref, tile):
        row = lax.broadcasted_iota(jnp.int32, (HEAD_DIM, tq), 0)
        qdt = qd_src_ref[...].astype(F32).T
        zero = jnp.zeros_like(qdt)
        top = jnp.concatenate([jnp.where(row < DIFF_QK_DIM, qdt, zero),
                               jnp.where(row >= DIFF_QK_DIM, qdt, zero)], axis=1).astype(BF16)
        fq = fcol_ref[pl.ds(pl.multiple_of(tile * tq, tq), tq), :].astype(F32).T
        extra = jnp.where(row < 3, -1.0, jnp.where(row < 6, pltpu.roll(fq, 3, 0), 0.0))
        qf_aug = jnp.concatenate([qf_src_ref[...].astype(F32).T.astype(BF16),
                                  extra.astype(BF16)], axis=0)
        qda_ref[:HEAD_DIM, :] = top
        qfa_ref[...] = qf_aug
        buffer_scores(0, kda_ref, qda_ref[...], sd_ref, cd_ref)
        buffer_scores(0, kfa_ref, qf_aug, sf_ref, cf_ref)

    @pl.when(qi == 0)
    def _():
        prepare_keys_values()
        prepare_queries(qd_ref, qf_ref, 0)

    branches = ((kda_ref, qda_ref, vdt_ref, sd_ref, pd_ref, ad_ref, md_ref, accd_ref, cd_ref),
                (kfa_ref, qfa_ref, vft_ref, sf_ref, pf_ref, af_ref, mf_ref, accf_ref, cf_ref))

    def pipeline_step(j_prev, adds, j_next, next_adds=(None, None)):
        for (k_ref, q_ref, vt_ref, s_ref, p_ref, a_ref, m_ref, acc_ref, c_ref), add, next_add in zip(
                branches, adds, next_adds):
            acc_ref[...] = a_ref[...] * acc_ref[...] + _dot(vt_ref[j_prev], p_ref[...])
            s = s_ref[...]
            if add is None:
                s_max = c_ref[...]
            else:
                s = s + add
                s_max = jnp.max(s, axis=0, keepdims=True)
            m_prev = m_ref[...]
            m_new = jnp.maximum(m_prev, s_max)
            p_ref[...] = jnp.exp2(s - m_new).astype(BF16)
            a_ref[...] = jnp.exp2(m_prev - m_new)
            m_ref[...] = m_new
            if j_next is not None:
                buffer_scores(j_next, k_ref, q_ref[...], s_ref, c_ref, next_add)

    for k_ref, q_ref, vt_ref, s_ref, p_ref, a_ref, m_ref, acc_ref, c_ref in branches:
        m_ref[...] = jnp.full_like(m_ref, NEG_INF)
        acc_ref[...] = jnp.zeros_like(acc_ref)
        p_ref[...] = jnp.zeros_like(p_ref)
        a_ref[...] = jnp.ones_like(a_ref)

    def far_tiles(unroll):
        def body(i, carry):
            for u in range(unroll):
                j = i * unroll + u
                pipeline_step(jnp.maximum(j - 1, 0), (None, None), j + 1)
            return carry
        return body

    first_diag = qi * ratio
    n_far = jnp.maximum(first_diag - 1, 0)
    n_main = n_far // FAR_UNROLL
    lax.fori_loop(0, n_main, far_tiles(FAR_UNROLL), 0)
    lax.fori_loop(n_main * FAR_UNROLL, n_far, far_tiles(1), 0)

    near = jnp.maximum(first_diag - 1, 0)
    gone = jnp.where(qi == 0, NEG_INF, 0.0)

    def diag_adds(u):
        bias = bias_ref[u + 1]
        return jnp.concatenate([bias, bias], axis=1), cmask_ref[u]

    bias = bias_ref[0] + gone
    pipeline_step(jnp.maximum(near - 1, 0), (jnp.concatenate([bias, bias], axis=1), gone),
                  first_diag, diag_adds(0))
    for u in range(ratio):
        more = u + 1 < ratio
        pipeline_step(near if u == 0 else first_diag + u - 1, (None, None),
                      first_diag + u + 1 if more else None, diag_adds(u + 1) if more else (None, None))
    last = first_diag + ratio - 1
    for k_ref, q_ref, vt_ref, s_ref, p_ref, a_ref, m_ref, acc_ref, c_ref in branches:
        acc_ref[...] = a_ref[...] * acc_ref[...] + _dot(vt_ref[last], p_ref[...])

    prepare_queries(qdn_ref, qfn_ref, jnp.minimum(qi + 1, pl.num_programs(2) - 1))

    lam = _lambda_value(lam_ref, lam_init)
    acc = accd_ref[...]
    num, inv = acc[:HEAD_DIM], 1.0 / acc[HEAD_DIM:HEAD_DIM + 1]
    odt = num[:, :tq] * inv[:, :tq] - lam * (num[:, tq:] * inv[:, tq:])
    od_ref[...] = (_rms_rows(odt.T, subln_ref[...]) * (1.0 - lam_init)).astype(BF16)
    acc = accf_ref[...]
    of_ref[...] = (acc[:HEAD_DIM] * (1.0 / acc[HEAD_DIM:HEAD_DIM + 1])).T.astype(BF16)


def _prompt_attention(zb, fcols, rel_table, lam_params, subln, lam_init, batch, seq, n_heads, tq, tk):
    m = batch * seq
    dw = n_heads * HEAD_DIM
    nq = seq // tq
    ratio = tq // tk
    assert tq == ratio * tk
    far_bucket = _far_bucket(tk + 1)
    vrows = HEAD_DIM + ONES_ROWS
    qpos = tk + jnp.arange(tq, dtype=jnp.int32)
    kpos = jnp.arange((ratio + 1) * tk, dtype=jnp.int32).reshape(ratio + 1, tk)
    bias = _bias_tiles(rel_table, jnp.broadcast_to(qpos, (ratio + 1, tq)), kpos,
                       keys_major=True, minus_bucket=far_bucket)
    cmask = jnp.where(kpos[1:, :, None] <= qpos[None, None, :], 0.0, NEG_INF).astype(F32)

    def q_spec(seg, ahead=0):
        return pl.BlockSpec((tq, HEAD_DIM), lambda b, h, q, seg=seg: (b * nq + jnp.minimum(q + ahead, nq - 1),
                                                                      seg * n_heads + h))

    def kv_spec(seg):
        return pl.BlockSpec((seq, HEAD_DIM), lambda b, h, q, seg=seg: (b, seg * n_heads + h))

    out_spec = pl.BlockSpec((tq, HEAD_DIM), lambda b, h, q: (b * nq + q, h))
    const2 = lambda b, h, q: (0, 0)
    kern = functools.partial(_prompt_attn_kernel, tq=tq, tk=tk, lam_init=lam_init, far_bucket=far_bucket)
    return pl.pallas_call(
        kern,
        grid=(batch, n_heads, nq),
        in_specs=[
            pl.BlockSpec(memory_space=pltpu.SMEM),
            pl.BlockSpec((4, DIFF_QK_DIM), const2),
            pl.BlockSpec((1, HEAD_DIM), const2),
            q_spec(0), kv_spec(1), kv_spec(2), q_spec(3), kv_spec(4), kv_spec(5),
            pl.BlockSpec((None, None, seq, LANES), lambda b, h, q: (b, h, 0, 0)),
            pl.BlockSpec((None, ratio + 1, tk, tq), lambda b, h, q: (h, 0, 0, 0)),
            pl.BlockSpec((ratio, tk, tq), lambda b, h, q: (0, 0, 0)),
            q_spec(0, ahead=1), q_spec(3, ahead=1),
        ],
        out_specs=[out_spec, out_spec],
        out_shape=[jax.ShapeDtypeStruct((m, dw), BF16), jax.ShapeDtypeStruct((m, dw), BF16)],
        scratch_shapes=[
            pltpu.VMEM((seq, 2 * HEAD_DIM), BF16), pltpu.VMEM((seq, 2 * HEAD_DIM), BF16),
            pltpu.VMEM((seq // tk, vrows, tk), BF16), pltpu.VMEM((seq // tk, vrows, tk), BF16),
            pltpu.VMEM((2 * HEAD_DIM, 2 * tq), BF16), pltpu.VMEM((2 * HEAD_DIM, tq), BF16),
            pltpu.VMEM((1, 2 * tq), F32), pltpu.VMEM((vrows, 2 * tq), F32),
            pltpu.VMEM((1, tq), F32), pltpu.VMEM((vrows, tq), F32),
            pltpu.VMEM((tk, 2 * tq), F32), pltpu.VMEM((tk, 2 * tq), BF16), pltpu.VMEM((1, 2 * tq), F32),
            pltpu.VMEM((tk, tq), F32), pltpu.VMEM((tk, tq), BF16), pltpu.VMEM((1, tq), F32),
            pltpu.VMEM((1, 2 * tq), F32), pltpu.VMEM((1, tq), F32),
        ],
        compiler_params=_params("arbitrary", "arbitrary", "arbitrary"),
        name="prompt_attention",
    )(rel_table.astype(F32), lam_params, subln, zb, zb, zb, zb, zb, zb, fcols, bias, cmask, zb, zb)


def _split_diff_queries(q):
    lane = lax.broadcasted_iota(jnp.int32, q.shape, 1)
    zero = jnp.zeros_like(q)
    return jnp.concatenate([jnp.where(lane < DIFF_QK_DIM, q, zero),
                            jnp.where(lane >= DIFF_QK_DIM, q, zero)], axis=0)


def _sample_attn_kernel(lam_ref, subln_ref, z_ref, kdc_ref, vdc_ref, kfc_ref, vfc_ref, frow_ref, bias_ref,
                        cmask_ref, od_ref, of_ref, *, past, t_new, n_heads, lam_init):
    dw = n_heads * HEAD_DIM

    def cached(ref, h):
        return ref[pl.ds(h, past, stride=n_heads), :].astype(BF16)

    def new_rows(seg, h):
        lo = seg * dw + h * HEAD_DIM
        return z_ref[:, lo:lo + HEAD_DIM]

    def attend(q, kc, vc, kn, vn, add_c, add_n):
        sc = _dot_nt(q, kc) + add_c
        sn = _dot_nt(q, kn) + add_n
        mx = jnp.maximum(jnp.max(sc, axis=-1, keepdims=True), jnp.max(sn, axis=-1, keepdims=True))
        pc = jnp.exp2(sc - mx)
        pn = jnp.exp2(sn - mx)
        l = jnp.sum(pc, axis=-1, keepdims=True) + jnp.sum(pn, axis=-1, keepdims=True)
        acc = _dot(pc.astype(BF16), vc) + _dot(pn.astype(BF16), vn)
        return acc, l

    lam = _lambda_value(lam_ref, lam_init)
    cmask = cmask_ref[...]
    eye = (lax.broadcasted_iota(jnp.int32, (t_new, t_new), 0)
           == lax.broadcasted_iota(jnp.int32, (t_new, t_new), 1))
    for h in range(n_heads):
        cols = slice(h * HEAD_DIM, (h + 1) * HEAD_DIM)
        qs = _split_diff_queries(new_rows(0, h))
        bias_c = bias_ref[h, :, :past]
        bias_n = bias_ref[h, :, past:past + t_new]
        acc, l = attend(qs, cached(kdc_ref, h), cached(vdc_ref, h),
                        new_rows(1, h), new_rows(2, h),
                        jnp.concatenate([bias_c, bias_c], axis=0), jnp.concatenate([bias_n, bias_n], axis=0))
        od = acc[:t_new] / l[:t_new] - lam * (acc[t_new:] / l[t_new:])
        od_ref[:, cols] = (_rms_rows(od, subln_ref[...]) * (1.0 - lam_init)).astype(BF16)

        f_new = frow_ref[h:h + 1, past:past + t_new]
        fq = jnp.sum(jnp.where(eye, jnp.broadcast_to(f_new, (t_new, t_new)), 0.0), axis=1, keepdims=True)
        dec_c = (fq - frow_ref[h:h + 1, :past]) * LOG2E
        dec_n = (fq - f_new) * LOG2E
        acc, l = attend(new_rows(3, h), cached(kfc_ref, h), cached(vfc_ref, h),
                        new_rows(4, h), new_rows(5, h), dec_c, dec_n + cmask)
        of_ref[:, cols] = (acc / l).astype(BF16)


def _sample_attention(zb, caches, layer, frow, rel_table, lam_params, subln, lam_init, nb, t_new, past, n_heads):
    dw = n_heads * HEAD_DIM
    tpad = frow.shape[-1]
    ar = jnp.arange(t_new, dtype=jnp.int32)
    bias = _bias_tiles(rel_table, (past + ar)[None], jnp.arange(tpad, dtype=jnp.int32)[None])[:, 0]
    cmask = jnp.where(ar[None, :] <= ar[:, None], 0.0, NEG_INF).astype(F32)
    caches = [c.reshape(c.shape[0], nb, past * n_heads, HEAD_DIM) for c in caches]
    cache_spec = pl.BlockSpec((None, None, past * n_heads, HEAD_DIM), lambda b: (layer, b, 0, 0))
    out_spec = pl.BlockSpec((t_new, dw), lambda b: (b, 0))
    const2 = lambda b: (0, 0)
    kern = functools.partial(_sample_attn_kernel, past=past, t_new=t_new, n_heads=n_heads, lam_init=lam_init)
    return pl.pallas_call(
        kern,
        grid=(nb,),
        in_specs=[
            pl.BlockSpec((4, DIFF_QK_DIM), const2),
            pl.BlockSpec((1, HEAD_DIM), const2),
            pl.BlockSpec((t_new, 6 * dw), lambda b: (b, 0)),
            cache_spec, cache_spec, cache_spec, cache_spec,
            pl.BlockSpec((None, n_heads, tpad), lambda b: (b, 0, 0)),
            pl.BlockSpec((n_heads, t_new, tpad), lambda b: (0, 0, 0)),
            pl.BlockSpec((t_new, t_new), const2),
        ],
        out_specs=[out_spec, out_spec],
        out_shape=[jax.ShapeDtypeStruct((nb * t_new, dw), BF16), jax.ShapeDtypeStruct((nb * t_new, dw), BF16)],
        compiler_params=_params("arbitrary"),
        name="sample_attention",
    )(lam_params, subln, zb, *caches, frow, bias, cmask)


def _merge_kernel(x_ref, g1_ref, od_ref, of_ref, wbd_ref, wbf_ref, wga_ref, wgb_ref, wout_ref, gp_ref,
                  o_ref, *, chunk):
    x = x_ref[...]
    xn = _rms_rows(x, g1_ref[...]).astype(BF16)
    od = od_ref[...]
    of = of_ref[...]
    o = None
    for c in range(x.shape[1] // chunk):
        cols = slice(c * chunk, (c + 1) * chunk)
        ga = jax.nn.sigmoid(_dot(xn, wga_ref[:, cols]))
        gb = jax.nn.sigmoid(_dot(xn, wgb_ref[:, cols]))
        u = (ga * _dot(od, wbd_ref[:, cols]) + gb * _dot(of, wbf_ref[:, cols])).astype(BF16)
        part = _dot(u, wout_ref[cols, :])
        o = part if o is None else o + part
    o_ref[...] = x + _rms_rows(o, gp_ref[...])


def _merge(x, g1, od, of, w_bd, w_bf, w_ga, w_gb, w_out, g_post, tm, chunk):
    m, d = x.shape
    dw = od.shape[1]
    row = lambda i: (i, 0)
    const2 = lambda i: (0, 0)
    resident = lambda shape: pl.BlockSpec(shape, const2, pipeline_mode=pl.Buffered(1))
    return pl.pallas_call(
        functools.partial(_merge_kernel, chunk=chunk),
        grid=(m // tm,),
        in_specs=[
            pl.BlockSpec((tm, d), row),
            pl.BlockSpec((1, d), const2),
            pl.BlockSpec((tm, dw), row),
            pl.BlockSpec((tm, dw), row),
            resident((dw, d)), resident((dw, d)), resident((d, d)), resident((d, d)), resident((d, d)),
            pl.BlockSpec((1, d), const2),
        ],
        out_specs=pl.BlockSpec((tm, d), row),
        out_shape=jax.ShapeDtypeStruct((m, d), F32),
        compiler_params=_params("arbitrary"),
        name="merge",
    )(x, g1, od, of, w_bd, w_bf, w_ga, w_gb, w_out, g_post)


def _gelu_tanh(x):
    k = -2.0 * math.sqrt(2.0 / math.pi) * LOG2E
    return x / (1.0 + jnp.exp2(x * (k + (k * 0.044715) * (x * x))))


def _ffn_kernel(*refs, seq, has_edges):
    if has_edges:
        (h_ref, g2_ref, wa_ref, wb_ref, cw_ref, cb_ref, wd_ref, gp_ref, prev_ref,
         o_ref, cs_ref, xn_ref, tail_ref) = refs
    else:
        (h_ref, g2_ref, wa_ref, wb_ref, cw_ref, cb_ref, wd_ref, gp_ref,
         o_ref, cs_ref, xn_ref, tail_ref) = refs
    i = pl.program_id(0)
    f = pl.program_id(1)
    tm = h_ref.shape[0]

    tf = wa_ref.shape[1]

    @pl.when(f == 0)
    def _():
        xn_ref[...] = _rms_rows(h_ref[...], g2_ref[...]).astype(BF16)
        o_ref[...] = jnp.zeros_like(o_ref)

    if not has_edges:
        @pl.when((i * tm) % seq == 0)
        def _():
            tail_ref[f] = jnp.zeros(tail_ref.shape[1:], F32)

    xn = xn_ref[...]

    def gated_chunk(cols):
        width = cols.stop - cols.start
        a = _dot(xn, wa_ref[:, cols])
        gate = _dot(xn, wb_ref[:, cols])
        back1 = pltpu.roll(a, 1, 0)
        back2 = pltpu.roll(a, 2, 0)
        if has_edges:
            shape3 = (tm // seq, seq, width)
            t = lax.broadcasted_iota(jnp.int32, shape3, 1)
            prev = prev_ref[:, :, cols]
            p0, p1 = prev[:, 0:1, :], prev[:, 1:2, :]
            am1 = jnp.where(t == 0, p1, back1.reshape(shape3)).reshape(tm, width)
            am2 = jnp.where(t == 0, p0, jnp.where(t == 1, p1, back2.reshape(shape3))).reshape(tm, width)
            cs_ref[:, :, cols] = a.reshape(shape3)[:, seq - (CONV_WIDTH - 1):, :]
        else:
            prev = tail_ref[f, :, cols]
            p0 = prev[SUBLANES - 2:SUBLANES - 1]
            p1 = prev[SUBLANES - 1:SUBLANES]
            top = lax.broadcasted_iota(jnp.int32, (SUBLANES, width), 0)
            am1 = jnp.concatenate([jnp.where(top == 0, p1, back1[:SUBLANES]), back1[SUBLANES:]], axis=0)
            am2 = jnp.concatenate([jnp.where(top == 0, p0, jnp.where(top == 1, p1, back2[:SUBLANES])),
                                   back2[SUBLANES:]], axis=0)
            tail_ref[f, :, cols] = a[tm - SUBLANES:, :]
            cs_ref[:, :, cols] = a[tm - (CONV_WIDTH - 1):, :][None]
        cw = cw_ref[:, cols]
        ac = cw[0:1] * am2 + cw[1:2] * am1 + cw[2:3] * a + cb_ref[:, cols]
        return (_gelu_tanh(ac) * gate).astype(BF16)

    width = tf // FFN_CHUNKS
    contrib = None
    for c in range(FFN_CHUNKS):
        cols = slice(c * width, (c + 1) * width)
        part = _dot(gated_chunk(cols), wd_ref[cols, :])
        contrib = part if contrib is None else contrib + part
    o_ref[...] += contrib

    @pl.when(f == pl.num_programs(1) - 1)
    def _():
        o_ref[...] = h_ref[...] + _rms_rows(o_ref[...], gp_ref[...])


def _ffn(h, g2, w_up, conv_w, conv_b, w_d, g_post, edges, seq, tm, tf):
    m, d = h.shape
    dff = w_d.shape[0]
    nf = dff // tf
    has_edges = edges is not None
    row = lambda i, f: (i, 0)
    col = lambda i, f: (0, f)
    const2 = lambda i, f: (0, 0)
    in_specs = [
        pl.BlockSpec((tm, d), row),
        pl.BlockSpec((1, d), const2),
        pl.BlockSpec((d, tf), col),
        pl.BlockSpec((d, tf), lambda i, f: (0, nf + f)),
        pl.BlockSpec((CONV_WIDTH, tf), col),
        pl.BlockSpec((1, tf), col),
        pl.BlockSpec((tf, d), lambda i, f: (f, 0)),
        pl.BlockSpec((1, d), const2),
    ]
    args = [h, g2, w_up, w_up, conv_w, conv_b, w_d, g_post]
    if has_edges:
        assert tm % seq == 0
        tails_per_tile = tm // seq
        in_specs.append(pl.BlockSpec((tails_per_tile, CONV_WIDTH - 1, tf), lambda i, f: (i, 0, f)))
        args.append(edges)
    else:
        assert seq % tm == 0
        tails_per_tile = 1
    n_tails = (m // tm) * tails_per_tile
    out, tails = pl.pallas_call(
        functools.partial(_ffn_kernel, seq=seq, has_edges=has_edges),
        grid=(m // tm, nf),
        in_specs=in_specs,
        out_specs=[pl.BlockSpec((tm, d), row),
                   pl.BlockSpec((tails_per_tile, CONV_WIDTH - 1, tf), lambda i, f: (i, 0, f))],
        out_shape=[jax.ShapeDtypeStruct((m, d), F32),
                   jax.ShapeDtypeStruct((n_tails, CONV_WIDTH - 1, dff), F32)],
        scratch_shapes=[pltpu.VMEM((tm, d), BF16), pltpu.VMEM((nf, SUBLANES, tf), F32)],
        compiler_params=_params("arbitrary", "arbitrary"),
        name="conv_ffn",
    )(*args)
    if not has_edges:
        tiles_per_seq = seq // tm
        tails = tails[tiles_per_seq - 1::tiles_per_seq]
    return out, tails


def _tile(n, cap):
    t = min(n, cap)
    assert n % t == 0
    return t


def _layer(layer, hp, hs, past, rel_table, lam_params, lam_init, p):
    (pre1, w_in, b_forget, subln, w_bd, w_bf, w_out, post1, pre2, w_up, conv_w, conv_b, w_down, post2) = p
    batch, seq, d = hp.shape
    nb, t_new, _ = hs.shape
    n_heads = d // (2 * HEAD_DIM)
    dw = n_heads * HEAD_DIM
    dff = w_down.shape[0]
    caches, plogf, conv_prev = past
    plen = plogf.shape[1]

    row2 = lambda v: v.reshape(1, -1).astype(F32)
    w_in_b = w_in.astype(BF16)
    b_f = jnp.pad(b_forget.astype(F32), (0, LANES - n_heads)).reshape(1, LANES)
    w_ga = w_in_b[:, 6 * dw + n_heads:6 * dw + n_heads + d]
    w_gb = w_in_b[:, 6 * dw + n_heads + d:]
    w_bd_b, w_bf_b, w_out_b = w_bd.astype(BF16), w_bf.astype(BF16), w_out.astype(BF16)
    w_up_b = w_up.astype(BF16)
    w_d = w_down.astype(BF16)
    subln2 = row2(subln)

    def dense_tail(x2, od, of, edges, t_seq):
        m = x2.shape[0]
        h1 = _merge(x2, row2(pre1), od, of, w_bd_b, w_bf_b, w_ga, w_gb, w_out_b, row2(post1),
                    _tile(m, MERGE_ROWS), _tile(d, 1024))
        return _ffn(h1, row2(pre2), w_up_b, conv_w.astype(F32), row2(conv_b), w_d, row2(post2),
                    edges, t_seq, _tile(m, 512), _tile(dff, 1024))

    heads = lambda a, n, t: a.reshape(n, t, n_heads, HEAD_DIM)

    xp = hp.reshape(batch * seq, d)
    zb, kd, vd, kf, vf, logf, fcols = _inproj(xp, row2(pre1), w_in_b, b_f, n_heads,
                                              _tile(seq, INPROJ_ROWS), seq=seq)
    od, of = _prompt_attention(zb, fcols, rel_table, lam_params, subln2, lam_init,
                               batch, seq, n_heads, _tile(seq, ATTN_QUERY_TILE), _tile(seq, ATTN_KEY_TILE))
    hp_out, conv_p = dense_tail(xp, od, of, None, seq)
    state_p = (heads(kd, batch, seq), heads(vd, batch, seq), heads(kf, batch, seq), heads(vf, batch, seq),
               logf.reshape(batch, seq, n_heads), conv_p)

    xs = hs.reshape(nb * t_new, d)
    zb, kd, vd, kf, vf, logf = _inproj(xs, row2(pre1), w_in_b, b_f, n_heads,
                                       _tile(nb * t_new, INPROJ_ROWS))
    tpad = -(-(plen + t_new) // LANES) * LANES
    flog = jnp.concatenate([jnp.swapaxes(plogf.astype(F32), 1, 2),
                            jnp.swapaxes(logf.reshape(nb, t_new, n_heads), 1, 2),
                            jnp.zeros((nb, n_heads, tpad - plen - t_new), F32)], axis=2)
    frow = _cumsum_rows(flog.reshape(nb * n_heads, tpad)).reshape(nb, n_heads, tpad)
    od, of = _sample_attention(zb, caches, layer, frow, rel_table, lam_params, subln2, lam_init,
                               nb, t_new, plen, n_heads)
    hs_out, conv_s = dense_tail(xs, od, of, conv_prev.astype(F32), t_new)
    state_s = (heads(kd, nb, t_new), heads(vd, nb, t_new), heads(kf, nb, t_new), heads(vf, nb, t_new),
               logf.reshape(nb, t_new, n_heads), conv_s)
    return hp_out.reshape(batch, seq, d), hs_out.reshape(nb, t_new, d), state_p, state_s


def kernel(x_prompt, x_sample, cache_diff_k, cache_diff_v, cache_fox_k, cache_fox_v, cache_fox_logf,
           state_ffn_conv, rel_table, pre_norm1, w_in, b_forget, lam_q1, lam_k1, lam_q2, lam_k2,
           diff_subln, w_branch_diff, w_branch_fox, w_out, post_norm1, pre_norm2, w_up, conv_w, conv_b,
           w_down, post_norm2):
    depth = w_in.shape[0]
    hp, hs = x_prompt, x_sample
    caches = (cache_diff_k, cache_diff_v, cache_fox_k, cache_fox_v)
    new_p, new_s = [], []
    for l in range(depth):
        lam_init = 0.8 - 0.6 * math.exp(-0.3 * l)
        lam_params = jnp.stack([lam_q1[l], lam_k1[l], lam_q2[l], lam_k2[l]]).astype(F32)
        params = (pre_norm1[l], w_in[l], b_forget[l], diff_subln[l], w_branch_diff[l], w_branch_fox[l],
                  w_out[l], post_norm1[l], pre_norm2[l], w_up[l], conv_w[l], conv_b[l], w_down[l],
                  post_norm2[l])
        past = (caches, cache_fox_logf[l], state_ffn_conv[l])
        hp, hs, sp, ss = _layer(l, hp, hs, past, rel_table, lam_params, lam_init, params)
        new_p.append(sp)
        new_s.append(ss)
    st = lambda lst, i: jnp.stack([e[i] for e in lst])
    return (hp, hs,
            st(new_p, 0), st(new_p, 1), st(new_p, 2), st(new_p, 3), st(new_p, 4), st(new_p, 5),
            st(new_s, 0), st(new_s, 1), st(new_s, 2), st(new_s, 3), st(new_s, 4), st(new_s, 5))
```

```python
import functools
import math

import jax
import jax.numpy as jnp
from jax import lax
from jax.experimental import pallas as pl
from jax.experimental.pallas import tpu as pltpu

HEAD_DIM = 128
DIFF_QK_DIM = HEAD_DIM // 2
CHUNK = 64
CONV_WIDTH = 3
REL_BUCKETS = 32
REL_MAX_DIST = 128
EPS = 1e-6
NEG_INF = -1e30
LOG2E = math.log2(math.e)

LANES = 128
SUBLANES = 8
ONES_ROWS = 16
ATTN_QUERY_TILE = 512
ATTN_KEY_TILE = 512
FAR_UNROLL = 3
FFN_CHUNKS = 2
MERGE_ROWS = 512
INPROJ_ROWS = 256
VMEM_LIMIT_BYTES = 60 * 1024 * 1024

F32 = jnp.float32
BF16 = jnp.bfloat16


def _params(*sem, flags=None):
    return pltpu.CompilerParams(dimension_semantics=sem, vmem_limit_bytes=VMEM_LIMIT_BYTES, flags=flags)


def _rms_rows(x, g):
    return x * lax.rsqrt(jnp.mean(x * x, axis=-1, keepdims=True) + EPS) * g


def _dot(a, b):
    return jnp.dot(a, b, preferred_element_type=F32)


def _dot_nt(a, b):
    return lax.dot_general(a, b, (((1,), (1,)), ((), ())), preferred_element_type=F32)


def _split3(x):
    hi = x.astype(BF16)
    r1 = x - hi.astype(F32)
    mid = r1.astype(BF16)
    lo = (r1 - mid.astype(F32)).astype(BF16)
    return hi, mid, lo


def _split3_on_axis(x, index, ones_after=False):
    hi, mid, lo = (v.astype(F32) for v in _split3(x))
    rest = jnp.where(index < 6, 1.0, 0.0) if ones_after else 0.0
    return jnp.where(index == 0, hi, jnp.where(index == 1, mid, jnp.where(index == 2, lo, rest))).astype(BF16)


def _transpose_bf16(x):
    return x.astype(F32).T.astype(BF16)


def _inproj_kernel(x_ref, g_ref, w_ref, wf_ref, bf_ref,
                   zb_ref, kd_ref, vd_ref, kf_ref, vf_ref, logf_ref, *rest, n_heads, q_scales, seq):
    dw = kd_ref.shape[1]
    tm = x_ref.shape[0]
    xn = _rms_rows(x_ref[...], g_ref[...]).astype(BF16)
    fl = _dot(xn, wf_ref[...]) + bf_ref[...]
    lf = jnp.minimum(fl, 0.0) - jnp.log1p(jnp.exp(-jnp.abs(fl)))
    logf_ref[...] = lf[:, :n_heads]
    if seq is not None:
        fcol_ref, carry_ref = rest

        @pl.when((pl.program_id(0) * tm) % seq == 0)
        def _():
            carry_ref[...] = jnp.zeros_like(carry_ref)

        hi, mid, lo = _split3(lf)
        tri = _triangle(tm, lower=True)
        local = _dot(tri, hi) + _dot(tri, mid) + _dot(tri, lo)
        f = (local + carry_ref[...]) * LOG2E
        carry_ref[...] = carry_ref[...] + local[tm - 1:tm, :]
        lane = lax.broadcasted_iota(jnp.int32, (tm, LANES), 1)
        for h in range(n_heads):
            fcol_ref[h] = _split3_on_axis(jnp.broadcast_to(f[:, h:h + 1], (tm, LANES)), lane, ones_after=True)
    f32_outs = {1: kd_ref, 2: vd_ref, 4: kf_ref, 5: vf_ref}
    for seg in range(6):
        cols = slice(seg * dw, (seg + 1) * dw)
        z = _dot(xn, w_ref[:, cols])
        if seg in f32_outs:
            f32_outs[seg][...] = z
        else:
            z = z * q_scales[seg]
        zb_ref[:, cols] = z.astype(BF16)


def _inproj(x, g, w_in, b_f, n_heads, tm, seq=None):
    m, d = x.shape
    dw = n_heads * HEAD_DIM
    assert (6 * dw) % LANES == 0
    row = lambda i: (i, 0)
    const2 = lambda i: (0, 0)
    f32_out = jax.ShapeDtypeStruct((m, dw), F32)
    q_scales = {0: DIFF_QK_DIM ** -0.5 * LOG2E, 3: HEAD_DIM ** -0.5 * LOG2E}
    extra_specs, extra_shapes, scratch = [], [], []
    if seq is not None:
        assert seq % tm == 0
        tiles = seq // tm
        extra_specs = [pl.BlockSpec((None, n_heads, tm, LANES), lambda i: (i // tiles, 0, i % tiles, 0))]
        extra_shapes = [jax.ShapeDtypeStruct((m // seq, n_heads, seq, LANES), BF16)]
        scratch = [pltpu.VMEM((1, LANES), F32)]
    return pl.pallas_call(
        functools.partial(_inproj_kernel, n_heads=n_heads, q_scales=q_scales, seq=seq),
        grid=(m // tm,),
        in_specs=[
            pl.BlockSpec((tm, d), row),
            pl.BlockSpec((1, d), const2),
            pl.BlockSpec((d, 6 * dw), const2, pipeline_mode=pl.Buffered(1)),
            pl.BlockSpec((d, LANES), lambda i: (0, 6 * dw // LANES)),
            pl.BlockSpec((1, LANES), const2),
        ],
        out_specs=[
            pl.BlockSpec((tm, 6 * dw), row),
            pl.BlockSpec((tm, dw), row),
            pl.BlockSpec((tm, dw), row),
            pl.BlockSpec((tm, dw), row),
            pl.BlockSpec((tm, dw), row),
            pl.BlockSpec((tm, n_heads), row),
        ] + extra_specs,
        out_shape=[
            jax.ShapeDtypeStruct((m, 6 * dw), BF16),
            f32_out, f32_out, f32_out, f32_out,
            jax.ShapeDtypeStruct((m, n_heads), F32),
        ] + extra_shapes,
        scratch_shapes=scratch,
        compiler_params=_params("arbitrary"),
        name="inproj",
    )(x, g, w_in, w_in, b_f)


def _triangle(n, lower):
    r = lax.broadcasted_iota(jnp.int32, (n, n), 0)
    c = lax.broadcasted_iota(jnp.int32, (n, n), 1)
    return ((r >= c) if lower else (r <= c)).astype(BF16)


def _cumsum_rows_kernel(x_ref, o_ref):
    t = x_ref.shape[1]
    tri = _triangle(LANES, lower=False)
    carry = jnp.zeros((x_ref.shape[0], 1), F32)
    for k in range(t // LANES):
        cols = slice(k * LANES, (k + 1) * LANES)
        hi, mid, lo = _split3(x_ref[:, cols])
        local = _dot(hi, tri) + _dot(mid, tri) + _dot(lo, tri)
        o_ref[:, cols] = local + carry
        carry = carry + local[:, LANES - 1:LANES]


def _cumsum_rows(x):
    return pl.pallas_call(
        _cumsum_rows_kernel,
        out_shape=jax.ShapeDtypeStruct(x.shape, F32),
        compiler_params=_params(),
        name="cumsum_rows",
    )(x)


def _t5_bucket(rel):
    nb = REL_BUCKETS // 2
    max_exact = nb // 2
    n = jnp.abs(rel)
    nf = jnp.maximum(n, 1).astype(jnp.float32)
    large = max_exact + (jnp.log(nf / max_exact) / math.log(REL_MAX_DIST / max_exact)
                         * (nb - max_exact)).astype(jnp.int32)
    large = jnp.minimum(large, nb - 1)
    return jnp.where(rel > 0, nb, 0) + jnp.where(n < max_exact, n, large)


def _bias_kernel(table_ref, idx_ref, mask_ref, o_ref, *, minus_bucket):
    h = pl.program_id(1)
    idx = idx_ref[...]
    acc = mask_ref[...]
    if minus_bucket is not None:
        acc = acc - table_ref[minus_bucket, h]
    vals = [table_ref[b, h] for b in range(REL_BUCKETS)]
    bit = 1
    while len(vals) > 1:
        odd = (idx & bit) != 0
        vals = [jnp.where(odd, vals[2 * i + 1], vals[2 * i]) for i in range(len(vals) // 2)]
        bit *= 2
    o_ref[...] = (acc + vals[0]) * LOG2E


def _bias_tiles(rel_table, qpos, kpos, keys_major=False, minus_bucket=None):
    n_heads = rel_table.shape[1]
    if keys_major:
        qp, kp = qpos[:, None, :], kpos[:, :, None]
    else:
        qp, kp = qpos[:, :, None], kpos[:, None, :]
    idx = _t5_bucket(kp - qp).astype(jnp.int32)
    mask = jnp.where((kp // CHUNK) <= (qp // CHUNK), 0.0, NEG_INF).astype(F32)
    nt, r, c = idx.shape
    return pl.pallas_call(
        functools.partial(_bias_kernel, minus_bucket=minus_bucket),
        grid=(nt, n_heads),
        in_specs=[
            pl.BlockSpec(memory_space=pltpu.SMEM),
            pl.BlockSpec((None, r, c), lambda t, h: (t, 0, 0)),
            pl.BlockSpec((None, r, c), lambda t, h: (t, 0, 0)),
        ],
        out_specs=pl.BlockSpec((None, None, r, c), lambda t, h: (h, t, 0, 0)),
        out_shape=jax.ShapeDtypeStruct((n_heads, nt, r, c), F32),
        compiler_params=_params("arbitrary", "arbitrary"),
        name="bias_tiles",
    )(rel_table.astype(F32), idx, mask)


def _far_bucket(min_dist):
    nb = REL_BUCKETS // 2
    max_exact = nb // 2
    large = max_exact + math.log(min_dist / max_exact) / math.log(REL_MAX_DIST / max_exact) * (nb - max_exact)
    assert large >= nb - 1 + 0.5, "key tile too short for a constant far-field bias"
    return nb - 1


def _lambda_value(lam_ref, lam_init):
    a = lam_ref[...]
    s1 = jnp.sum(a[0:1] * a[1:2], axis=-1, keepdims=True)
    s2 = jnp.sum(a[2:3] * a[3:4], axis=-1, keepdims=True)
    return jnp.exp(s1) - jnp.exp(s2) + lam_init


def _prompt_attn_kernel(table_ref, lam_ref, subln_ref, qd_ref, kd_ref, vd_ref, qf_ref, kf_ref, vf_ref,
                        fcol_ref, bias_ref, cmask_ref, qdn_ref, qfn_ref, od_ref, of_ref,
                        kda_ref, kfa_ref, vdt_ref, vft_ref, qda_ref, qfa_ref,
                        md_ref, accd_ref, mf_ref, accf_ref,
                        sd_ref, pd_ref, ad_ref, sf_ref, pf_ref, af_ref, cd_ref, cf_ref,
                        *, tq, tk, lam_init, far_bucket):
    h = pl.program_id(1)
    qi = pl.program_id(2)
    seq = kd_ref.shape[0]
    ratio = tq // tk

    def prepare_keys_values():
        lane = lax.broadcasted_iota(jnp.int32, (seq, LANES), 1)
        kda_ref[:, :HEAD_DIM] = kd_ref[...]
        kda_ref[:, HEAD_DIM:] = jnp.where(lane < 3, 1.0, 0.0).astype(BF16)
        kfa_ref[:, :HEAD_DIM] = kf_ref[...]
        kfa_ref[:, HEAD_DIM:] = fcol_ref[...]
        row2 = lax.broadcasted_iota(jnp.int32, (HEAD_DIM, 2 * tq), 0)
        qda_ref[HEAD_DIM:, :] = _split3_on_axis(
            jnp.full((HEAD_DIM, 2 * tq), table_ref[far_bucket, h] * LOG2E, F32), row2)

        ones = jnp.ones((ONES_ROWS, tk), BF16)

        def transpose_values(c, carry):
            r = pl.ds(pl.multiple_of(c * tk, tk), tk)
            vdt_ref[c] = jnp.concatenate([_transpose_bf16(vd_ref[r, :]), ones], axis=0)
            vft_ref[c] = jnp.concatenate([_transpose_bf16(vf_ref[r, :]), ones], axis=0)
            return carry

        lax.fori_loop(0, seq // tk, transpose_values, 0)

    def key_rows(ref, j):
        return ref[pl.ds(pl.multiple_of(j * tk, tk), tk), :]

    def buffer_scores(j, k_ref, q_aug, s_ref, c_ref, add=None):
        s = _dot(key_rows(k_ref, j), q_aug)
        if add is not None:
            s = s + add
        s_ref[...] = s
        c_ref[...] = jnp.max(s, axis=0, keepdims=True)

    def prepare_queries(qd_src_ref, qf_src_ref, tile):
        row = lax.broadcasted_iota(jnp.int32, (HEAD_DIM, tq), 0)
        qdt = qd_src_ref[...].astype(F32).T
        zero = jnp.zeros_like(qdt)
        top = jnp.concatenate([jnp.where(row < DIFF_QK_DIM, qdt, zero),
                               jnp.where(row >= DIFF_QK_DIM, qdt, zero)], axis=1).astype(BF16)
        fq = fcol_ref[pl.ds(pl.multiple_of(tile * tq, tq), tq), :].astype(F32).T
        extra = jnp.where(row < 3, -1.0, jnp.where(row < 6, pltpu.roll(fq, 3, 0), 0.0))
        qf_aug = jnp.concatenate([qf_src_ref[...].astype(F32).T.astype(BF16),
                                  extra.astype(BF16)], axis=0)
        qda_ref[:HEAD_DIM, :] = top
        qfa_ref[...] = qf_aug
        buffer_scores(0, kda_ref, qda_ref[...], sd_ref, cd_ref)
        buffer_scores(0, kfa_ref, qf_aug, sf_ref, cf_ref)

    @pl.when(qi == 0)
    def _():
        prepare_keys_values()
        prepare_queries(qd_ref, qf_ref, 0)

    branches = ((kda_ref, qda_ref, vdt_ref, sd_ref, pd_ref, ad_ref, md_ref, accd_ref, cd_ref),
                (kfa_ref, qfa_ref, vft_ref, sf_ref, pf_ref, af_ref, mf_ref, accf_ref, cf_ref))

    def pipeline_step(j_prev, adds, j_next, next_adds=(None, None)):
        for (k_ref, q_ref, vt_ref, s_ref, p_ref, a_ref, m_ref, acc_ref, c_ref), add, next_add in zip(
                branches, adds, next_adds):
            acc_ref[...] = a_ref[...] * acc_ref[...] + _dot(vt_ref[j_prev], p_ref[...])
            s = s_ref[...]
            if add is None:
                s_max = c_ref[...]
            else:
                s = s + add
                s_max = jnp.max(s, axis=0, keepdims=True)
            m_prev = m_ref[...]
            m_new = jnp.maximum(m_prev, s_max)
            p_ref[...] = jnp.exp2(s - m_new).astype(BF16)
            a_ref[...] = jnp.exp2(m_prev - m_new)
            m_ref[...] = m_new
            if j_next is not None:
                buffer_scores(j_next, k_ref, q_ref[...], s_ref, c_ref, next_add)

    for k_ref, q_ref, vt_ref, s_ref, p_ref, a_ref, m_ref, acc_ref, c_ref in branches:
        m_ref[...] = jnp.full_like(m_ref, NEG_INF)
        acc_ref[...] = jnp.zeros_like(acc_ref)
        p_ref[...] = jnp.zeros_like(p_ref)
        a_ref[...] = jnp.ones_like(a_ref)

    def far_tiles(unroll):
        def body(i, carry):
            for u in range(unroll):
                j = i * unroll + u
                pipeline_step(jnp.maximum(j - 1, 0), (None, None), j + 1)
            return carry
        return body

    first_diag = qi * ratio
    n_far = jnp.maximum(first_diag - 1, 0)
    n_main = n_far // FAR_UNROLL
    lax.fori_loop(0, n_main, far_tiles(FAR_UNROLL), 0)
    lax.fori_loop(n_main * FAR_UNROLL, n_far, far_tiles(1), 0)

    near = jnp.maximum(first_diag - 1, 0)
    gone = jnp.where(qi == 0, NEG_INF, 0.0)

    def diag_adds(u):
        bias = bias_ref[u + 1]
        return jnp.concatenate([bias, bias], axis=1), cmask_ref[u]

    bias = bias_ref[0] + gone
    pipeline_step(jnp.maximum(near - 1, 0), (jnp.concatenate([bias, bias], axis=1), gone),
                  first_diag, diag_adds(0))
    for u in range(ratio):
        more = u + 1 < ratio
        pipeline_step(near if u == 0 else first_diag + u - 1, (None, None),
                      first_diag + u + 1 if more else None, diag_adds(u + 1) if more else (None, None))
    last = first_diag + ratio - 1
    for k_ref, q_ref, vt_ref, s_ref, p_ref, a_ref, m_ref, acc_ref, c_ref in branches:
        acc_ref[...] = a_ref[...] * acc_ref[...] + _dot(vt_ref[last], p_ref[...])

    prepare_queries(qdn_ref, qfn_ref, jnp.minimum(qi + 1, pl.num_programs(2) - 1))

    lam = _lambda_value(lam_ref, lam_init)
    acc = accd_ref[...]
    num, inv = acc[:HEAD_DIM], 1.0 / acc[HEAD_DIM:HEAD_DIM + 1]
    odt = num[:, :tq] * inv[:, :tq] - lam * (num[:, tq:] * inv[:, tq:])
    od_ref[...] = (_rms_rows(odt.T, subln_ref[...]) * (1.0 - lam_init)).astype(BF16)
    acc = accf_ref[...]
    of_ref[...] = (acc[:HEAD_DIM] * (1.0 / acc[HEAD_DIM:HEAD_DIM + 1])).T.astype(BF16)


def _prompt_attention(zb, fcols, rel_table, lam_params, subln, lam_init, batch, seq, n_heads, tq, tk):
    m = batch * seq
    dw = n_heads * HEAD_DIM
    nq = seq // tq
    ratio = tq // tk
    assert tq == ratio * tk
    far_bucket = _far_bucket(tk + 1)
    vrows = HEAD_DIM + ONES_ROWS
    qpos = tk + jnp.arange(tq, dtype=jnp.int32)
    kpos = jnp.arange((ratio + 1) * tk, dtype=jnp.int32).reshape(ratio + 1, tk)
    bias = _bias_tiles(rel_table, jnp.broadcast_to(qpos, (ratio + 1, tq)), kpos,
                       keys_major=True, minus_bucket=far_bucket)
    cmask = jnp.where(kpos[1:, :, None] <= qpos[None, None, :], 0.0, NEG_INF).astype(F32)

    def q_spec(seg, ahead=0):
        return pl.BlockSpec((tq, HEAD_DIM), lambda b, h, q, seg=seg: (b * nq + jnp.minimum(q + ahead, nq - 1),
                                                                      seg * n_heads + h))

    def kv_spec(seg):
        return pl.BlockSpec((seq, HEAD_DIM), lambda b, h, q, seg=seg: (b, seg * n_heads + h))

    out_spec = pl.BlockSpec((tq, HEAD_DIM), lambda b, h, q: (b * nq + q, h))
    const2 = lambda b, h, q: (0, 0)
    kern = functools.partial(_prompt_attn_kernel, tq=tq, tk=tk, lam_init=lam_init, far_bucket=far_bucket)
    return pl.pallas_call(
        kern,
        grid=(batch, n_heads, nq),
        in_specs=[
            pl.BlockSpec(memory_space=pltpu.SMEM),
            pl.BlockSpec((4, DIFF_QK_DIM), const2),
            pl.BlockSpec((1, HEAD_DIM), const2),
            q_spec(0), kv_spec(1), kv_spec(2), q_spec(3), kv_spec(4), kv_spec(5),
            pl.BlockSpec((None, None, seq, LANES), lambda b, h, q: (b, h, 0, 0)),
            pl.BlockSpec((None, ratio + 1, tk, tq), lambda b, h, q: (h, 0, 0, 0)),
            pl.BlockSpec((ratio, tk, tq), lambda b, h, q: (0, 0, 0)),
            q_spec(0, ahead=1), q_spec(3, ahead=1),
        ],
        out_specs=[out_spec, out_spec],
        out_shape=[jax.ShapeDtypeStruct((m, dw), BF16), jax.ShapeDtypeStruct((m, dw), BF16)],
        scratch_shapes=[
            pltpu.VMEM((seq, 2 * HEAD_DIM), BF16), pltpu.VMEM((seq, 2 * HEAD_DIM), BF16),
            pltpu.VMEM((seq // tk, vrows, tk), BF16), pltpu.VMEM((seq // tk, vrows, tk), BF16),
            pltpu.VMEM((2 * HEAD_DIM, 2 * tq), BF16), pltpu.VMEM((2 * HEAD_DIM, tq), BF16),
            pltpu.VMEM((1, 2 * tq), F32), pltpu.VMEM((vrows, 2 * tq), F32),
            pltpu.VMEM((1, tq), F32), pltpu.VMEM((vrows, tq), F32),
            pltpu.VMEM((tk, 2 * tq), F32), pltpu.VMEM((tk, 2 * tq), BF16), pltpu.VMEM((1, 2 * tq), F32),
            pltpu.VMEM((tk, tq), F32), pltpu.VMEM((tk, tq), BF16), pltpu.VMEM((1, tq), F32),
            pltpu.VMEM((1, 2 * tq), F32), pltpu.VMEM((1, tq), F32),
        ],
        compiler_params=_params("arbitrary", "arbitrary", "arbitrary"),
        name="prompt_attention",
    )(rel_table.astype(F32), lam_params, subln, zb, zb, zb, zb, zb, zb, fcols, bias, cmask, zb, zb)


def _split_diff_queries(q):
    lane = lax.broadcasted_iota(jnp.int32, q.shape, 1)
    zero = jnp.zeros_like(q)
    return jnp.concatenate([jnp.where(lane < DIFF_QK_DIM, q, zero),
                            jnp.where(lane >= DIFF_QK_DIM, q, zero)], axis=0)


def _sample_attn_kernel(lam_ref, subln_ref, z_ref, kdc_ref, vdc_ref, kfc_ref, vfc_ref, frow_ref, bias_ref,
                        cmask_ref, od_ref, of_ref, *, past, t_new, n_heads, lam_init):
    dw = n_heads * HEAD_DIM

    def cached(ref, h):
        return ref[pl.ds(h, past, stride=n_heads), :].astype(BF16)

    def new_rows(seg, h):
        lo = seg * dw + h * HEAD_DIM
        return z_ref[:, lo:lo + HEAD_DIM]

    def attend(q, kc, vc, kn, vn, add_c, add_n):
        sc = _dot_nt(q, kc) + add_c
        sn = _dot_nt(q, kn) + add_n
        mx = jnp.maximum(jnp.max(sc, axis=-1, keepdims=True), jnp.max(sn, axis=-1, keepdims=True))
        pc = jnp.exp2(sc - mx)
        pn = jnp.exp2(sn - mx)
        l = jnp.sum(pc, axis=-1, keepdims=True) + jnp.sum(pn, axis=-1, keepdims=True)
        acc = _dot(pc.astype(BF16), vc) + _dot(pn.astype(BF16), vn)
        return acc, l

    lam = _lambda_value(lam_ref, lam_init)
    cmask = cmask_ref[...]
    eye = (lax.broadcasted_iota(jnp.int32, (t_new, t_new), 0)
           == lax.broadcasted_iota(jnp.int32, (t_new, t_new), 1))
    for h in range(n_heads):
        cols = slice(h * HEAD_DIM, (h + 1) * HEAD_DIM)
        qs = _split_diff_queries(new_rows(0, h))
        bias_c = bias_ref[h, :, :past]
        bias_n = bias_ref[h, :, past:past + t_new]
        acc, l = attend(qs, cached(kdc_ref, h), cached(vdc_ref, h),
                        new_rows(1, h), new_rows(2, h),
                        jnp.concatenate([bias_c, bias_c], axis=0), jnp.concatenate([bias_n, bias_n], axis=0))
        od = acc[:t_new] / l[:t_new] - lam * (acc[t_new:] / l[t_new:])
        od_ref[:, cols] = (_rms_rows(od, subln_ref[...]) * (1.0 - lam_init)).astype(BF16)

        f_new = frow_ref[h:h + 1, past:past + t_new]
        fq = jnp.sum(jnp.where(eye, jnp.broadcast_to(f_new, (t_new, t_new)), 0.0), axis=1, keepdims=True)
        dec_c = (fq - frow_ref[h:h + 1, :past]) * LOG2E
        dec_n = (fq - f_new) * LOG2E
        acc, l = attend(new_rows(3, h), cached(kfc_ref, h), cached(vfc_ref, h),
                        new_rows(4, h), new_rows(5, h), dec_c, dec_n + cmask)
        of_ref[:, cols] = (acc / l).astype(BF16)


def _sample_attention(zb, caches, layer, frow, rel_table, lam_params, subln, lam_init, nb, t_new, past, n_heads):
    dw = n_heads * HEAD_DIM
    tpad = frow.shape[-1]
    ar = jnp.arange(t_new, dtype=jnp.int32)
    bias = _bias_tiles(rel_table, (past + ar)[None], jnp.arange(tpad, dtype=jnp.int32)[None])[:, 0]
    cmask = jnp.where(ar[None, :] <= ar[:, None], 0.0, NEG_INF).astype(F32)
    caches = [c.reshape(c.shape[0], nb, past * n_heads, HEAD_DIM) for c in caches]
    cache_spec = pl.BlockSpec((None, None, past * n_heads, HEAD_DIM), lambda b: (layer, b, 0, 0))
    out_spec = pl.BlockSpec((t_new, dw), lambda b: (b, 0))
    const2 = lambda b: (0, 0)
    kern = functools.partial(_sample_attn_kernel, past=past, t_new=t_new, n_heads=n_heads, lam_init=lam_init)
    return pl.pallas_call(
        kern,
        grid=(nb,),
        in_specs=[
            pl.BlockSpec((4, DIFF_QK_DIM), const2),
            pl.BlockSpec((1, HEAD_DIM), const2),
            pl.BlockSpec((t_new, 6 * dw), lambda b: (b, 0)),
            cache_spec, cache_spec, cache_spec, cache_spec,
            pl.BlockSpec((None, n_heads, tpad), lambda b: (b, 0, 0)),
            pl.BlockSpec((n_heads, t_new, tpad), lambda b: (0, 0, 0)),
            pl.BlockSpec((t_new, t_new), const2),
        ],
        out_specs=[out_spec, out_spec],
        out_shape=[jax.ShapeDtypeStruct((nb * t_new, dw), BF16), jax.ShapeDtypeStruct((nb * t_new, dw), BF16)],
        compiler_params=_params("arbitrary"),
        name="sample_attention",
    )(lam_params, subln, zb, *caches, frow, bias, cmask)


def _merge_kernel(x_ref, g1_ref, od_ref, of_ref, wbd_ref, wbf_ref, wga_ref, wgb_ref, wout_ref, gp_ref,
                  o_ref, *, chunk):
    x = x_ref[...]
    xn = _rms_rows(x, g1_ref[...]).astype(BF16)
    od = od_ref[...]
    of = of_ref[...]
    o = None
    for c in range(x.shape[1] // chunk):
        cols = slice(c * chunk, (c + 1) * chunk)
        ga = jax.nn.sigmoid(_dot(xn, wga_ref[:, cols]))
        gb = jax.nn.sigmoid(_dot(xn, wgb_ref[:, cols]))
        u = (ga * _dot(od, wbd_ref[:, cols]) + gb * _dot(of, wbf_ref[:, cols])).astype(BF16)
        part = _dot(u, wout_ref[cols, :])
        o = part if o is None else o + part
    o_ref[...] = x + _rms_rows(o, gp_ref[...])


def _merge(x, g1, od, of, w_bd, w_bf, w_ga, w_gb, w_out, g_post, tm, chunk):
    m, d = x.shape
    dw = od.shape[1]
    row = lambda i: (i, 0)
    const2 = lambda i: (0, 0)
    resident = lambda shape: pl.BlockSpec(shape, const2, pipeline_mode=pl.Buffered(1))
    return pl.pallas_call(
        functools.partial(_merge_kernel, chunk=chunk),
        grid=(m // tm,),
        in_specs=[
            pl.BlockSpec((tm, d), row),
            pl.BlockSpec((1, d), const2),
            pl.BlockSpec((tm, dw), row),
            pl.BlockSpec((tm, dw), row),
            resident((dw, d)), resident((dw, d)), resident((d, d)), resident((d, d)), resident((d, d)),
            pl.BlockSpec((1, d), const2),
        ],
        out_specs=pl.BlockSpec((tm, d), row),
        out_shape=jax.ShapeDtypeStruct((m, d), F32),
        compiler_params=_params("arbitrary"),
        name="merge",
    )(x, g1, od, of, w_bd, w_bf, w_ga, w_gb, w_out, g_post)


def _gelu_tanh(x):
    k = -2.0 * math.sqrt(2.0 / math.pi) * LOG2E
    return x / (1.0 + jnp.exp2(x * (k + (k * 0.044715) * (x * x))))


def _ffn_kernel(*refs, seq, has_edges):
    if has_edges:
        (h_ref, g2_ref, wa_ref, wb_ref, cw_ref, cb_ref, wd_ref, gp_ref, prev_ref,
         o_ref, cs_ref, xn_ref, tail_ref) = refs
    else:
        (h_ref, g2_ref, wa_ref, wb_ref, cw_ref, cb_ref, wd_ref, gp_ref,
         o_ref, cs_ref, xn_ref, tail_ref) = refs
    i = pl.program_id(0)
    f = pl.program_id(1)
    tm = h_ref.shape[0]

    tf = wa_ref.shape[1]

    @pl.when(f == 0)
    def _():
        xn_ref[...] = _rms_rows(h_ref[...], g2_ref[...]).astype(BF16)
        o_ref[...] = jnp.zeros_like(o_ref)

    if not has_edges:
        @pl.when((i * tm) % seq == 0)
        def _():
            tail_ref[f] = jnp.zeros(tail_ref.shape[1:], F32)

    xn = xn_ref[...]

    def gated_chunk(cols):
        width = cols.stop - cols.start
        a = _dot(xn, wa_ref[:, cols])
        gate = _dot(xn, wb_ref[:, cols])
        back1 = pltpu.roll(a, 1, 0)
        back2 = pltpu.roll(a, 2, 0)
        if has_edges:
            shape3 = (tm // seq, seq, width)
            t = lax.broadcasted_iota(jnp.int32, shape3, 1)
            prev = prev_ref[:, :, cols]
            p0, p1 = prev[:, 0:1, :], prev[:, 1:2, :]
            am1 = jnp.where(t == 0, p1, back1.reshape(shape3)).reshape(tm, width)
            am2 = jnp.where(t == 0, p0, jnp.where(t == 1, p1, back2.reshape(shape3))).reshape(tm, width)
            cs_ref[:, :, cols] = a.reshape(shape3)[:, seq - (CONV_WIDTH - 1):, :]
        else:
            prev = tail_ref[f, :, cols]
            p0 = prev[SUBLANES - 2:SUBLANES - 1]
            p1 = prev[SUBLANES - 1:SUBLANES]
            top = lax.broadcasted_iota(jnp.int32, (SUBLANES, width), 0)
            am1 = jnp.concatenate([jnp.where(top == 0, p1, back1[:SUBLANES]), back1[SUBLANES:]], axis=0)
            am2 = jnp.concatenate([jnp.where(top == 0, p0, jnp.where(top == 1, p1, back2[:SUBLANES])),
                                   back2[SUBLANES:]], axis=0)
            tail_ref[f, :, cols] = a[tm - SUBLANES:, :]
            cs_ref[:, :, cols] = a[tm - (CONV_WIDTH - 1):, :][None]
        cw = cw_ref[:, cols]
        ac = cw[0:1] * am2 + cw[1:2] * am1 + cw[2:3] * a + cb_ref[:, cols]
        return (_gelu_tanh(ac) * gate).astype(BF16)

    width = tf // FFN_CHUNKS
    contrib = None
    for c in range(FFN_CHUNKS):
        cols = slice(c * width, (c + 1) * width)
        part = _dot(gated_chunk(cols), wd_ref[cols, :])
        contrib = part if contrib is None else contrib + part
    o_ref[...] += contrib

    @pl.when(f == pl.num_programs(1) - 1)
    def _():
        o_ref[...] = h_ref[...] + _rms_rows(o_ref[...], gp_ref[...])


def _ffn(h, g2, w_up, conv_w, conv_b, w_d, g_post, edges, seq, tm, tf):
    m, d = h.shape
    dff = w_d.shape[0]
    nf = dff // tf
    has_edges = edges is not None
    row = lambda i, f: (i, 0)
    col = lambda i, f: (0, f)
    const2 = lambda i, f: (0, 0)
    in_specs = [
        pl.BlockSpec((tm, d), row),
        pl.BlockSpec((1, d), const2),
        pl.BlockSpec((d, tf), col),
        pl.BlockSpec((d, tf), lambda i, f: (0, nf + f)),
        pl.BlockSpec((CONV_WIDTH, tf), col),
        pl.BlockSpec((1, tf), col),
        pl.BlockSpec((tf, d), lambda i, f: (f, 0)),
        pl.BlockSpec((1, d), const2),
    ]
    args = [h, g2, w_up, w_up, conv_w, conv_b, w_d, g_post]
    if has_edges:
        assert tm % seq == 0
        tails_per_tile = tm // seq
        in_specs.append(pl.BlockSpec((tails_per_tile, CONV_WIDTH - 1, tf), lambda i, f: (i, 0, f)))
        args.append(edges)
    else:
        assert seq % tm == 0
        tails_per_tile = 1
    n_tails = (m // tm) * tails_per_tile
    out, tails = pl.pallas_call(
        functools.partial(_ffn_kernel, seq=seq, has_edges=has_edges),
        grid=(m // tm, nf),
        in_specs=in_specs,
        out_specs=[pl.BlockSpec((tm, d), row),
                   pl.BlockSpec((tails_per_tile, CONV_WIDTH - 1, tf), lambda i, f: (i, 0, f))],
        out_shape=[jax.ShapeDtypeStruct((m, d), F32),
                   jax.ShapeDtypeStruct((n_tails, CONV_WIDTH - 1, dff), F32)],
        scratch_shapes=[pltpu.VMEM((tm, d), BF16), pltpu.VMEM((nf, SUBLANES, tf), F32)],
        compiler_params=_params("arbitrary", "arbitrary"),
        name="conv_ffn",
    )(*args)
    if not has_edges:
        tiles_per_seq = seq // tm
        tails = tails[tiles_per_seq - 1::tiles_per_seq]
    return out, tails


def _tile(n, cap):
    t = min(n, cap)
    assert n % t == 0
    return t


def _layer(layer, hp, hs, past, rel_table, lam_params, lam_init, p):
    (pre1, w_in, b_forget, subln, w_bd, w_bf, w_out, post1, pre2, w_up, conv_w, conv_b, w_down, post2) = p
    batch, seq, d = hp.shape
    nb, t_new, _ = hs.shape
    n_heads = d // (2 * HEAD_DIM)
    dw = n_heads * HEAD_DIM
    dff = w_down.shape[0]
    caches, plogf, conv_prev = past
    plen = plogf.shape[1]

    row2 = lambda v: v.reshape(1, -1).astype(F32)
    w_in_b = w_in.astype(BF16)
    b_f = jnp.pad(b_forget.astype(F32), (0, LANES - n_heads)).reshape(1, LANES)
    w_ga = w_in_b[:, 6 * dw + n_heads:6 * dw + n_heads + d]
    w_gb = w_in_b[:, 6 * dw + n_heads + d:]
    w_bd_b, w_bf_b, w_out_b = w_bd.astype(BF16), w_bf.astype(BF16), w_out.astype(BF16)
    w_up_b = w_up.astype(BF16)
    w_d = w_down.astype(BF16)
    subln2 = row2(subln)

    def dense_tail(x2, od, of, edges, t_seq):
        m = x2.shape[0]
        h1 = _merge(x2, row2(pre1), od, of, w_bd_b, w_bf_b, w_ga, w_gb, w_out_b, row2(post1),
                    _tile(m, MERGE_ROWS), _tile(d, 1024))
        return _ffn(h1, row2(pre2), w_up_b, conv_w.astype(F32), row2(conv_b), w_d, row2(post2),
                    edges, t_seq, _tile(m, 512), _tile(dff, 1024))

    heads = lambda a, n, t: a.reshape(n, t, n_heads, HEAD_DIM)

    xp = hp.reshape(batch * seq, d)
    zb, kd, vd, kf, vf, logf, fcols = _inproj(xp, row2(pre1), w_in_b, b_f, n_heads,
                                              _tile(seq, INPROJ_ROWS), seq=seq)
    od, of = _prompt_attention(zb, fcols, rel_table, lam_params, subln2, lam_init,
                               batch, seq, n_heads, _tile(seq, ATTN_QUERY_TILE), _tile(seq, ATTN_KEY_TILE))
    hp_out, conv_p = dense_tail(xp, od, of, None, seq)
    state_p = (heads(kd, batch, seq), heads(vd, batch, seq), heads(kf, batch, seq), heads(vf, batch, seq),
               logf.reshape(batch, seq, n_heads), conv_p)

    xs = hs.reshape(nb * t_new, d)
    zb, kd, vd, kf, vf, logf = _inproj(xs, row2(pre1), w_in_b, b_f, n_heads,
                                       _tile(nb * t_new, INPROJ_ROWS))
    tpad = -(-(plen + t_new) // LANES) * LANES
    flog = jnp.concatenate([jnp.swapaxes(plogf.astype(F32), 1, 2),
                            jnp.swapaxes(logf.reshape(nb, t_new, n_heads), 1, 2),
                            jnp.zeros((nb, n_heads, tpad - plen - t_new), F32)], axis=2)
    frow = _cumsum_rows(flog.reshape(nb * n_heads, tpad)).reshape(nb, n_heads, tpad)
    od, of = _sample_attention(zb, caches, layer, frow, rel_table, lam_params, subln2, lam_init,
                               nb, t_new, plen, n_heads)
    hs_out, conv_s = dense_tail(xs, od, of, conv_prev.astype(F32), t_new)
    state_s = (heads(kd, nb, t_new), heads(vd, nb, t_new), heads(kf, nb, t_new), heads(vf, nb, t_new),
               logf.reshape(nb, t_new, n_heads), conv_s)
    return hp_out.reshape(batch, seq, d), hs_out.reshape(nb, t_new, d), state_p, state_s


def kernel(x_prompt, x_sample, cache_diff_k, cache_diff_v, cache_fox_k, cache_fox_v, cache_fox_logf,
           state_ffn_conv, rel_table, pre_norm1, w_in, b_forget, lam_q1, lam_k1, lam_q2, lam_k2,
           diff_subln, w_branch_diff, w_branch_fox, w_out, post_norm1, pre_norm2, w_up, conv_w, conv_b,
           w_down, post_norm2):
    depth = w_in.shape[0]
    hp, hs = x_prompt, x_sample
    caches = (cache_diff_k, cache_diff_v, cache_fox_k, cache_fox_v)
    new_p, new_s = [], []
    for l in range(depth):
        lam_init = 0.8 - 0.6 * math.exp(-0.3 * l)
        lam_params = jnp.stack([lam_q1[l], lam_k1[l], lam_q2[l], lam_k2[l]]).astype(F32)
        params = (pre_norm1[l], w_in[l], b_forget[l], diff_subln[l], w_branch_diff[l], w_branch_fox[l],
                  w_out[l], post_norm1[l], pre_norm2[l], w_up[l], conv_w[l], conv_b[l], w_down[l],
                  post_norm2[l])
        past = (caches, cache_fox_logf[l], state_ffn_conv[l])
        hp, hs, sp, ss = _layer(l, hp, hs, past, rel_table, lam_params, lam_init, params)
        new_p.append(sp)
        new_s.append(ss)
    st = lambda lst, i: jnp.stack([e[i] for e in lst])
    return (hp, hs,
            st(new_p, 0), st(new_p, 1), st(new_p, 2), st(new_p, 3), st(new_p, 4), st(new_p, 5),
            st(new_s, 0), st(new_s, 1), st(new_s, 2), st(new_s, 3), st(new_s, 4), st(new_s, 5))
```

```python
import functools
import math

import jax
import jax.numpy as jnp
from jax import lax
from jax.experimental import pallas as pl
from jax.experimental.pallas import tpu as pltpu

HEAD_DIM = 128
DIFF_QK_DIM = HEAD_DIM // 2
CHUNK = 64
CONV_WIDTH = 3
REL_BUCKETS = 32
REL_MAX_DIST = 128
EPS = 1e-6
NEG_INF = -1e30
LOG2E = math.log2(math.e)

LANES = 128
SUBLANES = 8
ONES_ROWS = 16
ATTN_QUERY_TILE = 512
ATTN_KEY_TILE = 512
FAR_UNROLL = 3
FFN_CHUNKS = 2
MERGE_ROWS = 256
INPROJ_ROWS = 256
VMEM_LIMIT_BYTES = 60 * 1024 * 1024

F32 = jnp.float32
BF16 = jnp.bfloat16


def _params(*sem, flags=None):
    return pltpu.CompilerParams(dimension_semantics=sem, vmem_limit_bytes=VMEM_LIMIT_BYTES, flags=flags)


def _rms_rows(x, g):
    return x * lax.rsqrt(jnp.mean(x * x, axis=-1, keepdims=True) + EPS) * g


def _dot(a, b):
    return jnp.dot(a, b, preferred_element_type=F32)


def _dot_nt(a, b):
    return lax.dot_general(a, b, (((1,), (1,)), ((), ())), preferred_element_type=F32)


def _split3(x):
    hi = x.astype(BF16)
    r1 = x - hi.astype(F32)
    mid = r1.astype(BF16)
    lo = (r1 - mid.astype(F32)).astype(BF16)
    return hi, mid, lo


def _split3_on_axis(x, index, ones_after=False):
    hi, mid, lo = (v.astype(F32) for v in _split3(x))
    rest = jnp.where(index < 6, 1.0, 0.0) if ones_after else 0.0
    return jnp.where(index == 0, hi, jnp.where(index == 1, mid, jnp.where(index == 2, lo, rest))).astype(BF16)


def _transpose_bf16(x):
    return x.astype(F32).T.astype(BF16)


def _inproj_kernel(x_ref, g_ref, w_ref, wf_ref, bf_ref,
                   zb_ref, kd_ref, vd_ref, kf_ref, vf_ref, logf_ref, *rest, n_heads, q_scales, seq):
    dw = kd_ref.shape[1]
    tm = x_ref.shape[0]
    xn = _rms_rows(x_ref[...], g_ref[...]).astype(BF16)
    fl = _dot(xn, wf_ref[...]) + bf_ref[...]
    lf = jnp.minimum(fl, 0.0) - jnp.log1p(jnp.exp(-jnp.abs(fl)))
    logf_ref[...] = lf[:, :n_heads]
    if seq is not None:
        fcol_ref, carry_ref = rest

        @pl.when((pl.program_id(0) * tm) % seq == 0)
        def _():
            carry_ref[...] = jnp.zeros_like(carry_ref)

        hi, mid, lo = _split3(lf)
        tri = _triangle(tm, lower=True)
        local = _dot(tri, hi) + _dot(tri, mid) + _dot(tri, lo)
        f = (local + carry_ref[...]) * LOG2E
        carry_ref[...] = carry_ref[...] + local[tm - 1:tm, :]
        lane = lax.broadcasted_iota(jnp.int32, (tm, LANES), 1)
        for h in range(n_heads):
            fcol_ref[h] = _split3_on_axis(jnp.broadcast_to(f[:, h:h + 1], (tm, LANES)), lane, ones_after=True)
    f32_outs = {1: kd_ref, 2: vd_ref, 4: kf_ref, 5: vf_ref}
    for seg in range(6):
        cols = slice(seg * dw, (seg + 1) * dw)
        z = _dot(xn, w_ref[:, cols])
        if seg in f32_outs:
            f32_outs[seg][...] = z
        else:
            z = z * q_scales[seg]
        zb_ref[:, cols] = z.astype(BF16)


def _inproj(x, g, w_in, b_f, n_heads, tm, seq=None):
    m, d = x.shape
    dw = n_heads * HEAD_DIM
    assert (6 * dw) % LANES == 0
    row = lambda i: (i, 0)
    const2 = lambda i: (0, 0)
    f32_out = jax.ShapeDtypeStruct((m, dw), F32)
    q_scales = {0: DIFF_QK_DIM ** -0.5 * LOG2E, 3: HEAD_DIM ** -0.5 * LOG2E}
    extra_specs, extra_shapes, scratch = [], [], []
    if seq is not None:
        assert seq % tm == 0
        tiles = seq // tm
        extra_specs = [pl.BlockSpec((None, n_heads, tm, LANES), lambda i: (i // tiles, 0, i % tiles, 0))]
        extra_shapes = [jax.ShapeDtypeStruct((m // seq, n_heads, seq, LANES), BF16)]
        scratch = [pltpu.VMEM((1, LANES), F32)]
    return pl.pallas_call(
        functools.partial(_inproj_kernel, n_heads=n_heads, q_scales=q_scales, seq=seq),
        grid=(m // tm,),
        in_specs=[
            pl.BlockSpec((tm, d), row),
            pl.BlockSpec((1, d), const2),
            pl.BlockSpec((d, 6 * dw), const2, pipeline_mode=pl.Buffered(1)),
            pl.BlockSpec((d, LANES), lambda i: (0, 6 * dw // LANES)),
            pl.BlockSpec((1, LANES), const2),
        ],
        out_specs=[
            pl.BlockSpec((tm, 6 * dw), row),
            pl.BlockSpec((tm, dw), row),
            pl.BlockSpec((tm, dw), row),
            pl.BlockSpec((tm, dw), row),
            pl.BlockSpec((tm, dw), row),
            pl.BlockSpec((tm, n_heads), row),
        ] + extra_specs,
        out_shape=[
            jax.ShapeDtypeStruct((m, 6 * dw), BF16),
            f32_out, f32_out, f32_out, f32_out,
            jax.ShapeDtypeStruct((m, n_heads), F32),
        ] + extra_shapes,
        scratch_shapes=scratch,
        compiler_params=_params("arbitrary"),
        name="inproj",
    )(x, g, w_in, w_in, b_f)


def _triangle(n, lower):
    r = lax.broadcasted_iota(jnp.int32, (n, n), 0)
    c = lax.broadcasted_iota(jnp.int32, (n, n), 1)
    return ((r >= c) if lower else (r <= c)).astype(BF16)


def _cumsum_rows_kernel(x_ref, o_ref):
    t = x_ref.shape[1]
    tri = _triangle(LANES, lower=False)
    carry = jnp.zeros((x_ref.shape[0], 1), F32)
    for k in range(t // LANES):
        cols = slice(k * LANES, (k + 1) * LANES)
        hi, mid, lo = _split3(x_ref[:, cols])
        local = _dot(hi, tri) + _dot(mid, tri) + _dot(lo, tri)
        o_ref[:, cols] = local + carry
        carry = carry + local[:, LANES - 1:LANES]


def _cumsum_rows(x):
    return pl.pallas_call(
        _cumsum_rows_kernel,
        out_shape=jax.ShapeDtypeStruct(x.shape, F32),
        compiler_params=_params(),
        name="cumsum_rows",
    )(x)


def _t5_bucket(rel):
    nb = REL_BUCKETS // 2
    max_exact = nb // 2
    n = jnp.abs(rel)
    nf = jnp.maximum(n, 1).astype(jnp.float32)
    large = max_exact + (jnp.log(nf / max_exact) / math.log(REL_MAX_DIST / max_exact)
                         * (nb - max_exact)).astype(jnp.int32)
    large = jnp.minimum(large, nb - 1)
    return jnp.where(rel > 0, nb, 0) + jnp.where(n < max_exact, n, large)


def _bias_kernel(table_ref, idx_ref, mask_ref, o_ref, *, minus_bucket):
    h = pl.program_id(1)
    idx = idx_ref[...]
    acc = mask_ref[...]
    if minus_bucket is not None:
        acc = acc - table_ref[minus_bucket, h]
    vals = [table_ref[b, h] for b in range(REL_BUCKETS)]
    bit = 1
    while len(vals) > 1:
        odd = (idx & bit) != 0
        vals = [jnp.where(odd, vals[2 * i + 1], vals[2 * i]) for i in range(len(vals) // 2)]
        bit *= 2
    o_ref[...] = (acc + vals[0]) * LOG2E


def _bias_tiles(rel_table, qpos, kpos, keys_major=False, minus_bucket=None):
    n_heads = rel_table.shape[1]
    if keys_major:
        qp, kp = qpos[:, None, :], kpos[:, :, None]
    else:
        qp, kp = qpos[:, :, None], kpos[:, None, :]
    idx = _t5_bucket(kp - qp).astype(jnp.int32)
    mask = jnp.where((kp // CHUNK) <= (qp // CHUNK), 0.0, NEG_INF).astype(F32)
    nt, r, c = idx.shape
    return pl.pallas_call(
        functools.partial(_bias_kernel, minus_bucket=minus_bucket),
        grid=(nt, n_heads),
        in_specs=[
            pl.BlockSpec(memory_space=pltpu.SMEM),
            pl.BlockSpec((None, r, c), lambda t, h: (t, 0, 0)),
            pl.BlockSpec((None, r, c), lambda t, h: (t, 0, 0)),
        ],
        out_specs=pl.BlockSpec((None, None, r, c), lambda t, h: (h, t, 0, 0)),
        out_shape=jax.ShapeDtypeStruct((n_heads, nt, r, c), F32),
        compiler_params=_params("arbitrary", "arbitrary"),
        name="bias_tiles",
    )(rel_table.astype(F32), idx, mask)


def _far_bucket(min_dist):
    nb = REL_BUCKETS // 2
    max_exact = nb // 2
    large = max_exact + math.log(min_dist / max_exact) / math.log(REL_MAX_DIST / max_exact) * (nb - max_exact)
    assert large >= nb - 1 + 0.5, "key tile too short for a constant far-field bias"
    return nb - 1


def _lambda_value(lam_ref, lam_init):
    a = lam_ref[...]
    s1 = jnp.sum(a[0:1] * a[1:2], axis=-1, keepdims=True)
    s2 = jnp.sum(a[2:3] * a[3:4], axis=-1, keepdims=True)
    return jnp.exp(s1) - jnp.exp(s2) + lam_init


def _prompt_attn_kernel(table_ref, lam_ref, subln_ref, qd_ref, kd_ref, vd_ref, qf_ref, kf_ref, vf_ref,
                        fcol_ref, bias_ref, cmask_ref, qdn_ref, qfn_ref, od_ref, of_ref,
                        kda_ref, kfa_ref, vdt_ref, vft_ref, qda_ref, qfa_ref,
                        md_ref, accd_ref, mf_ref, accf_ref,
                        sd_ref, pd_ref, ad_ref, sf_ref, pf_ref, af_ref, cd_ref, cf_ref,
                        *, tq, tk, lam_init, far_bucket):
    h = pl.program_id(1)
    qi = pl.program_id(2)
    seq = kd_ref.shape[0]
    ratio = tq // tk

    def prepare_keys_values():
        lane = lax.broadcasted_iota(jnp.int32, (seq, LANES), 1)
        kda_ref[:, :HEAD_DIM] = kd_ref[...]
        kda_ref[:, HEAD_DIM:] = jnp.where(lane < 3, 1.0, 0.0).astype(BF16)
        kfa_ref[:, :HEAD_DIM] = kf_ref[...]
        kfa_ref[:, HEAD_DIM:] = fcol_ref[...]
        row2 = lax.broadcasted_iota(jnp.int32, (HEAD_DIM, 2 * tq), 0)
        qda_ref[HEAD_DIM:, :] = _split3_on_axis(
            jnp.full((HEAD_DIM, 2 * tq), table_ref[far_bucket, h] * LOG2E, F32), row2)

        ones = jnp.ones((ONES_ROWS, tk), BF16)

        def transpose_values(c, carry):
            r = pl.ds(pl.multiple_of(c * tk, tk), tk)
            vdt_ref[c] = jnp.concatenate([_transpose_bf16(vd_ref[r, :]), ones], axis=0)
            vft_ref[c] = jnp.concatenate([_transpose_bf16(vf_ref[r, :]), ones], axis=0)
            return carry

        lax.fori_loop(0, seq // tk, transpose_values, 0)

    def key_rows(ref, j):
        return ref[pl.ds(pl.multiple_of(j * tk, tk), tk), :]

    def buffer_scores(j, k_ref, q_aug, s_ref, c_ref, add=None):
        s = _dot(key_rows(k_ref, j), q_aug)
        if add is not None:
            s = s + add
        s_ref[...] = s
        c_ref[...] = jnp.max(s, axis=0, keepdims=True)

    def prepare_queries(qd_src_ref, qf_src_ref, tile):
        row = lax.broadcasted_iota(jnp.int32, (HEAD_DIM, tq), 0)
        qdt = qd_src_ref[...].astype(F32).T
        zero = jnp.zeros_like(qdt)
        top = jnp.concatenate([jnp.where(row < DIFF_QK_DIM, qdt, zero),
                               jnp.where(row >= DIFF_QK_DIM, qdt, zero)], axis=1).astype(BF16)
        fq = fcol_ref[pl.ds(pl.multiple_of(tile * tq, tq), tq), :].astype(F32).T
        extra = jnp.where(row < 3, -1.0, jnp.where(row < 6, pltpu.roll(fq, 3, 0), 0.0))
        qf_aug = jnp.concatenate([qf_src_ref[...].astype(F32).T.astype(BF16),
                                  extra.astype(BF16)], axis=0)
        qda_ref[:HEAD_DIM, :] = top
        qfa_ref[...] = qf_aug
        buffer_scores(0, kda_ref, qda_ref[...], sd_ref, cd_ref)
        buffer_scores(0, kfa_ref, qf_aug, sf_ref, cf_ref)

    @pl.when(qi == 0)
    def _():
        prepare_keys_values()
        prepare_queries(qd_ref, qf_ref, 0)

    branches = ((kda_ref, qda_ref, vdt_ref, sd_ref, pd_ref, ad_ref, md_ref, accd_ref, cd_ref),
                (kfa_ref, qfa_ref, vft_ref, sf_ref, pf_ref, af_ref, mf_ref, accf_ref, cf_ref))

    def pipeline_step(j_prev, adds, j_next, next_adds=(None, None)):
        for (k_ref, q_ref, vt_ref, s_ref, p_ref, a_ref, m_ref, acc_ref, c_ref), add, next_add in zip(
                branches, adds, next_adds):
            acc_ref[...] = a_ref[...] * acc_ref[...] + _dot(vt_ref[j_prev], p_ref[...])
            s = s_ref[...]
            if add is None:
                s_max = c_ref[...]
            else:
                s = s + add
                s_max = jnp.max(s, axis=0, keepdims=True)
            m_prev = m_ref[...]
            m_new = jnp.maximum(m_prev, s_max)
            p_ref[...] = jnp.exp2(s - m_new).astype(BF16)
            a_ref[...] = jnp.exp2(m_prev - m_new)
            m_ref[...] = m_new
            if j_next is not None:
                buffer_scores(j_next, k_ref, q_ref[...], s_ref, c_ref, next_add)

    for k_ref, q_ref, vt_ref, s_ref, p_ref, a_ref, m_ref, acc_ref, c_ref in branches:
        m_ref[...] = jnp.full_like(m_ref, NEG_INF)
        acc_ref[...] = jnp.zeros_like(acc_ref)
        p_ref[...] = jnp.zeros_like(p_ref)
        a_ref[...] = jnp.ones_like(a_ref)

    def far_tiles(unroll, start):
        def body(i, carry):
            for u in range(unroll):
                j = start + i * unroll + u
                pipeline_step(jnp.maximum(j - 1, 0), (None, None), j + 1)
            return carry
        return body

    first_diag = qi * ratio
    n_far = jnp.maximum(first_diag - 1, 0)
    done = 0
    for unroll in range(FAR_UNROLL, 0, -1):
        trips = (n_far - done) // unroll
        lax.fori_loop(0, trips, far_tiles(unroll, done), 0)
        done = done + trips * unroll

    near = jnp.maximum(first_diag - 1, 0)
    gone = jnp.where(qi == 0, NEG_INF, 0.0)

    def diag_adds(u):
        bias = bias_ref[u + 1]
        return jnp.concatenate([bias, bias], axis=1), cmask_ref[u]

    bias = bias_ref[0] + gone
    pipeline_step(jnp.maximum(near - 1, 0), (jnp.concatenate([bias, bias], axis=1), gone),
                  first_diag, diag_adds(0))
    for u in range(ratio):
        more = u + 1 < ratio
        pipeline_step(near if u == 0 else first_diag + u - 1, (None, None),
                      first_diag + u + 1 if more else None, diag_adds(u + 1) if more else (None, None))
    last = first_diag + ratio - 1
    for k_ref, q_ref, vt_ref, s_ref, p_ref, a_ref, m_ref, acc_ref, c_ref in branches:
        acc_ref[...] = a_ref[...] * acc_ref[...] + _dot(vt_ref[last], p_ref[...])

    prepare_queries(qdn_ref, qfn_ref, jnp.minimum(qi + 1, pl.num_programs(2) - 1))

    lam = _lambda_value(lam_ref, lam_init)
    acc = accd_ref[...]
    num, inv = acc[:HEAD_DIM], 1.0 / acc[HEAD_DIM:HEAD_DIM + 1]
    odt = num[:, :tq] * inv[:, :tq] - lam * (num[:, tq:] * inv[:, tq:])
    od_ref[...] = (_rms_rows(odt.T, subln_ref[...]) * (1.0 - lam_init)).astype(BF16)
    acc = accf_ref[...]
    of_ref[...] = (acc[:HEAD_DIM] * (1.0 / acc[HEAD_DIM:HEAD_DIM + 1])).T.astype(BF16)


def _prompt_attention(zb, fcols, rel_table, lam_params, subln, lam_init, batch, seq, n_heads, tq, tk):
    m = batch * seq
    dw = n_heads * HEAD_DIM
    nq = seq // tq
    ratio = tq // tk
    assert tq == ratio * tk
    far_bucket = _far_bucket(tk + 1)
    vrows = HEAD_DIM + ONES_ROWS
    qpos = tk + jnp.arange(tq, dtype=jnp.int32)
    kpos = jnp.arange((ratio + 1) * tk, dtype=jnp.int32).reshape(ratio + 1, tk)
    bias = _bias_tiles(rel_table, jnp.broadcast_to(qpos, (ratio + 1, tq)), kpos,
                       keys_major=True, minus_bucket=far_bucket)
    cmask = jnp.where(kpos[1:, :, None] <= qpos[None, None, :], 0.0, NEG_INF).astype(F32)

    def q_spec(seg, ahead=0):
        return pl.BlockSpec((tq, HEAD_DIM), lambda b, h, q, seg=seg: (b * nq + jnp.minimum(q + ahead, nq - 1),
                                                                      seg * n_heads + h))

    def kv_spec(seg):
        return pl.BlockSpec((seq, HEAD_DIM), lambda b, h, q, seg=seg: (b, seg * n_heads + h))

    out_spec = pl.BlockSpec((tq, HEAD_DIM), lambda b, h, q: (b * nq + q, h))
    const2 = lambda b, h, q: (0, 0)
    kern = functools.partial(_prompt_attn_kernel, tq=tq, tk=tk, lam_init=lam_init, far_bucket=far_bucket)
    return pl.pallas_call(
        kern,
        grid=(batch, n_heads, nq),
        in_specs=[
            pl.BlockSpec(memory_space=pltpu.SMEM),
            pl.BlockSpec((4, DIFF_QK_DIM), const2),
            pl.BlockSpec((1, HEAD_DIM), const2),
            q_spec(0), kv_spec(1), kv_spec(2), q_spec(3), kv_spec(4), kv_spec(5),
            pl.BlockSpec((None, None, seq, LANES), lambda b, h, q: (b, h, 0, 0)),
            pl.BlockSpec((None, ratio + 1, tk, tq), lambda b, h, q: (h, 0, 0, 0)),
            pl.BlockSpec((ratio, tk, tq), lambda b, h, q: (0, 0, 0)),
            q_spec(0, ahead=1), q_spec(3, ahead=1),
        ],
        out_specs=[out_spec, out_spec],
        out_shape=[jax.ShapeDtypeStruct((m, dw), BF16), jax.ShapeDtypeStruct((m, dw), BF16)],
        scratch_shapes=[
            pltpu.VMEM((seq, 2 * HEAD_DIM), BF16), pltpu.VMEM((seq, 2 * HEAD_DIM), BF16),
            pltpu.VMEM((seq // tk, vrows, tk), BF16), pltpu.VMEM((seq // tk, vrows, tk), BF16),
            pltpu.VMEM((2 * HEAD_DIM, 2 * tq), BF16), pltpu.VMEM((2 * HEAD_DIM, tq), BF16),
            pltpu.VMEM((1, 2 * tq), F32), pltpu.VMEM((vrows, 2 * tq), F32),
            pltpu.VMEM((1, tq), F32), pltpu.VMEM((vrows, tq), F32),
            pltpu.VMEM((tk, 2 * tq), F32), pltpu.VMEM((tk, 2 * tq), BF16), pltpu.VMEM((1, 2 * tq), F32),
            pltpu.VMEM((tk, tq), F32), pltpu.VMEM((tk, tq), BF16), pltpu.VMEM((1, tq), F32),
            pltpu.VMEM((1, 2 * tq), F32), pltpu.VMEM((1, tq), F32),
        ],
        compiler_params=_params("arbitrary", "arbitrary", "arbitrary"),
        name="prompt_attention",
    )(rel_table.astype(F32), lam_params, subln, zb, zb, zb, zb, zb, zb, fcols, bias, cmask, zb, zb)


def _split_diff_queries(q):
    lane = lax.broadcasted_iota(jnp.int32, q.shape, 1)
    zero = jnp.zeros_like(q)
    return jnp.concatenate([jnp.where(lane < DIFF_QK_DIM, q, zero),
                            jnp.where(lane >= DIFF_QK_DIM, q, zero)], axis=0)


def _sample_attn_kernel(lam_ref, subln_ref, z_ref, kdc_ref, vdc_ref, kfc_ref, vfc_ref, frow_ref, bias_ref,
                        cmask_ref, od_ref, of_ref, *, past, t_new, n_heads, lam_init):
    dw = n_heads * HEAD_DIM

    def cached(ref, h):
        return ref[pl.ds(h, past, stride=n_heads), :].astype(BF16)

    def new_rows(seg, h):
        lo = seg * dw + h * HEAD_DIM
        return z_ref[:, lo:lo + HEAD_DIM]

    def attend(q, kc, vc, kn, vn, add_c, add_n):
        sc = _dot_nt(q, kc) + add_c
        sn = _dot_nt(q, kn) + add_n
        mx = jnp.maximum(jnp.max(sc, axis=-1, keepdims=True), jnp.max(sn, axis=-1, keepdims=True))
        pc = jnp.exp2(sc - mx)
        pn = jnp.exp2(sn - mx)
        l = jnp.sum(pc, axis=-1, keepdims=True) + jnp.sum(pn, axis=-1, keepdims=True)
        acc = _dot(pc.astype(BF16), vc) + _dot(pn.astype(BF16), vn)
        return acc, l

    lam = _lambda_value(lam_ref, lam_init)
    cmask = cmask_ref[...]
    eye = (lax.broadcasted_iota(jnp.int32, (t_new, t_new), 0)
           == lax.broadcasted_iota(jnp.int32, (t_new, t_new), 1))
    for h in range(n_heads):
        cols = slice(h * HEAD_DIM, (h + 1) * HEAD_DIM)
        qs = _split_diff_queries(new_rows(0, h))
        bias_c = bias_ref[h, :, :past]
        bias_n = bias_ref[h, :, past:past + t_new]
        acc, l = attend(qs, cached(kdc_ref, h), cached(vdc_ref, h),
                        new_rows(1, h), new_rows(2, h),
                        jnp.concatenate([bias_c, bias_c], axis=0), jnp.concatenate([bias_n, bias_n], axis=0))
        od = acc[:t_new] / l[:t_new] - lam * (acc[t_new:] / l[t_new:])
        od_ref[:, cols] = (_rms_rows(od, subln_ref[...]) * (1.0 - lam_init)).astype(BF16)

        f_new = frow_ref[h:h + 1, past:past + t_new]
        fq = jnp.sum(jnp.where(eye, jnp.broadcast_to(f_new, (t_new, t_new)), 0.0), axis=1, keepdims=True)
        dec_c = (fq - frow_ref[h:h + 1, :past]) * LOG2E
        dec_n = (fq - f_new) * LOG2E
        acc, l = attend(new_rows(3, h), cached(kfc_ref, h), cached(vfc_ref, h),
                        new_rows(4, h), new_rows(5, h), dec_c, dec_n + cmask)
        of_ref[:, cols] = (acc / l).astype(BF16)


def _sample_attention(zb, caches, layer, frow, rel_table, lam_params, subln, lam_init, nb, t_new, past, n_heads):
    dw = n_heads * HEAD_DIM
    tpad = frow.shape[-1]
    ar = jnp.arange(t_new, dtype=jnp.int32)
    bias = _bias_tiles(rel_table, (past + ar)[None], jnp.arange(tpad, dtype=jnp.int32)[None])[:, 0]
    cmask = jnp.where(ar[None, :] <= ar[:, None], 0.0, NEG_INF).astype(F32)
    caches = [c.reshape(c.shape[0], nb, past * n_heads, HEAD_DIM) for c in caches]
    cache_spec = pl.BlockSpec((None, None, past * n_heads, HEAD_DIM), lambda b: (layer, b, 0, 0))
    out_spec = pl.BlockSpec((t_new, dw), lambda b: (b, 0))
    const2 = lambda b: (0, 0)
    kern = functools.partial(_sample_attn_kernel, past=past, t_new=t_new, n_heads=n_heads, lam_init=lam_init)
    return pl.pallas_call(
        kern,
        grid=(nb,),
        in_specs=[
            pl.BlockSpec((4, DIFF_QK_DIM), const2),
            pl.BlockSpec((1, HEAD_DIM), const2),
            pl.BlockSpec((t_new, 6 * dw), lambda b: (b, 0)),
            cache_spec, cache_spec, cache_spec, cache_spec,
            pl.BlockSpec((None, n_heads, tpad), lambda b: (b, 0, 0)),
            pl.BlockSpec((n_heads, t_new, tpad), lambda b: (0, 0, 0)),
            pl.BlockSpec((t_new, t_new), const2),
        ],
        out_specs=[out_spec, out_spec],
        out_shape=[jax.ShapeDtypeStruct((nb * t_new, dw), BF16), jax.ShapeDtypeStruct((nb * t_new, dw), BF16)],
        compiler_params=_params("arbitrary"),
        name="sample_attention",
    )(lam_params, subln, zb, *caches, frow, bias, cmask)


def _merge_kernel(x_ref, g1_ref, od_ref, of_ref, wbd_ref, wbf_ref, wga_ref, wgb_ref, wout_ref, gp_ref,
                  o_ref, *, chunk):
    x = x_ref[...]
    xn = _rms_rows(x, g1_ref[...]).astype(BF16)
    od = od_ref[...]
    of = of_ref[...]
    o = None
    for c in range(x.shape[1] // chunk):
        cols = slice(c * chunk, (c + 1) * chunk)
        ga = jax.nn.sigmoid(_dot(xn, wga_ref[:, cols]))
        gb = jax.nn.sigmoid(_dot(xn, wgb_ref[:, cols]))
        u = (ga * _dot(od, wbd_ref[:, cols]) + gb * _dot(of, wbf_ref[:, cols])).astype(BF16)
        part = _dot(u, wout_ref[cols, :])
        o = part if o is None else o + part
    o_ref[...] = x + _rms_rows(o, gp_ref[...])


def _merge(x, g1, od, of, w_bd, w_bf, w_ga, w_gb, w_out, g_post, tm, chunk):
    m, d = x.shape
    dw = od.shape[1]
    row = lambda i: (i, 0)
    const2 = lambda i: (0, 0)
    resident = lambda shape: pl.BlockSpec(shape, const2, pipeline_mode=pl.Buffered(1))
    return pl.pallas_call(
        functools.partial(_merge_kernel, chunk=chunk),
        grid=(m // tm,),
        in_specs=[
            pl.BlockSpec((tm, d), row),
            pl.BlockSpec((1, d), const2),
            pl.BlockSpec((tm, dw), row),
            pl.BlockSpec((tm, dw), row),
            resident((dw, d)), resident((dw, d)), resident((d, d)), resident((d, d)), resident((d, d)),
            pl.BlockSpec((1, d), const2),
        ],
        out_specs=pl.BlockSpec((tm, d), row),
        out_shape=jax.ShapeDtypeStruct((m, d), F32),
        compiler_params=_params("arbitrary"),
        name="merge",
    )(x, g1, od, of, w_bd, w_bf, w_ga, w_gb, w_out, g_post)


def _gelu_tanh(x):
    k = -2.0 * math.sqrt(2.0 / math.pi) * LOG2E
    return x / (1.0 + jnp.exp2(x * (k + (k * 0.044715) * (x * x))))


def _ffn_kernel(*refs, seq, has_edges):
    if has_edges:
        (h_ref, g2_ref, wa_ref, wb_ref, cw_ref, cb_ref, wd_ref, gp_ref, prev_ref,
         o_ref, cs_ref, xn_ref, tail_ref) = refs
    else:
        (h_ref, g2_ref, wa_ref, wb_ref, cw_ref, cb_ref, wd_ref, gp_ref,
         o_ref, cs_ref, xn_ref, tail_ref) = refs
    i = pl.program_id(0)
    f = pl.program_id(1)
    tm = h_ref.shape[0]

    tf = wa_ref.shape[1]

    @pl.when(f == 0)
    def _():
        xn_ref[...] = _rms_rows(h_ref[...], g2_ref[...]).astype(BF16)
        o_ref[...] = jnp.zeros_like(o_ref)

    if not has_edges:
        @pl.when((i * tm) % seq == 0)
        def _():
            tail_ref[f] = jnp.zeros(tail_ref.shape[1:], F32)

    xn = xn_ref[...]

    def gated_chunk(cols):
        width = cols.stop - cols.start
        a = _dot(xn, wa_ref[:, cols])
        gate = _dot(xn, wb_ref[:, cols])
        back1 = pltpu.roll(a, 1, 0)
        back2 = pltpu.roll(a, 2, 0)
        if has_edges:
            shape3 = (tm // seq, seq, width)
            t = lax.broadcasted_iota(jnp.int32, shape3, 1)
            prev = prev_ref[:, :, cols]
            p0, p1 = prev[:, 0:1, :], prev[:, 1:2, :]
            am1 = jnp.where(t == 0, p1, back1.reshape(shape3)).reshape(tm, width)
            am2 = jnp.where(t == 0, p0, jnp.where(t == 1, p1, back2.reshape(shape3))).reshape(tm, width)
            cs_ref[:, :, cols] = a.reshape(shape3)[:, seq - (CONV_WIDTH - 1):, :]
        else:
            prev = tail_ref[f, :, cols]
            p0 = prev[SUBLANES - 2:SUBLANES - 1]
            p1 = prev[SUBLANES - 1:SUBLANES]
            top = lax.broadcasted_iota(jnp.int32, (SUBLANES, width), 0)
            am1 = jnp.concatenate([jnp.where(top == 0, p1, back1[:SUBLANES]), back1[SUBLANES:]], axis=0)
            am2 = jnp.concatenate([jnp.where(top == 0, p0, jnp.where(top == 1, p1, back2[:SUBLANES])),
                                   back2[SUBLANES:]], axis=0)
            tail_ref[f, :, cols] = a[tm - SUBLANES:, :]
            cs_ref[:, :, cols] = a[tm - (CONV_WIDTH - 1):, :][None]
        cw = cw_ref[:, cols]
        ac = cw[0:1] * am2 + cw[1:2] * am1 + cw[2:3] * a + cb_ref[:, cols]
        return (_gelu_tanh(ac) * gate).astype(BF16)

    width = tf // FFN_CHUNKS
    contrib = None
    for c in range(FFN_CHUNKS):
        cols = slice(c * width, (c + 1) * width)
        part = _dot(gated_chunk(cols), wd_ref[cols, :])
        contrib = part if contrib is None else contrib + part
    o_ref[...] += contrib

    @pl.when(f == pl.num_programs(1) - 1)
    def _():
        o_ref[...] = h_ref[...] + _rms_rows(o_ref[...], gp_ref[...])


def _ffn(h, g2, w_up, conv_w, conv_b, w_d, g_post, edges, seq, tm, tf):
    m, d = h.shape
    dff = w_d.shape[0]
    nf = dff // tf
    has_edges = edges is not None
    row = lambda i, f: (i, 0)
    col = lambda i, f: (0, f)
    const2 = lambda i, f: (0, 0)
    in_specs = [
        pl.BlockSpec((tm, d), row),
        pl.BlockSpec((1, d), const2),
        pl.BlockSpec((d, tf), col),
        pl.BlockSpec((d, tf), lambda i, f: (0, nf + f)),
        pl.BlockSpec((CONV_WIDTH, tf), col),
        pl.BlockSpec((1, tf), col),
        pl.BlockSpec((tf, d), lambda i, f: (f, 0)),
        pl.BlockSpec((1, d), const2),
    ]
    args = [h, g2, w_up, w_up, conv_w, conv_b, w_d, g_post]
    if has_edges:
        assert tm % seq == 0
        tails_per_tile = tm // seq
        in_specs.append(pl.BlockSpec((tails_per_tile, CONV_WIDTH - 1, tf), lambda i, f: (i, 0, f)))
        args.append(edges)
    else:
        assert seq % tm == 0
        tails_per_tile = 1
    n_tails = (m // tm) * tails_per_tile
    out, tails = pl.pallas_call(
        functools.partial(_ffn_kernel, seq=seq, has_edges=has_edges),
        grid=(m // tm, nf),
        in_specs=in_specs,
        out_specs=[pl.BlockSpec((tm, d), row),
                   pl.BlockSpec((tails_per_tile, CONV_WIDTH - 1, tf), lambda i, f: (i, 0, f))],
        out_shape=[jax.ShapeDtypeStruct((m, d), F32),
                   jax.ShapeDtypeStruct((n_tails, CONV_WIDTH - 1, dff), F32)],
        scratch_shapes=[pltpu.VMEM((tm, d), BF16), pltpu.VMEM((nf, SUBLANES, tf), F32)],
        compiler_params=_params("arbitrary", "arbitrary"),
        name="conv_ffn",
    )(*args)
    if not has_edges:
        tiles_per_seq = seq // tm
        tails = tails[tiles_per_seq - 1::tiles_per_seq]
    return out, tails


def _tile(n, cap):
    t = min(n, cap)
    assert n % t == 0
    return t


def _layer(layer, hp, hs, past, rel_table, lam_params, lam_init, p):
    (pre1, w_in, b_forget, subln, w_bd, w_bf, w_out, post1, pre2, w_up, conv_w, conv_b, w_down, post2) = p
    batch, seq, d = hp.shape
    nb, t_new, _ = hs.shape
    n_heads = d // (2 * HEAD_DIM)
    dw = n_heads * HEAD_DIM
    dff = w_down.shape[0]
    caches, plogf, conv_prev = past
    plen = plogf.shape[1]

    row2 = lambda v: v.reshape(1, -1).astype(F32)
    w_in_b = w_in.astype(BF16)
    b_f = jnp.pad(b_forget.astype(F32), (0, LANES - n_heads)).reshape(1, LANES)
    w_ga = w_in_b[:, 6 * dw + n_heads:6 * dw + n_heads + d]
    w_gb = w_in_b[:, 6 * dw + n_heads + d:]
    w_bd_b, w_bf_b, w_out_b = w_bd.astype(BF16), w_bf.astype(BF16), w_out.astype(BF16)
    w_up_b = w_up.astype(BF16)
    w_d = w_down.astype(BF16)
    subln2 = row2(subln)

    def dense_tail(x2, od, of, edges, t_seq):
        m = x2.shape[0]
        h1 = _merge(x2, row2(pre1), od, of, w_bd_b, w_bf_b, w_ga, w_gb, w_out_b, row2(post1),
                    _tile(m, MERGE_ROWS), _tile(d, 1024))
        return _ffn(h1, row2(pre2), w_up_b, conv_w.astype(F32), row2(conv_b), w_d, row2(post2),
                    edges, t_seq, _tile(m, 512), _tile(dff, 1024))

    heads = lambda a, n, t: a.reshape(n, t, n_heads, HEAD_DIM)

    xp = hp.reshape(batch * seq, d)
    zb, kd, vd, kf, vf, logf, fcols = _inproj(xp, row2(pre1), w_in_b, b_f, n_heads,
                                              _tile(seq, INPROJ_ROWS), seq=seq)
    od, of = _prompt_attention(zb, fcols, rel_table, lam_params, subln2, lam_init,
                               batch, seq, n_heads, _tile(seq, ATTN_QUERY_TILE), _tile(seq, ATTN_KEY_TILE))
    hp_out, conv_p = dense_tail(xp, od, of, None, seq)
    state_p = (heads(kd, batch, seq), heads(vd, batch, seq), heads(kf, batch, seq), heads(vf, batch, seq),
               logf.reshape(batch, seq, n_heads), conv_p)

    xs = hs.reshape(nb * t_new, d)
    zb, kd, vd, kf, vf, logf = _inproj(xs, row2(pre1), w_in_b, b_f, n_heads,
                                       _tile(nb * t_new, INPROJ_ROWS))
    tpad = -(-(plen + t_new) // LANES) * LANES
    flog = jnp.concatenate([jnp.swapaxes(plogf.astype(F32), 1, 2),
                            jnp.swapaxes(logf.reshape(nb, t_new, n_heads), 1, 2),
                            jnp.zeros((nb, n_heads, tpad - plen - t_new), F32)], axis=2)
    frow = _cumsum_rows(flog.reshape(nb * n_heads, tpad)).reshape(nb, n_heads, tpad)
    od, of = _sample_attention(zb, caches, layer, frow, rel_table, lam_params, subln2, lam_init,
                               nb, t_new, plen, n_heads)
    hs_out, conv_s = dense_tail(xs, od, of, conv_prev.astype(F32), t_new)
    state_s = (heads(kd, nb, t_new), heads(vd, nb, t_new), heads(kf, nb, t_new), heads(vf, nb, t_new),
               logf.reshape(nb, t_new, n_heads), conv_s)
    return hp_out.reshape(batch, seq, d), hs_out.reshape(nb, t_new, d), state_p, state_s


def kernel(x_prompt, x_sample, cache_diff_k, cache_diff_v, cache_fox_k, cache_fox_v, cache_fox_logf,
           state_ffn_conv, rel_table, pre_norm1, w_in, b_forget, lam_q1, lam_k1, lam_q2, lam_k2,
           diff_subln, w_branch_diff, w_branch_fox, w_out, post_norm1, pre_norm2, w_up, conv_w, conv_b,
           w_down, post_norm2):
    depth = w_in.shape[0]
    hp, hs = x_prompt, x_sample
    caches = (cache_diff_k, cache_diff_v, cache_fox_k, cache_fox_v)
    new_p, new_s = [], []
    for l in range(depth):
        lam_init = 0.8 - 0.6 * math.exp(-0.3 * l)
        lam_params = jnp.stack([lam_q1[l], lam_k1[l], lam_q2[l], lam_k2[l]]).astype(F32)
        params = (pre_norm1[l], w_in[l], b_forget[l], diff_subln[l], w_branch_diff[l], w_branch_fox[l],
                  w_out[l], post_norm1[l], pre_norm2[l], w_up[l], conv_w[l], conv_b[l], w_down[l],
                  post_norm2[l])
        past = (caches, cache_fox_logf[l], state_ffn_conv[l])
        hp, hs, sp, ss = _layer(l, hp, hs, past, rel_table, lam_params, lam_init, params)
        new_p.append(sp)
        new_s.append(ss)
    st = lambda lst, i: jnp.stack([e[i] for e in lst])
    return (hp, hs,
            st(new_p, 0), st(new_p, 1), st(new_p, 2), st(new_p, 3), st(new_p, 4), st(new_p, 5),
            st(new_s, 0), st(new_s, 1), st(new_s, 2), st(new_s, 3), st(new_s, 4), st(new_s, 5))
```

```python
import functools
import math

import jax
import jax.numpy as jnp
from jax import lax
from jax.experimental import pallas as pl
from jax.experimental.pallas import tpu as pltpu

HEAD_DIM = 128
DIFF_QK_DIM = HEAD_DIM // 2
CHUNK = 64
CONV_WIDTH = 3
REL_BUCKETS = 32
REL_MAX_DIST = 128
EPS = 1e-6
NEG_INF = -1e30
LOG2E = math.log2(math.e)

LANES = 128
SUBLANES = 8
ONES_ROWS = 16
ATTN_QUERY_TILE = 512
ATTN_KEY_TILE = 512
FAR_UNROLL = 3
CACHE_BUFFERS = 3
FFN_CHUNKS = 2
MERGE_ROWS = 256
INPROJ_ROWS = 256
VMEM_LIMIT_BYTES = 60 * 1024 * 1024

F32 = jnp.float32
BF16 = jnp.bfloat16


def _params(*sem, flags=None):
    return pltpu.CompilerParams(dimension_semantics=sem, vmem_limit_bytes=VMEM_LIMIT_BYTES, flags=flags)


def _rms_rows(x, g):
    return x * lax.rsqrt(jnp.mean(x * x, axis=-1, keepdims=True) + EPS) * g


def _dot(a, b):
    return jnp.dot(a, b, preferred_element_type=F32)


def _dot_nt(a, b):
    return lax.dot_general(a, b, (((1,), (1,)), ((), ())), preferred_element_type=F32)


def _split3(x):
    hi = x.astype(BF16)
    r1 = x - hi.astype(F32)
    mid = r1.astype(BF16)
    lo = (r1 - mid.astype(F32)).astype(BF16)
    return hi, mid, lo


def _split3_on_axis(x, index, ones_after=False):
    hi, mid, lo = (v.astype(F32) for v in _split3(x))
    rest = jnp.where(index < 6, 1.0, 0.0) if ones_after else 0.0
    return jnp.where(index == 0, hi, jnp.where(index == 1, mid, jnp.where(index == 2, lo, rest))).astype(BF16)


def _transpose_bf16(x):
    return x.astype(F32).T.astype(BF16)


def _inproj_kernel(x_ref, g_ref, w_ref, wf_ref, bf_ref,
                   zb_ref, kd_ref, vd_ref, kf_ref, vf_ref, logf_ref, *rest, n_heads, q_scales, seq):
    dw = kd_ref.shape[1]
    tm = x_ref.shape[0]
    xn = _rms_rows(x_ref[...], g_ref[...]).astype(BF16)
    fl = _dot(xn, wf_ref[...]) + bf_ref[...]
    lf = jnp.minimum(fl, 0.0) - jnp.log1p(jnp.exp(-jnp.abs(fl)))
    logf_ref[...] = lf[:, :n_heads]
    if seq is not None:
        fcol_ref, carry_ref = rest

        @pl.when((pl.program_id(0) * tm) % seq == 0)
        def _():
            carry_ref[...] = jnp.zeros_like(carry_ref)

        hi, mid, lo = _split3(lf)
        tri = _triangle(tm, lower=True)
        local = _dot(tri, hi) + _dot(tri, mid) + _dot(tri, lo)
        f = (local + carry_ref[...]) * LOG2E
        carry_ref[...] = carry_ref[...] + local[tm - 1:tm, :]
        lane = lax.broadcasted_iota(jnp.int32, (tm, LANES), 1)
        for h in range(n_heads):
            fcol_ref[h] = _split3_on_axis(jnp.broadcast_to(f[:, h:h + 1], (tm, LANES)), lane, ones_after=True)
    f32_outs = {1: kd_ref, 2: vd_ref, 4: kf_ref, 5: vf_ref}
    for seg in range(6):
        cols = slice(seg * dw, (seg + 1) * dw)
        z = _dot(xn, w_ref[:, cols])
        if seg in f32_outs:
            f32_outs[seg][...] = z
        else:
            z = z * q_scales[seg]
        zb_ref[:, cols] = z.astype(BF16)


def _inproj(x, g, w_in, b_f, n_heads, tm, seq=None):
    m, d = x.shape
    dw = n_heads * HEAD_DIM
    assert (6 * dw) % LANES == 0
    row = lambda i: (i, 0)
    const2 = lambda i: (0, 0)
    f32_out = jax.ShapeDtypeStruct((m, dw), F32)
    q_scales = {0: DIFF_QK_DIM ** -0.5 * LOG2E, 3: HEAD_DIM ** -0.5 * LOG2E}
    extra_specs, extra_shapes, scratch = [], [], []
    if seq is not None:
        assert seq % tm == 0
        tiles = seq // tm
        extra_specs = [pl.BlockSpec((None, n_heads, tm, LANES), lambda i: (i // tiles, 0, i % tiles, 0))]
        extra_shapes = [jax.ShapeDtypeStruct((m // seq, n_heads, seq, LANES), BF16)]
        scratch = [pltpu.VMEM((1, LANES), F32)]
    return pl.pallas_call(
        functools.partial(_inproj_kernel, n_heads=n_heads, q_scales=q_scales, seq=seq),
        grid=(m // tm,),
        in_specs=[
            pl.BlockSpec((tm, d), row),
            pl.BlockSpec((1, d), const2),
            pl.BlockSpec((d, 6 * dw), const2, pipeline_mode=pl.Buffered(1)),
            pl.BlockSpec((d, LANES), lambda i: (0, 6 * dw // LANES)),
            pl.BlockSpec((1, LANES), const2),
        ],
        out_specs=[
            pl.BlockSpec((tm, 6 * dw), row),
            pl.BlockSpec((tm, dw), row),
            pl.BlockSpec((tm, dw), row),
            pl.BlockSpec((tm, dw), row),
            pl.BlockSpec((tm, dw), row),
            pl.BlockSpec((tm, n_heads), row),
        ] + extra_specs,
        out_shape=[
            jax.ShapeDtypeStruct((m, 6 * dw), BF16),
            f32_out, f32_out, f32_out, f32_out,
            jax.ShapeDtypeStruct((m, n_heads), F32),
        ] + extra_shapes,
        scratch_shapes=scratch,
        compiler_params=_params("arbitrary"),
        name="inproj",
    )(x, g, w_in, w_in, b_f)


def _triangle(n, lower):
    r = lax.broadcasted_iota(jnp.int32, (n, n), 0)
    c = lax.broadcasted_iota(jnp.int32, (n, n), 1)
    return ((r >= c) if lower else (r <= c)).astype(BF16)


def _cumsum_rows_kernel(x_ref, o_ref):
    t = x_ref.shape[1]
    tri = _triangle(LANES, lower=False)
    carry = jnp.zeros((x_ref.shape[0], 1), F32)
    for k in range(t // LANES):
        cols = slice(k * LANES, (k + 1) * LANES)
        hi, mid, lo = _split3(x_ref[:, cols])
        local = _dot(hi, tri) + _dot(mid, tri) + _dot(lo, tri)
        o_ref[:, cols] = local + carry
        carry = carry + local[:, LANES - 1:LANES]


def _cumsum_rows(x):
    return pl.pallas_call(
        _cumsum_rows_kernel,
        out_shape=jax.ShapeDtypeStruct(x.shape, F32),
        compiler_params=_params(),
        name="cumsum_rows",
    )(x)


def _t5_bucket(rel):
    nb = REL_BUCKETS // 2
    max_exact = nb // 2
    n = jnp.abs(rel)
    nf = jnp.maximum(n, 1).astype(jnp.float32)
    large = max_exact + (jnp.log(nf / max_exact) / math.log(REL_MAX_DIST / max_exact)
                         * (nb - max_exact)).astype(jnp.int32)
    large = jnp.minimum(large, nb - 1)
    return jnp.where(rel > 0, nb, 0) + jnp.where(n < max_exact, n, large)


def _bias_kernel(table_ref, idx_ref, mask_ref, o_ref, *, minus_bucket):
    h = pl.program_id(1)
    idx = idx_ref[...]
    acc = mask_ref[...]
    if minus_bucket is not None:
        acc = acc - table_ref[minus_bucket, h]
    vals = [table_ref[b, h] for b in range(REL_BUCKETS)]
    bit = 1
    while len(vals) > 1:
        odd = (idx & bit) != 0
        vals = [jnp.where(odd, vals[2 * i + 1], vals[2 * i]) for i in range(len(vals) // 2)]
        bit *= 2
    o_ref[...] = (acc + vals[0]) * LOG2E


def _bias_tiles(rel_table, qpos, kpos, keys_major=False, minus_bucket=None):
    n_heads = rel_table.shape[1]
    if keys_major:
        qp, kp = qpos[:, None, :], kpos[:, :, None]
    else:
        qp, kp = qpos[:, :, None], kpos[:, None, :]
    idx = _t5_bucket(kp - qp).astype(jnp.int32)
    mask = jnp.where((kp // CHUNK) <= (qp // CHUNK), 0.0, NEG_INF).astype(F32)
    nt, r, c = idx.shape
    return pl.pallas_call(
        functools.partial(_bias_kernel, minus_bucket=minus_bucket),
        grid=(nt, n_heads),
        in_specs=[
            pl.BlockSpec(memory_space=pltpu.SMEM),
            pl.BlockSpec((None, r, c), lambda t, h: (t, 0, 0)),
            pl.BlockSpec((None, r, c), lambda t, h: (t, 0, 0)),
        ],
        out_specs=pl.BlockSpec((None, None, r, c), lambda t, h: (h, t, 0, 0)),
        out_shape=jax.ShapeDtypeStruct((n_heads, nt, r, c), F32),
        compiler_params=_params("arbitrary", "arbitrary"),
        name="bias_tiles",
    )(rel_table.astype(F32), idx, mask)


def _far_bucket(min_dist):
    nb = REL_BUCKETS // 2
    max_exact = nb // 2
    large = max_exact + math.log(min_dist / max_exact) / math.log(REL_MAX_DIST / max_exact) * (nb - max_exact)
    assert large >= nb - 1 + 0.5, "key tile too short for a constant far-field bias"
    return nb - 1


def _lambda_value(lam_ref, lam_init):
    a = lam_ref[...]
    s1 = jnp.sum(a[0:1] * a[1:2], axis=-1, keepdims=True)
    s2 = jnp.sum(a[2:3] * a[3:4], axis=-1, keepdims=True)
    return jnp.exp(s1) - jnp.exp(s2) + lam_init


def _prompt_attn_kernel(table_ref, lam_ref, subln_ref, qd_ref, kd_ref, vd_ref, qf_ref, kf_ref, vf_ref,
                        fcol_ref, bias_ref, cmask_ref, qdn_ref, qfn_ref, od_ref, of_ref,
                        kda_ref, kfa_ref, vdt_ref, vft_ref, qda_ref, qfa_ref,
                        md_ref, accd_ref, mf_ref, accf_ref,
                        sd_ref, pd_ref, ad_ref, sf_ref, pf_ref, af_ref, cd_ref, cf_ref,
                        *, tq, tk, lam_init, far_bucket):
    h = pl.program_id(1)
    qi = pl.program_id(2)
    seq = kd_ref.shape[0]
    ratio = tq // tk

    def prepare_keys_values():
        lane = lax.broadcasted_iota(jnp.int32, (seq, LANES), 1)
        kda_ref[:, :HEAD_DIM] = kd_ref[...]
        kda_ref[:, HEAD_DIM:] = jnp.where(lane < 3, 1.0, 0.0).astype(BF16)
        kfa_ref[:, :HEAD_DIM] = kf_ref[...]
        kfa_ref[:, HEAD_DIM:] = fcol_ref[...]
        row2 = lax.broadcasted_iota(jnp.int32, (HEAD_DIM, 2 * tq), 0)
        qda_ref[HEAD_DIM:, :] = _split3_on_axis(
            jnp.full((HEAD_DIM, 2 * tq), table_ref[far_bucket, h] * LOG2E, F32), row2)

        ones = jnp.ones((ONES_ROWS, tk), BF16)

        def transpose_values(c, carry):
            r = pl.ds(pl.multiple_of(c * tk, tk), tk)
            vdt_ref[c] = jnp.concatenate([_transpose_bf16(vd_ref[r, :]), ones], axis=0)
            vft_ref[c] = jnp.concatenate([_transpose_bf16(vf_ref[r, :]), ones], axis=0)
            return carry

        lax.fori_loop(0, seq // tk, transpose_values, 0)

    def key_rows(ref, j):
        return ref[pl.ds(pl.multiple_of(j * tk, tk), tk), :]

    def buffer_scores(j, k_ref, q_aug, s_ref, c_ref, add=None):
        s = _dot(key_rows(k_ref, j), q_aug)
        if add is not None:
            s = s + add
        s_ref[...] = s
        c_ref[...] = jnp.max(s, axis=0, keepdims=True)

    def prepare_queries(qd_src_ref, qf_src_ref, tile):
        row = lax.broadcasted_iota(jnp.int32, (HEAD_DIM, tq), 0)
        qdt = qd_src_ref[...].astype(F32).T
        zero = jnp.zeros_like(qdt)
        top = jnp.concatenate([jnp.where(row < DIFF_QK_DIM, qdt, zero),
                               jnp.where(row >= DIFF_QK_DIM, qdt, zero)], axis=1).astype(BF16)
        fq = fcol_ref[pl.ds(pl.multiple_of(tile * tq, tq), tq), :].astype(F32).T
        extra = jnp.where(row < 3, -1.0, jnp.where(row < 6, pltpu.roll(fq, 3, 0), 0.0))
        qf_aug = jnp.concatenate([qf_src_ref[...].astype(F32).T.astype(BF16),
                                  extra.astype(BF16)], axis=0)
        qda_ref[:HEAD_DIM, :] = top
        qfa_ref[...] = qf_aug
        buffer_scores(0, kda_ref, qda_ref[...], sd_ref, cd_ref)
        buffer_scores(0, kfa_ref, qf_aug, sf_ref, cf_ref)

    @pl.when(qi == 0)
    def _():
        prepare_keys_values()
        prepare_queries(qd_ref, qf_ref, 0)

    branches = ((kda_ref, qda_ref, vdt_ref, sd_ref, pd_ref, ad_ref, md_ref, accd_ref, cd_ref),
                (kfa_ref, qfa_ref, vft_ref, sf_ref, pf_ref, af_ref, mf_ref, accf_ref, cf_ref))

    def pipeline_step(j_prev, adds, j_next, next_adds=(None, None)):
        for (k_ref, q_ref, vt_ref, s_ref, p_ref, a_ref, m_ref, acc_ref, c_ref), add, next_add in zip(
                branches, adds, next_adds):
            acc_ref[...] = a_ref[...] * acc_ref[...] + _dot(vt_ref[j_prev], p_ref[...])
            s = s_ref[...]
            if add is None:
                s_max = c_ref[...]
            else:
                s = s + add
                s_max = jnp.max(s, axis=0, keepdims=True)
            m_prev = m_ref[...]
            m_new = jnp.maximum(m_prev, s_max)
            p_ref[...] = jnp.exp2(s - m_new).astype(BF16)
            a_ref[...] = jnp.exp2(m_prev - m_new)
            m_ref[...] = m_new
            if j_next is not None:
                buffer_scores(j_next, k_ref, q_ref[...], s_ref, c_ref, next_add)

    for k_ref, q_ref, vt_ref, s_ref, p_ref, a_ref, m_ref, acc_ref, c_ref in branches:
        m_ref[...] = jnp.full_like(m_ref, NEG_INF)
        acc_ref[...] = jnp.zeros_like(acc_ref)
        p_ref[...] = jnp.zeros_like(p_ref)
        a_ref[...] = jnp.ones_like(a_ref)

    def far_tiles(unroll):
        def body(i, carry):
            for u in range(unroll):
                j = i * unroll + u
                pipeline_step(jnp.maximum(j - 1, 0), (None, None), j + 1)
            return carry
        return body

    first_diag = qi * ratio
    n_far = jnp.maximum(first_diag - 1, 0)
    n_main = n_far // FAR_UNROLL
    lax.fori_loop(0, n_main, far_tiles(FAR_UNROLL), 0)
    lax.fori_loop(n_main * FAR_UNROLL, n_far, far_tiles(1), 0)

    near = jnp.maximum(first_diag - 1, 0)
    gone = jnp.where(qi == 0, NEG_INF, 0.0)

    def diag_adds(u):
        bias = bias_ref[u + 1]
        return jnp.concatenate([bias, bias], axis=1), cmask_ref[u]

    bias = bias_ref[0] + gone
    pipeline_step(jnp.maximum(near - 1, 0), (jnp.concatenate([bias, bias], axis=1), gone),
                  first_diag, diag_adds(0))
    for u in range(ratio):
        more = u + 1 < ratio
        pipeline_step(near if u == 0 else first_diag + u - 1, (None, None),
                      first_diag + u + 1 if more else None, diag_adds(u + 1) if more else (None, None))
    last = first_diag + ratio - 1
    for k_ref, q_ref, vt_ref, s_ref, p_ref, a_ref, m_ref, acc_ref, c_ref in branches:
        acc_ref[...] = a_ref[...] * acc_ref[...] + _dot(vt_ref[last], p_ref[...])

    prepare_queries(qdn_ref, qfn_ref, jnp.minimum(qi + 1, pl.num_programs(2) - 1))

    lam = _lambda_value(lam_ref, lam_init)
    acc = accd_ref[...]
    num, inv = acc[:HEAD_DIM], 1.0 / acc[HEAD_DIM:HEAD_DIM + 1]
    odt = num[:, :tq] * inv[:, :tq] - lam * (num[:, tq:] * inv[:, tq:])
    od_ref[...] = (_rms_rows(odt.T, subln_ref[...]) * (1.0 - lam_init)).astype(BF16)
    acc = accf_ref[...]
    of_ref[...] = (acc[:HEAD_DIM] * (1.0 / acc[HEAD_DIM:HEAD_DIM + 1])).T.astype(BF16)


def _prompt_attention(zb, fcols, rel_table, lam_params, subln, lam_init, batch, seq, n_heads, tq, tk):
    m = batch * seq
    dw = n_heads * HEAD_DIM
    nq = seq // tq
    ratio = tq // tk
    assert tq == ratio * tk
    far_bucket = _far_bucket(tk + 1)
    vrows = HEAD_DIM + ONES_ROWS
    qpos = tk + jnp.arange(tq, dtype=jnp.int32)
    kpos = jnp.arange((ratio + 1) * tk, dtype=jnp.int32).reshape(ratio + 1, tk)
    bias = _bias_tiles(rel_table, jnp.broadcast_to(qpos, (ratio + 1, tq)), kpos,
                       keys_major=True, minus_bucket=far_bucket)
    cmask = jnp.where(kpos[1:, :, None] <= qpos[None, None, :], 0.0, NEG_INF).astype(F32)

    def q_spec(seg, ahead=0):
        return pl.BlockSpec((tq, HEAD_DIM), lambda b, h, q, seg=seg: (b * nq + jnp.minimum(q + ahead, nq - 1),
                                                                      seg * n_heads + h))

    def kv_spec(seg):
        return pl.BlockSpec((seq, HEAD_DIM), lambda b, h, q, seg=seg: (b, seg * n_heads + h))

    out_spec = pl.BlockSpec((tq, HEAD_DIM), lambda b, h, q: (b * nq + q, h))
    const2 = lambda b, h, q: (0, 0)
    kern = functools.partial(_prompt_attn_kernel, tq=tq, tk=tk, lam_init=lam_init, far_bucket=far_bucket)
    return pl.pallas_call(
        kern,
        grid=(batch, n_heads, nq),
        in_specs=[
            pl.BlockSpec(memory_space=pltpu.SMEM),
            pl.BlockSpec((4, DIFF_QK_DIM), const2),
            pl.BlockSpec((1, HEAD_DIM), const2),
            q_spec(0), kv_spec(1), kv_spec(2), q_spec(3), kv_spec(4), kv_spec(5),
            pl.BlockSpec((None, None, seq, LANES), lambda b, h, q: (b, h, 0, 0)),
            pl.BlockSpec((None, ratio + 1, tk, tq), lambda b, h, q: (h, 0, 0, 0)),
            pl.BlockSpec((ratio, tk, tq), lambda b, h, q: (0, 0, 0)),
            q_spec(0, ahead=1), q_spec(3, ahead=1),
        ],
        out_specs=[out_spec, out_spec],
        out_shape=[jax.ShapeDtypeStruct((m, dw), BF16), jax.ShapeDtypeStruct((m, dw), BF16)],
        scratch_shapes=[
            pltpu.VMEM((seq, 2 * HEAD_DIM), BF16), pltpu.VMEM((seq, 2 * HEAD_DIM), BF16),
            pltpu.VMEM((seq // tk, vrows, tk), BF16), pltpu.VMEM((seq // tk, vrows, tk), BF16),
            pltpu.VMEM((2 * HEAD_DIM, 2 * tq), BF16), pltpu.VMEM((2 * HEAD_DIM, tq), BF16),
            pltpu.VMEM((1, 2 * tq), F32), pltpu.VMEM((vrows, 2 * tq), F32),
            pltpu.VMEM((1, tq), F32), pltpu.VMEM((vrows, tq), F32),
            pltpu.VMEM((tk, 2 * tq), F32), pltpu.VMEM((tk, 2 * tq), BF16), pltpu.VMEM((1, 2 * tq), F32),
            pltpu.VMEM((tk, tq), F32), pltpu.VMEM((tk, tq), BF16), pltpu.VMEM((1, tq), F32),
            pltpu.VMEM((1, 2 * tq), F32), pltpu.VMEM((1, tq), F32),
        ],
        compiler_params=_params("arbitrary", "arbitrary", "arbitrary"),
        name="prompt_attention",
    )(rel_table.astype(F32), lam_params, subln, zb, zb, zb, zb, zb, zb, fcols, bias, cmask, zb, zb)


def _split_diff_queries(q):
    lane = lax.broadcasted_iota(jnp.int32, q.shape, 1)
    zero = jnp.zeros_like(q)
    return jnp.concatenate([jnp.where(lane < DIFF_QK_DIM, q, zero),
                            jnp.where(lane >= DIFF_QK_DIM, q, zero)], axis=0)


def _sample_attn_kernel(lam_ref, subln_ref, z_ref, kdc_hbm, vdc_hbm, kfc_hbm, vfc_hbm, frow_ref, bias_ref,
                        cmask_ref, od_ref, of_ref, kdc_buf, vdc_buf, kfc_buf, vfc_buf, sem,
                        *, past, t_new, n_heads, lam_init, layer):
    dw = n_heads * HEAD_DIM
    b = pl.program_id(0)
    nb = pl.num_programs(0)
    streams = ((kdc_hbm, kdc_buf), (vdc_hbm, vdc_buf), (kfc_hbm, kfc_buf), (vfc_hbm, vfc_buf))

    def cache_copy(which, stream):
        hbm, buf = streams[which]
        slot = stream % CACHE_BUFFERS
        return pltpu.make_async_copy(hbm.at[layer, stream], buf.at[slot], sem.at[which, slot])

    def start_stream(stream):
        for which in range(len(streams)):
            cache_copy(which, stream).start()

    @pl.when(b == 0)
    def _():
        for ahead in range(CACHE_BUFFERS - 1):
            @pl.when(ahead < nb)
            def _(ahead=ahead):
                start_stream(ahead)

    @pl.when(b + CACHE_BUFFERS - 1 < nb)
    def _():
        start_stream(b + CACHE_BUFFERS - 1)

    for which in range(len(streams)):
        cache_copy(which, b).wait()
    slot = b % CACHE_BUFFERS
    kdc_ref, vdc_ref, kfc_ref, vfc_ref = (buf.at[slot] for _, buf in streams)

    def cached(ref, h):
        return ref[pl.ds(h, past, stride=n_heads), :].astype(BF16)

    def new_rows(seg, h):
        lo = seg * dw + h * HEAD_DIM
        return z_ref[:, lo:lo + HEAD_DIM]

    def attend(q, kc, vc, kn, vn, add_c, add_n):
        sc = _dot_nt(q, kc) + add_c
        sn = _dot_nt(q, kn) + add_n
        mx = jnp.maximum(jnp.max(sc, axis=-1, keepdims=True), jnp.max(sn, axis=-1, keepdims=True))
        pc = jnp.exp2(sc - mx)
        pn = jnp.exp2(sn - mx)
        l = jnp.sum(pc, axis=-1, keepdims=True) + jnp.sum(pn, axis=-1, keepdims=True)
        acc = _dot(pc.astype(BF16), vc) + _dot(pn.astype(BF16), vn)
        return acc, l

    lam = _lambda_value(lam_ref, lam_init)
    cmask = cmask_ref[...]
    eye = (lax.broadcasted_iota(jnp.int32, (t_new, t_new), 0)
           == lax.broadcasted_iota(jnp.int32, (t_new, t_new), 1))
    for h in range(n_heads):
        cols = slice(h * HEAD_DIM, (h + 1) * HEAD_DIM)
        qs = _split_diff_queries(new_rows(0, h))
        bias_c = bias_ref[h, :, :past]
        bias_n = bias_ref[h, :, past:past + t_new]
        acc, l = attend(qs, cached(kdc_ref, h), cached(vdc_ref, h),
                        new_rows(1, h), new_rows(2, h),
                        jnp.concatenate([bias_c, bias_c], axis=0), jnp.concatenate([bias_n, bias_n], axis=0))
        od = acc[:t_new] / l[:t_new] - lam * (acc[t_new:] / l[t_new:])
        od_ref[:, cols] = (_rms_rows(od, subln_ref[...]) * (1.0 - lam_init)).astype(BF16)

        f_new = frow_ref[h:h + 1, past:past + t_new]
        fq = jnp.sum(jnp.where(eye, jnp.broadcast_to(f_new, (t_new, t_new)), 0.0), axis=1, keepdims=True)
        dec_c = (fq - frow_ref[h:h + 1, :past]) * LOG2E
        dec_n = (fq - f_new) * LOG2E
        acc, l = attend(new_rows(3, h), cached(kfc_ref, h), cached(vfc_ref, h),
                        new_rows(4, h), new_rows(5, h), dec_c, dec_n + cmask)
        of_ref[:, cols] = (acc / l).astype(BF16)


def _sample_attention(zb, caches, layer, frow, rel_table, lam_params, subln, lam_init, nb, t_new, past, n_heads):
    dw = n_heads * HEAD_DIM
    tpad = frow.shape[-1]
    ar = jnp.arange(t_new, dtype=jnp.int32)
    bias = _bias_tiles(rel_table, (past + ar)[None], jnp.arange(tpad, dtype=jnp.int32)[None])[:, 0]
    cmask = jnp.where(ar[None, :] <= ar[:, None], 0.0, NEG_INF).astype(F32)
    caches = [c.reshape(c.shape[0], nb, past * n_heads, HEAD_DIM) for c in caches]
    cache_spec = pl.BlockSpec(memory_space=pl.ANY)
    out_spec = pl.BlockSpec((t_new, dw), lambda b: (b, 0))
    const2 = lambda b: (0, 0)
    kern = functools.partial(_sample_attn_kernel, past=past, t_new=t_new, n_heads=n_heads, lam_init=lam_init,
                             layer=layer)
    ring = pltpu.VMEM((CACHE_BUFFERS, past * n_heads, HEAD_DIM), F32)
    return pl.pallas_call(
        kern,
        grid=(nb,),
        scratch_shapes=[ring, ring, ring, ring, pltpu.SemaphoreType.DMA((4, CACHE_BUFFERS))],
        in_specs=[
            pl.BlockSpec((4, DIFF_QK_DIM), const2),
            pl.BlockSpec((1, HEAD_DIM), const2),
            pl.BlockSpec((t_new, 6 * dw), lambda b: (b, 0)),
            cache_spec, cache_spec, cache_spec, cache_spec,
            pl.BlockSpec((None, n_heads, tpad), lambda b: (b, 0, 0)),
            pl.BlockSpec((n_heads, t_new, tpad), lambda b: (0, 0, 0)),
            pl.BlockSpec((t_new, t_new), const2),
        ],
        out_specs=[out_spec, out_spec],
        out_shape=[jax.ShapeDtypeStruct((nb * t_new, dw), BF16), jax.ShapeDtypeStruct((nb * t_new, dw), BF16)],
        compiler_params=_params("arbitrary"),
        name="sample_attention",
    )(lam_params, subln, zb, *caches, frow, bias, cmask)


def _merge_kernel(x_ref, g1_ref, od_ref, of_ref, wbd_ref, wbf_ref, wga_ref, wgb_ref, wout_ref, gp_ref,
                  o_ref, *, chunk):
    x = x_ref[...]
    xn = _rms_rows(x, g1_ref[...]).astype(BF16)
    od = od_ref[...]
    of = of_ref[...]
    o = None
    for c in range(x.shape[1] // chunk):
        cols = slice(c * chunk, (c + 1) * chunk)
        ga = jax.nn.sigmoid(_dot(xn, wga_ref[:, cols]))
        gb = jax.nn.sigmoid(_dot(xn, wgb_ref[:, cols]))
        u = (ga * _dot(od, wbd_ref[:, cols]) + gb * _dot(of, wbf_ref[:, cols])).astype(BF16)
        part = _dot(u, wout_ref[cols, :])
        o = part if o is None else o + part
    o_ref[...] = x + _rms_rows(o, gp_ref[...])


def _merge(x, g1, od, of, w_bd, w_bf, w_ga, w_gb, w_out, g_post, tm, chunk):
    m, d = x.shape
    dw = od.shape[1]
    row = lambda i: (i, 0)
    const2 = lambda i: (0, 0)
    resident = lambda shape: pl.BlockSpec(shape, const2, pipeline_mode=pl.Buffered(1))
    return pl.pallas_call(
        functools.partial(_merge_kernel, chunk=chunk),
        grid=(m // tm,),
        in_specs=[
            pl.BlockSpec((tm, d), row),
            pl.BlockSpec((1, d), const2),
            pl.BlockSpec((tm, dw), row),
            pl.BlockSpec((tm, dw), row),
            resident((dw, d)), resident((dw, d)), resident((d, d)), resident((d, d)), resident((d, d)),
            pl.BlockSpec((1, d), const2),
        ],
        out_specs=pl.BlockSpec((tm, d), row),
        out_shape=jax.ShapeDtypeStruct((m, d), F32),
        compiler_params=_params("arbitrary"),
        name="merge",
    )(x, g1, od, of, w_bd, w_bf, w_ga, w_gb, w_out, g_post)


def _gelu_tanh(x):
    k = -2.0 * math.sqrt(2.0 / math.pi) * LOG2E
    return x / (1.0 + jnp.exp2(x * (k + (k * 0.044715) * (x * x))))


def _ffn_kernel(*refs, seq, has_edges):
    if has_edges:
        (h_ref, g2_ref, wa_ref, wb_ref, cw_ref, cb_ref, wd_ref, gp_ref, prev_ref,
         o_ref, cs_ref, xn_ref, tail_ref) = refs
    else:
        (h_ref, g2_ref, wa_ref, wb_ref, cw_ref, cb_ref, wd_ref, gp_ref,
         o_ref, cs_ref, xn_ref, tail_ref) = refs
    i = pl.program_id(0)
    f = pl.program_id(1)
    tm = h_ref.shape[0]

    tf = wa_ref.shape[1]

    @pl.when(f == 0)
    def _():
        xn_ref[...] = _rms_rows(h_ref[...], g2_ref[...]).astype(BF16)
        o_ref[...] = jnp.zeros_like(o_ref)

    if not has_edges:
        @pl.when((i * tm) % seq == 0)
        def _():
            tail_ref[f] = jnp.zeros(tail_ref.shape[1:], F32)

    xn = xn_ref[...]

    def gated_chunk(cols):
        width = cols.stop - cols.start
        a = _dot(xn, wa_ref[:, cols])
        gate = _dot(xn, wb_ref[:, cols])
        back1 = pltpu.roll(a, 1, 0)
        back2 = pltpu.roll(a, 2, 0)
        if has_edges:
            shape3 = (tm // seq, seq, width)
            t = lax.broadcasted_iota(jnp.int32, shape3, 1)
            prev = prev_ref[:, :, cols]
            p0, p1 = prev[:, 0:1, :], prev[:, 1:2, :]
            am1 = jnp.where(t == 0, p1, back1.reshape(shape3)).reshape(tm, width)
            am2 = jnp.where(t == 0, p0, jnp.where(t == 1, p1, back2.reshape(shape3))).reshape(tm, width)
            cs_ref[:, :, cols] = a.reshape(shape3)[:, seq - (CONV_WIDTH - 1):, :]
        else:
            prev = tail_ref[f, :, cols]
            p0 = prev[SUBLANES - 2:SUBLANES - 1]
            p1 = prev[SUBLANES - 1:SUBLANES]
            top = lax.broadcasted_iota(jnp.int32, (SUBLANES, width), 0)
            am1 = jnp.concatenate([jnp.where(top == 0, p1, back1[:SUBLANES]), back1[SUBLANES:]], axis=0)
            am2 = jnp.concatenate([jnp.where(top == 0, p0, jnp.where(top == 1, p1, back2[:SUBLANES])),
                                   back2[SUBLANES:]], axis=0)
            tail_ref[f, :, cols] = a[tm - SUBLANES:, :]
            cs_ref[:, :, cols] = a[tm - (CONV_WIDTH - 1):, :][None]
        cw = cw_ref[:, cols]
        ac = cw[0:1] * am2 + cw[1:2] * am1 + cw[2:3] * a + cb_ref[:, cols]
        return (_gelu_tanh(ac) * gate).astype(BF16)

    width = tf // FFN_CHUNKS
    contrib = None
    for c in range(FFN_CHUNKS):
        cols = slice(c * width, (c + 1) * width)
        part = _dot(gated_chunk(cols), wd_ref[cols, :])
        contrib = part if contrib is None else contrib + part
    o_ref[...] += contrib

    @pl.when(f == pl.num_programs(1) - 1)
    def _():
        o_ref[...] = h_ref[...] + _rms_rows(o_ref[...], gp_ref[...])


def _ffn(h, g2, w_up, conv_w, conv_b, w_d, g_post, edges, seq, tm, tf):
    m, d = h.shape
    dff = w_d.shape[0]
    nf = dff // tf
    has_edges = edges is not None
    row = lambda i, f: (i, 0)
    col = lambda i, f: (0, f)
    const2 = lambda i, f: (0, 0)
    in_specs = [
        pl.BlockSpec((tm, d), row),
        pl.BlockSpec((1, d), const2),
        pl.BlockSpec((d, tf), col),
        pl.BlockSpec((d, tf), lambda i, f: (0, nf + f)),
        pl.BlockSpec((CONV_WIDTH, tf), col),
        pl.BlockSpec((1, tf), col),
        pl.BlockSpec((tf, d), lambda i, f: (f, 0)),
        pl.BlockSpec((1, d), const2),
    ]
    args = [h, g2, w_up, w_up, conv_w, conv_b, w_d, g_post]
    if has_edges:
        assert tm % seq == 0
        tails_per_tile = tm // seq
        in_specs.append(pl.BlockSpec((tails_per_tile, CONV_WIDTH - 1, tf), lambda i, f: (i, 0, f)))
        args.append(edges)
    else:
        assert seq % tm == 0
        tails_per_tile = 1
    n_tails = (m // tm) * tails_per_tile
    out, tails = pl.pallas_call(
        functools.partial(_ffn_kernel, seq=seq, has_edges=has_edges),
        grid=(m // tm, nf),
        in_specs=in_specs,
        out_specs=[pl.BlockSpec((tm, d), row),
                   pl.BlockSpec((tails_per_tile, CONV_WIDTH - 1, tf), lambda i, f: (i, 0, f))],
        out_shape=[jax.ShapeDtypeStruct((m, d), F32),
                   jax.ShapeDtypeStruct((n_tails, CONV_WIDTH - 1, dff), F32)],
        scratch_shapes=[pltpu.VMEM((tm, d), BF16), pltpu.VMEM((nf, SUBLANES, tf), F32)],
        compiler_params=_params("arbitrary", "arbitrary"),
        name="conv_ffn",
    )(*args)
    if not has_edges:
        tiles_per_seq = seq // tm
        tails = tails[tiles_per_seq - 1::tiles_per_seq]
    return out, tails


def _tile(n, cap):
    t = min(n, cap)
    assert n % t == 0
    return t


def _layer(layer, hp, hs, past, rel_table, lam_params, lam_init, p):
    (pre1, w_in, b_forget, subln, w_bd, w_bf, w_out, post1, pre2, w_up, conv_w, conv_b, w_down, post2) = p
    batch, seq, d = hp.shape
    nb, t_new, _ = hs.shape
    n_heads = d // (2 * HEAD_DIM)
    dw = n_heads * HEAD_DIM
    dff = w_down.shape[0]
    caches, plogf, conv_prev = past
    plen = plogf.shape[1]

    row2 = lambda v: v.reshape(1, -1).astype(F32)
    w_in_b = w_in.astype(BF16)
    b_f = jnp.pad(b_forget.astype(F32), (0, LANES - n_heads)).reshape(1, LANES)
    w_ga = w_in_b[:, 6 * dw + n_heads:6 * dw + n_heads + d]
    w_gb = w_in_b[:, 6 * dw + n_heads + d:]
    w_bd_b, w_bf_b, w_out_b = w_bd.astype(BF16), w_bf.astype(BF16), w_out.astype(BF16)
    w_up_b = w_up.astype(BF16)
    w_d = w_down.astype(BF16)
    subln2 = row2(subln)

    def dense_tail(x2, od, of, edges, t_seq):
        m = x2.shape[0]
        h1 = _merge(x2, row2(pre1), od, of, w_bd_b, w_bf_b, w_ga, w_gb, w_out_b, row2(post1),
                    _tile(m, MERGE_ROWS), _tile(d, 1024))
        return _ffn(h1, row2(pre2), w_up_b, conv_w.astype(F32), row2(conv_b), w_d, row2(post2),
                    edges, t_seq, _tile(m, 512), _tile(dff, 1024))

    heads = lambda a, n, t: a.reshape(n, t, n_heads, HEAD_DIM)

    xp = hp.reshape(batch * seq, d)
    zb, kd, vd, kf, vf, logf, fcols = _inproj(xp, row2(pre1), w_in_b, b_f, n_heads,
                                              _tile(seq, INPROJ_ROWS), seq=seq)
    od, of = _prompt_attention(zb, fcols, rel_table, lam_params, subln2, lam_init,
                               batch, seq, n_heads, _tile(seq, ATTN_QUERY_TILE), _tile(seq, ATTN_KEY_TILE))
    hp_out, conv_p = dense_tail(xp, od, of, None, seq)
    state_p = (heads(kd, batch, seq), heads(vd, batch, seq), heads(kf, batch, seq), heads(vf, batch, seq),
               logf.reshape(batch, seq, n_heads), conv_p)

    xs = hs.reshape(nb * t_new, d)
    zb, kd, vd, kf, vf, logf = _inproj(xs, row2(pre1), w_in_b, b_f, n_heads,
                                       _tile(nb * t_new, INPROJ_ROWS))
    tpad = -(-(plen + t_new) // LANES) * LANES
    flog = jnp.concatenate([jnp.swapaxes(plogf.astype(F32), 1, 2),
                            jnp.swapaxes(logf.reshape(nb, t_new, n_heads), 1, 2),
                            jnp.zeros((nb, n_heads, tpad - plen - t_new), F32)], axis=2)
    frow = _cumsum_rows(flog.reshape(nb * n_heads, tpad)).reshape(nb, n_heads, tpad)
    od, of = _sample_attention(zb, caches, layer, frow, rel_table, lam_params, subln2, lam_init,
                               nb, t_new, plen, n_heads)
    hs_out, conv_s = dense_tail(xs, od, of, conv_prev.astype(F32), t_new)
    state_s = (heads(kd, nb, t_new), heads(vd, nb, t_new), heads(kf, nb, t_new), heads(vf, nb, t_new),
               logf.reshape(nb, t_new, n_heads), conv_s)
    return hp_out.reshape(batch, seq, d), hs_out.reshape(nb, t_new, d), state_p, state_s


def kernel(x_prompt, x_sample, cache_diff_k, cache_diff_v, cache_fox_k, cache_fox_v, cache_fox_logf,
           state_ffn_conv, rel_table, pre_norm1, w_in, b_forget, lam_q1, lam_k1, lam_q2, lam_k2,
           diff_subln, w_branch_diff, w_branch_fox, w_out, post_norm1, pre_norm2, w_up, conv_w, conv_b,
           w_down, post_norm2):
    depth = w_in.shape[0]
    hp, hs = x_prompt, x_sample
    caches = (cache_diff_k, cache_diff_v, cache_fox_k, cache_fox_v)
    new_p, new_s = [], []
    for l in range(depth):
        lam_init = 0.8 - 0.6 * math.exp(-0.3 * l)
        lam_params = jnp.stack([lam_q1[l], lam_k1[l], lam_q2[l], lam_k2[l]]).astype(F32)
        params = (pre_norm1[l], w_in[l], b_forget[l], diff_subln[l], w_branch_diff[l], w_branch_fox[l],
                  w_out[l], post_norm1[l], pre_norm2[l], w_up[l], conv_w[l], conv_b[l], w_down[l],
                  post_norm2[l])
        past = (caches, cache_fox_logf[l], state_ffn_conv[l])
        hp, hs, sp, ss = _layer(l, hp, hs, past, rel_table, lam_params, lam_init, params)
        new_p.append(sp)
        new_s.append(ss)
    st = lambda lst, i: jnp.stack([e[i] for e in lst])
    return (hp, hs,
            st(new_p, 0), st(new_p, 1), st(new_p, 2), st(new_p, 3), st(new_p, 4), st(new_p, 5),
            st(new_s, 0), st(new_s, 1), st(new_s, 2), st(new_s, 3), st(new_s, 4), st(new_s, 5))
```
